```python
import jax, jax.numpy as jnp
from jax import lax
import numpy as np

D_MODEL = 2048
BATCH = 1
SEQ = 16384
DEPTH = 1

N_HEADS_SWA = 8
HEAD_DIM_SWA = 128
SWA_PATTERNS = ((128, 1), (512, 4), (2048, 16))
SWA_BLOCK = 128
N_HEADS_MLA = 8
Q_LORA_RANK = 512
KV_LORA_RANK = 256
QK_NOPE_DIM = 128
QK_ROPE_DIM = 64
V_HEAD_DIM = 128
ROPE_THETA = 10000.0
Q_BLOCK = 128
D_SWA = N_HEADS_SWA * HEAD_DIM_SWA
D_MLA = N_HEADS_MLA * V_HEAD_DIM
D_MIX = D_SWA + D_MLA
D_IN = 3 * D_SWA + Q_LORA_RANK + KV_LORA_RANK + QK_ROPE_DIM
N_EXPERTS = 64
N_GROUPS = 8
TOPK_GROUPS = 4
TOP_K = 8
D_EXPERT = 512
ROUTED_SCALE = 2.5
MOE_BLOCK = 128
N_ADA = 6
EPS = 1e-6
NEG_INF = -1e30

kernel_name = 'hybrid_dilated_mla_moe_adaln_layer'


def rms_norm(x, g):
    xf = x.astype(jnp.float32)
    y = xf * lax.rsqrt(jnp.mean(xf * xf, axis=-1, keepdims=True) + EPS)
    return (y * g.astype(jnp.float32)).astype(x.dtype)


def swiglu(x, w_gate, w_up, w_down):
    return (jax.nn.silu(x @ w_gate) * (x @ w_up)) @ w_down


def rope_tables(positions):
    half = QK_ROPE_DIM // 2
    inv_freq = ROPE_THETA ** (-jnp.arange(half, dtype=jnp.float32) / half)
    ang = positions.astype(jnp.float32)[..., None] * inv_freq
    return jnp.cos(ang), jnp.sin(ang)


def apply_rope(x, cos, sin):
    x1, x2 = jnp.split(x.astype(jnp.float32), 2, axis=-1)
    return jnp.concatenate([x1 * cos - x2 * sin, x2 * cos + x1 * sin], axis=-1).astype(x.dtype)


def dilated_window_pattern(q, k, v, pos, window, dilation, slopes):
    B, S, H, Dh = q.shape
    seg = dilation * SWA_BLOCK
    L = -(-S // seg) * seg
    pad = L - S
    nb = L // seg
    n_window = window // dilation

    def pad_seq(a):
        return jnp.pad(a, [(0, 0), (0, pad)] + [(0, 0)] * (a.ndim - 2))

    def strided(a):
        rest = a.shape[2:]
        a = a.reshape((B, L // dilation, dilation) + rest)
        a = jnp.moveaxis(a, 2, 1)
        return a.reshape((B, dilation, nb, SWA_BLOCK) + rest)

    def band(a):
        prev = jnp.pad(a[:, :, :-1], [(0, 0), (0, 0), (1, 0)] + [(0, 0)] * (a.ndim - 3))
        return jnp.concatenate([prev, a], axis=3)

    def unstrided(a):
        rest = a.shape[4:]
        a = a.reshape((B, dilation, L // dilation) + rest)
        a = jnp.moveaxis(a, 1, 2).reshape((B, L) + rest)
        return a[:, :S]

    posf = jnp.pad(pos.astype(jnp.float32), ((0, 0), (0, pad)), mode='edge')
    qb = strided(pad_seq(q))
    kb = band(strided(pad_seq(k)))
    vb = band(strided(pad_seq(v)))
    pq = strided(posf)
    pk = band(strided(posf))

    s = jnp.einsum('brnqhd,brnkhd->brnhqk', qb, kb,
                   preferred_element_type=jnp.float32) * (Dh ** -0.5)
    dist = jnp.abs(pq[..., :, None] - pk[..., None, :])
    s = s - slopes[:, None, None] * dist[:, :, :, None]
    i = jnp.arange(SWA_BLOCK)[:, None] + SWA_BLOCK
    j = jnp.arange(2 * SWA_BLOCK)[None, :]
    rel = i - j
    band_ok = (rel >= 0) & (rel <= n_window)
    first_ok = (jnp.arange(nb)[:, None, None] > 0) | (j[None] >= SWA_BLOCK)
    valid = band_ok[None] & first_ok
    s = jnp.where(valid[:, None], s, NEG_INF)
    m = jnp.max(s, axis=-1, keepdims=True)
    p = jnp.exp(s - m)
    den = jnp.sum(p, axis=-1)
    o = jnp.einsum('brnhqk,brnkhd->brnqhd', p, vb.astype(jnp.float32))
    o = o / jnp.swapaxes(den, 3, 4)[..., None]
    lse = jnp.swapaxes(m[..., 0] + jnp.log(den), 3, 4)
    return unstrided(o), unstrided(lse)


def dilated_attention(q, k, v, pos):
    H = q.shape[2]
    slopes = jnp.exp2(-8.0 * jnp.arange(1, H + 1, dtype=jnp.float32) / H)
    outs, lses = [], []
    for window, dilation in SWA_PATTERNS:
        o, lse = dilated_window_pattern(q, k, v, pos, window, dilation, slopes)
        outs.append(o)
        lses.append(lse)
    alpha = jax.nn.softmax(jnp.stack(lses, axis=0), axis=0)
    o = jnp.einsum('pbsh,pbshd->bshd', alpha, jnp.stack(outs, axis=0))
    return o.astype(q.dtype)


def causal_block_attention(q, k, v):
    B, S, H, Dk = q.shape
    nq = S // Q_BLOCK
    qb = jnp.moveaxis(q.reshape(B, nq, Q_BLOCK, H, Dk), 1, 0)
    kpos = jnp.arange(S)

    def one_block(args):
        qblk, bi = args
        s = jnp.einsum('bqhd,bkhd->bhqk', qblk, k,
                       preferred_element_type=jnp.float32) * (Dk ** -0.5)
        qpos = bi * Q_BLOCK + jnp.arange(Q_BLOCK)
        s = jnp.where(kpos[None, :] <= qpos[:, None], s, NEG_INF)
        p = jax.nn.softmax(s, axis=-1)
        return jnp.einsum('bhqk,bkhd->bqhd', p.astype(v.dtype), v)

    ob = lax.map(one_block, (qb, jnp.arange(nq)))
    return jnp.moveaxis(ob, 0, 1).reshape(B, S, H, v.shape[-1])


def latent_attention(c_q, c_kv, k_rope, positions, g_q, w_uq, g_kv, w_ukv):
    B, S, _ = c_q.shape
    q = (rms_norm(c_q, g_q) @ w_uq).reshape(B, S, N_HEADS_MLA, QK_NOPE_DIM + QK_ROPE_DIM)
    kv = (rms_norm(c_kv, g_kv) @ w_ukv).reshape(B, S, N_HEADS_MLA, QK_NOPE_DIM + V_HEAD_DIM)
    q_nope, q_rope = jnp.split(q, [QK_NOPE_DIM], axis=-1)
    k_nope, v = jnp.split(kv, [QK_NOPE_DIM], axis=-1)
    cos, sin = rope_tables(positions)
    q_rope = apply_rope(q_rope, cos[:, :, None], sin[:, :, None])
    k_rope = apply_rope(k_rope, cos, sin)[:, :, None, :]
    q = jnp.concatenate([q_nope, q_rope], axis=-1)
    k = jnp.concatenate([k_nope, jnp.broadcast_to(k_rope, k_nope.shape[:-1] + (QK_ROPE_DIM,))], axis=-1)
    return causal_block_attention(q, k, v)


def routed_experts(xt, eidx, wts, w_gate, w_up, w_down):
    N, D = xt.shape
    A = N * TOP_K
    flat_e = eidx.reshape(A)
    order = jnp.argsort(flat_e)
    sorted_e = flat_e[order]
    sorted_tok = (order // TOP_K).astype(jnp.int32)
    sorted_w = wts.reshape(A)[order]
    counts = jnp.bincount(flat_e, length=N_EXPERTS)
    padded = (counts + MOE_BLOCK - 1) // MOE_BLOCK * MOE_BLOCK
    seg_start = jnp.cumsum(counts) - counts
    pad_end = jnp.cumsum(padded)
    pad_start = pad_end - padded
    dest = pad_start[sorted_e] + jnp.arange(A) - seg_start[sorted_e]
    n_slots = -(-A // MOE_BLOCK) * MOE_BLOCK + N_EXPERTS * MOE_BLOCK
    n_blocks = n_slots // MOE_BLOCK
    slot_tok = jnp.full((n_slots,), N, jnp.int32).at[dest].set(sorted_tok)
    slot_w = jnp.zeros((n_slots,), jnp.float32).at[dest].set(sorted_w)
    blk_e = jnp.minimum(jnp.searchsorted(pad_end, jnp.arange(n_blocks) * MOE_BLOCK, side='right'),
                        N_EXPERTS - 1)
    x_pad = jnp.concatenate([xt, jnp.zeros((1, D), xt.dtype)], axis=0)

    def body(y, args):
        tok, wt, e = args
        yb = swiglu(x_pad[tok], w_gate[e], w_up[e], w_down[e])
        return y.at[tok].add(yb * wt[:, None].astype(yb.dtype)), None

    y, _ = lax.scan(body, jnp.zeros_like(x_pad),
                    (slot_tok.reshape(n_blocks, MOE_BLOCK), slot_w.reshape(n_blocks, MOE_BLOCK), blk_e))
    return y[:N]


def moe_ffn(h, w_router, router_bias, w_exp_gate, w_exp_up, w_exp_down, w_sh_gate, w_sh_up, w_sh_down):
    B, S, D = h.shape
    N = B * S
    xt = h.reshape(N, D)
    logits = jnp.einsum('nd,de->ne', xt, w_router, preferred_element_type=jnp.float32)
    scores = jax.nn.sigmoid(logits)
    choice = scores + router_bias.astype(jnp.float32)
    grp = choice.reshape(N, N_GROUPS, N_EXPERTS // N_GROUPS)
    grp_score = jnp.sum(lax.top_k(grp, 2)[0], axis=-1)
    _, gidx = lax.top_k(grp_score, TOPK_GROUPS)
    gmask = jnp.sum(jax.nn.one_hot(gidx, N_GROUPS, dtype=jnp.float32), axis=1)
    emask = jnp.repeat(gmask, N_EXPERTS // N_GROUPS, axis=1) > 0
    _, eidx = lax.top_k(jnp.where(emask, choice, NEG_INF), TOP_K)
    w = jnp.take_along_axis(scores, eidx, axis=-1)
    w = w / jnp.sum(w, axis=-1, keepdims=True) * ROUTED_SCALE
    routed = routed_experts(xt, eidx, w, w_exp_gate, w_exp_up, w_exp_down)
    shared = swiglu(xt, w_sh_gate, w_sh_up, w_sh_down)
    return (routed + shared).reshape(B, S, D)


def hybrid_layer(x, c, positions, norm_attn_g, w_ada, b_ada, w_in, g_q, w_uq, g_kv, w_ukv,
                 g_out_swa, g_out_mla, w_o, norm_ffn_g, w_router, router_bias,
                 w_exp_gate, w_exp_up, w_exp_down, w_sh_gate, w_sh_up, w_sh_down):
    B, S, _ = x.shape
    mod = jnp.einsum('bd,de->be', jax.nn.silu(c), w_ada) + b_ada
    shift_a, scale_a, gate_a, shift_f, scale_f, gate_f = [m[:, None, :] for m in jnp.split(mod, N_ADA, axis=-1)]

    h = rms_norm(x, norm_attn_g) * (1 + scale_a) + shift_a
    proj = h @ w_in
    q_a, k_a, v_a, c_q, c_kv, k_rope = jnp.split(
        proj, [D_SWA, 2 * D_SWA, 3 * D_SWA, 3 * D_SWA + Q_LORA_RANK,
               3 * D_SWA + Q_LORA_RANK + KV_LORA_RANK], axis=-1)
    heads = lambda a: a.reshape(B, S, N_HEADS_SWA, HEAD_DIM_SWA)
    o_a = dilated_attention(heads(q_a), heads(k_a), heads(v_a), positions).reshape(B, S, D_SWA)
    o_b = latent_attention(c_q, c_kv, k_rope, positions, g_q, w_uq, g_kv, w_ukv).reshape(B, S, D_MLA)
    mix = jnp.concatenate([rms_norm(o_a, g_out_swa), rms_norm(o_b, g_out_mla)], axis=-1)
    x = x + gate_a * (mix @ w_o)

    h = rms_norm(x, norm_ffn_g) * (1 + scale_f) + shift_f
    x = x + gate_f * moe_ffn(h, w_router, router_bias, w_exp_gate, w_exp_up, w_exp_down,
                             w_sh_gate, w_sh_up, w_sh_down)
    return x


def setup_inputs(seed: int = 0) -> dict:
    key = jax.random.key(seed)
    ks = jax.random.split(key, 24)
    f32 = jnp.float32
    L, D, E = DEPTH, D_MODEL, N_EXPERTS

    def nrm(k, shape, scale):
        return jax.random.normal(k, shape, f32) * scale

    def gain(k, shape):
        return 1.0 + 0.02 * jax.random.normal(k, shape, f32)

    start = jax.random.randint(ks[2], (BATCH, 1), 0, 4096, dtype=jnp.int32)
    positions = start + jnp.arange(SEQ, dtype=jnp.int32)[None, :]
    return {
        'x': nrm(ks[0], (BATCH, SEQ, D), 1.0),
        'c': nrm(ks[1], (BATCH, D), 1.0),
        'positions': positions,
        'norm_attn_g': gain(ks[3], (L, D)),
        'w_ada': nrm(ks[4], (L, D, N_ADA * D), 0.5 * D ** -0.5),
        'b_ada': nrm(ks[5], (L, N_ADA * D), 0.02),
        'w_in': nrm(ks[6], (L, D, D_IN), D ** -0.5),
        'g_q': gain(ks[7], (L, Q_LORA_RANK)),
        'w_uq': nrm(ks[8], (L, Q_LORA_RANK, N_HEADS_MLA * (QK_NOPE_DIM + QK_ROPE_DIM)), Q_LORA_RANK ** -0.5),
        'g_kv': gain(ks[9], (L, KV_LORA_RANK)),
        'w_ukv': nrm(ks[10], (L, KV_LORA_RANK, N_HEADS_MLA * (QK_NOPE_DIM + V_HEAD_DIM)), KV_LORA_RANK ** -0.5),
        'g_out_swa': gain(ks[11], (L, D_SWA)),
        'g_out_mla': gain(ks[12], (L, D_MLA)),
        'w_o': nrm(ks[13], (L, D_MIX, D), D_MIX ** -0.5),
        'norm_ffn_g': gain(ks[14], (L, D)),
        'w_router': nrm(ks[15], (L, D, E), D ** -0.5),
        'router_bias': nrm(ks[16], (L, E), 0.01),
        'w_exp_gate': nrm(ks[17], (L, E, D, D_EXPERT), D ** -0.5),
        'w_exp_up': nrm(ks[18], (L, E, D, D_EXPERT), D ** -0.5),
        'w_exp_down': nrm(ks[19], (L, E, D_EXPERT, D), D_EXPERT ** -0.5),
        'w_sh_gate': nrm(ks[20], (L, D, D_EXPERT), D ** -0.5),
        'w_sh_up': nrm(ks[21], (L, D, D_EXPERT), D ** -0.5),
        'w_sh_down': nrm(ks[22], (L, D_EXPERT, D), D_EXPERT ** -0.5),
        'final_norm_g': gain(ks[23], (D,)),
    }


def reference(x, c, positions, norm_attn_g, w_ada, b_ada, w_in, g_q, w_uq, g_kv, w_ukv,
              g_out_swa, g_out_mla, w_o, norm_ffn_g, w_router, router_bias,
              w_exp_gate, w_exp_up, w_exp_down, w_sh_gate, w_sh_up, w_sh_down, final_norm_g):
    for l in range(DEPTH):
        x = hybrid_layer(x, c, positions, norm_attn_g[l], w_ada[l], b_ada[l], w_in[l], g_q[l], w_uq[l],
                         g_kv[l], w_ukv[l], g_out_swa[l], g_out_mla[l], w_o[l], norm_ffn_g[l],
                         w_router[l], router_bias[l], w_exp_gate[l], w_exp_up[l], w_exp_down[l],
                         w_sh_gate[l], w_sh_up[l], w_sh_down[l])
    return rms_norm(x, final_norm_g)
```

```python
import functools

import jax
import jax.numpy as jnp
from jax import lax
from jax.experimental import pallas as pl
from jax.experimental.pallas import tpu as pltpu

F32 = jnp.float32
BF16 = jnp.bfloat16
I32 = jnp.int32

D_MODEL = 2048
N_HEADS_SWA = 8
HEAD_DIM_SWA = 128
SWA_PATTERNS = ((128, 1), (512, 4), (2048, 16))
SWA_BLOCK = 128
N_HEADS_MLA = 8
Q_LORA_RANK = 512
KV_LORA_RANK = 256
QK_NOPE_DIM = 128
QK_ROPE_DIM = 64
V_HEAD_DIM = 128
ROPE_THETA = 10000.0
D_SWA = N_HEADS_SWA * HEAD_DIM_SWA
D_MLA = N_HEADS_MLA * V_HEAD_DIM
N_EXPERTS = 64
N_GROUPS = 8
TOPK_GROUPS = 4
TOP_K = 8
D_EXPERT = 512
ROUTED_SCALE = 2.5
N_ADA = 6
EPS = 1e-6
NEG_INF = -1e30

LANES = 128
MLA_QK_PAD = 256
PACK_ROWS = D_MODEL // (2 * LANES)
Y_ROWS = D_MODEL // LANES
VMEM_LIMIT = 56 * 1024 * 1024

TM_INPROJ = 512
TM_MLAPROJ = 512
T_MLA = 512
TM_OUTPROJ = 256
TN_ROUTE = 512
TB_DISPATCH = 256
TR_EXPERT = 256
TB_COMBINE = 128


def _params(*sem):
    return pltpu.CompilerParams(dimension_semantics=sem, vmem_limit_bytes=VMEM_LIMIT)


def _rms(x, g):
    return x * lax.rsqrt(jnp.mean(x * x, axis=-1, keepdims=True) + EPS) * g


def _resident(shape):
    nd = len(shape)
    return pl.BlockSpec(shape, lambda *_: (0,) * nd, pipeline_mode=pl.Buffered(1))


def _pack_bf16_pairs(a, b):
    ua = lax.bitcast_convert_type(a.astype(BF16).astype(F32), I32)
    ub = lax.bitcast_convert_type(b.astype(BF16).astype(F32), I32)
    return lax.shift_right_logical(ua, jnp.int32(16)) | (ub & jnp.int32(-65536))


def _unpack_bf16_pairs(w):
    lo = lax.bitcast_convert_type(lax.shift_left(w, jnp.int32(16)), F32).astype(BF16)
    hi = lax.bitcast_convert_type(w & jnp.int32(-65536), F32).astype(BF16)
    return jnp.concatenate([lo, hi], axis=1)


def _packed_chunk(ref, s, rows):
    return _unpack_bf16_pairs(ref[pl.ds(s, rows, stride=PACK_ROWS), :])


def _ada_body(c_ref, w_ref, b_ref, o_ref):
    c = c_ref[...]
    a = c * jax.nn.sigmoid(c)
    o_ref[...] = jnp.sum(w_ref[...] * a, axis=0, keepdims=True) + b_ref[...]


def _ada(c, w_ada, b_ada):
    d, n = w_ada.shape
    tn = 512
    return pl.pallas_call(
        _ada_body,
        grid=(n // tn,),
        in_specs=[pl.BlockSpec((d, 1), lambda j: (0, 0)),
                  pl.BlockSpec((d, tn), lambda j: (0, j)),
                  pl.BlockSpec((1, tn), lambda j: (0, j))],
        out_specs=pl.BlockSpec((1, tn), lambda j: (0, j)),
        out_shape=jax.ShapeDtypeStruct((1, n), F32),
        compiler_params=_params("parallel"),
        name="ada",
    )(c.reshape(d, 1), w_ada, b_ada.reshape(1, n))


def _inproj_body(x_ref, g_ref, sc_ref, sh_ref, wqkv_ref, wr_ref, qkv_ref, rest_ref):
    h = (_rms(x_ref[...], g_ref[...]) * (1.0 + sc_ref[...]) + sh_ref[...]).astype(BF16)
    qkv_ref[...] = jnp.dot(h, wqkv_ref[...], preferred_element_type=F32).astype(BF16)
    rest_ref[...] = jnp.dot(h, wr_ref[...], preferred_element_type=F32)


def _inproj(x, g, scale, shift, w_qkv, w_rest):
    s, d = x.shape
    tm = min(TM_INPROJ, s)
    n1, n2 = w_qkv.shape[1], w_rest.shape[1]
    row = lambda i: (i, 0)
    vec = pl.BlockSpec((1, d), lambda i: (0, 0))
    return pl.pallas_call(
        _inproj_body,
        grid=(s // tm,),
        in_specs=[pl.BlockSpec((tm, d), row), vec, vec, vec, _resident((d, n1)), _resident((d, n2))],
        out_specs=[pl.BlockSpec((tm, n1), row), pl.BlockSpec((tm, n2), row)],
        out_shape=[jax.ShapeDtypeStruct((s, n1), BF16), jax.ShapeDtypeStruct((s, n2), F32)],
        compiler_params=_params("parallel"),
        name="inproj",
    )(x, g, scale, shift, w_qkv, w_rest)


def _rope_tail(t, c, s1, s2):
    return t * c + pltpu.roll(t, 32, 1) * s1 + pltpu.roll(t, 96, 1) * s2


def _mlaproj_body(rest_ref, gq_ref, gkv_ref, wuq_ref, wukv_ref, c_ref, s1_ref, s2_ref,
                  q_ref, k_ref, v_ref, *, scale):
    rest = rest_ref[...]
    c, s1, s2 = c_ref[...], s1_ref[...], s2_ref[...]
    cq = _rms(rest[:, :Q_LORA_RANK], gq_ref[...]).astype(BF16)
    ckv = _rms(rest[:, Q_LORA_RANK:Q_LORA_RANK + KV_LORA_RANK], gkv_ref[...]).astype(BF16)
    ktail = _rope_tail(rest[:, Q_LORA_RANK + KV_LORA_RANK:], c, s1, s2).astype(BF16)
    q = jnp.dot(cq, wuq_ref[...], preferred_element_type=F32)
    kv = jnp.dot(ckv, wukv_ref[...], preferred_element_type=F32)
    for h in range(N_HEADS_MLA):
        b = h * MLA_QK_PAD
        q_ref[:, b:b + LANES] = (q[:, b:b + LANES] * scale).astype(BF16)
        q_ref[:, b + LANES:b + 2 * LANES] = (_rope_tail(q[:, b + LANES:b + 2 * LANES], c, s1, s2) * scale).astype(BF16)
        k_ref[:, b:b + LANES] = kv[:, b:b + LANES].astype(BF16)
        k_ref[:, b + LANES:b + 2 * LANES] = ktail
        v_ref[:, h * LANES:(h + 1) * LANES] = kv[:, b + LANES:b + 2 * LANES].astype(BF16)


def _mlaproj(rest, g_q, g_kv, w_uq, w_ukv, rc, rs1, rs2):
    s, nr = rest.shape
    tm = min(TM_MLAPROJ, s)
    nq = N_HEADS_MLA * MLA_QK_PAD
    row = lambda i: (i, 0)
    tab = pl.BlockSpec((tm, LANES), row)
    scale = float(QK_NOPE_DIM + QK_ROPE_DIM) ** -0.5
    return pl.pallas_call(
        functools.partial(_mlaproj_body, scale=scale),
        grid=(s // tm,),
        in_specs=[pl.BlockSpec((tm, nr), row),
                  pl.BlockSpec((1, Q_LORA_RANK), lambda i: (0, 0)),
                  pl.BlockSpec((1, KV_LORA_RANK), lambda i: (0, 0)),
                  _resident(w_uq.shape), _resident(w_ukv.shape), tab, tab, tab],
        out_specs=[pl.BlockSpec((tm, nq), row), pl.BlockSpec((tm, nq), row), pl.BlockSpec((tm, D_MLA), row)],
        out_shape=[jax.ShapeDtypeStruct((s, nq), BF16), jax.ShapeDtypeStruct((s, nq), BF16),
                   jax.ShapeDtypeStruct((s, D_MLA), BF16)],
        compiler_params=_params("parallel"),
        name="mlaproj",
    )(rest, g_q, g_kv, w_uq, w_ukv, rc, rs1, rs2)


def _mla_body(q_ref, k_ref, v_ref, o_ref, m_ref, l_ref, acc_ref, *, t):
    qi = pl.program_id(1)
    q = q_ref[...]
    m_ref[...] = jnp.full(m_ref.shape, NEG_INF, F32)
    l_ref[...] = jnp.zeros(l_ref.shape, F32)
    acc_ref[...] = jnp.zeros(acc_ref.shape, F32)

    def update(j, masked):
        start = pl.multiple_of(j * t, t)
        k = k_ref[pl.ds(start, t), :]
        v = v_ref[pl.ds(start, t), :]
        s = lax.dot_general(q, k, (((1,), (1,)), ((), ())), preferred_element_type=F32)
        if masked:
            r = lax.broadcasted_iota(I32, (t, t), 0)
            cidx = lax.broadcasted_iota(I32, (t, t), 1)
            s = jnp.where(cidx <= r, s, NEG_INF)
        m_old = m_ref[...]
        m_new = jnp.maximum(m_old, jnp.max(s, axis=-1, keepdims=True))
        p = jnp.exp(s - m_new)
        alpha = jnp.exp(m_old - m_new)
        l_ref[...] = alpha * l_ref[...] + jnp.sum(p, axis=-1, keepdims=True)
        acc_ref[...] = alpha * acc_ref[...] + jnp.dot(p.astype(BF16), v, preferred_element_type=F32)
        m_ref[...] = m_new

    def body(j, carry):
        update(j, False)
        return carry

    lax.fori_loop(0, qi, body, 0)
    update(qi, True)
    o_ref[...] = acc_ref[...] / l_ref[...]


def _mla(q, k, v):
    s = q.shape[0]
    t = min(T_MLA, s)
    return pl.pallas_call(
        functools.partial(_mla_body, t=t),
        grid=(N_HEADS_MLA, s // t),
        in_specs=[pl.BlockSpec((t, MLA_QK_PAD), lambda h, i: (i, h)),
                  pl.BlockSpec((s, MLA_QK_PAD), lambda h, i: (0, h)),
                  pl.BlockSpec((s, V_HEAD_DIM), lambda h, i: (0, h))],
        out_specs=pl.BlockSpec((t, V_HEAD_DIM), lambda h, i: (i, h)),
        out_shape=jax.ShapeDtypeStruct((s, D_MLA), F32),
        scratch_shapes=[pltpu.VMEM((t, 1), F32), pltpu.VMEM((t, 1), F32), pltpu.VMEM((t, V_HEAD_DIM), F32)],
        compiler_params=_params("parallel", "arbitrary"),
        name="mla",
    )(q, k, v)


def _dilated_body(q_ref, kc_ref, kp_ref, vc_ref, vp_ref, pq_ref, pkc_ref, pkp_ref, o_ref, lse_ref):
    n = pl.program_id(1)
    blk = SWA_BLOCK
    i = lax.broadcasted_iota(I32, (blk, blk), 0)
    j = lax.broadcasted_iota(I32, (blk, blk), 1)
    ok_cur = j <= i
    ok_prev = (j >= i) & (n > 0)
    pq = pq_ref[...]
    dist_cur = jnp.abs(pq - pkc_ref[0])
    dist_prev = jnp.abs(pq - pkp_ref[0])
    scale = float(HEAD_DIM_SWA) ** -0.5
    nt = (((1,), (1,)), ((), ()))
    for h in range(N_HEADS_SWA):
        hs = slice(h * HEAD_DIM_SWA, (h + 1) * HEAD_DIM_SWA)
        slope = 2.0 ** (-8.0 * (h + 1) / N_HEADS_SWA)
        q = q_ref[:, hs]
        sc = lax.dot_general(q, kc_ref[:, hs], nt, preferred_element_type=F32) * scale - slope * dist_cur
        sp = lax.dot_general(q, kp_ref[:, hs], nt, preferred_element_type=F32) * scale - slope * dist_prev
        sc = jnp.where(ok_cur, sc, NEG_INF)
        sp = jnp.where(ok_prev, sp, NEG_INF)
        m = jnp.maximum(jnp.max(sc, axis=-1, keepdims=True), jnp.max(sp, axis=-1, keepdims=True))
        pc = jnp.exp(sc - m)
        pp = jnp.exp(sp - m)
        den = jnp.sum(pc, axis=-1, keepdims=True) + jnp.sum(pp, axis=-1, keepdims=True)
        o = (jnp.dot(pc.astype(BF16), vc_ref[:, hs], preferred_element_type=F32)
             + jnp.dot(pp.astype(BF16), vp_ref[:, hs], preferred_element_type=F32))
        o_ref[:, hs] = o / den
        lse_ref[:, hs] = jnp.broadcast_to(m + jnp.log(den), (blk, HEAD_DIM_SWA))


def _dilated(qkv, pos_lanes, pos_rows, dil):
    s = qkv.shape[0]
    sd = s // dil
    nb = sd // SWA_BLOCK
    blk = SWA_BLOCK
    qkv_v = qkv.reshape(sd, dil * 3 * D_SWA)
    pq_v = pos_lanes.reshape(sd, dil * LANES)
    prev = lambda n: jnp.maximum(n - 1, 0)
    wide = (blk, D_SWA)
    o, lse = pl.pallas_call(
        _dilated_body,
        grid=(dil, nb),
        in_specs=[pl.BlockSpec(wide, lambda r, n: (n, 3 * r)),
                  pl.BlockSpec(wide, lambda r, n: (n, 3 * r + 1)),
                  pl.BlockSpec(wide, lambda r, n: (prev(n), 3 * r + 1)),
                  pl.BlockSpec(wide, lambda r, n: (n, 3 * r + 2)),
                  pl.BlockSpec(wide, lambda r, n: (prev(n), 3 * r + 2)),
                  pl.BlockSpec((blk, LANES), lambda r, n: (n, r)),
                  pl.BlockSpec((1, 1, blk), lambda r, n: (r, 0, n)),
                  pl.BlockSpec((1, 1, blk), lambda r, n: (r, 0, prev(n)))],
        out_specs=[pl.BlockSpec(wide, lambda r, n: (n, r)), pl.BlockSpec(wide, lambda r, n: (n, r))],
        out_shape=[jax.ShapeDtypeStruct((sd, dil * D_SWA), F32), jax.ShapeDtypeStruct((sd, dil * D_SWA), F32)],
        compiler_params=_params("parallel", "parallel"),
        name=f"dil{dil}",
    )(qkv_v, qkv_v, qkv_v, qkv_v, qkv_v, pq_v, pos_rows, pos_rows)
    return o.reshape(s, D_SWA), lse.reshape(s, D_SWA)


def _outproj_body(o1_ref, o2_ref, o3_ref, l1_ref, l2_ref, l3_ref, ob_ref, x_ref, gsw_ref, gml_ref, wo_ref,
                  ga_ref, nfg_ref, scf_ref, shf_ref, wrt_ref, x1_ref, h2p_ref, lgt_ref, *, tm):
    l1, l2, l3 = l1_ref[...], l2_ref[...], l3_ref[...]
    m = jnp.maximum(jnp.maximum(l1, l2), l3)
    e1, e2, e3 = jnp.exp(l1 - m), jnp.exp(l2 - m), jnp.exp(l3 - m)
    oa = (e1 * o1_ref[...] + e2 * o2_ref[...] + e3 * o3_ref[...]) / (e1 + e2 + e3)
    mix = jnp.concatenate([_rms(oa, gsw_ref[...]), _rms(ob_ref[...], gml_ref[...])], axis=-1).astype(BF16)
    x1 = x_ref[...] + ga_ref[...] * jnp.dot(mix, wo_ref[...], preferred_element_type=F32)
    x1_ref[...] = x1
    h2 = _rms(x1, nfg_ref[...]) * (1.0 + scf_ref[...]) + shf_ref[...]
    lgt_ref[...] = lax.dot_general(wrt_ref[...], h2, (((1,), (1,)), ((), ())),
                                   precision=lax.Precision.HIGHEST, preferred_element_type=F32)
    for s in range(PACK_ROWS):
        b = 2 * LANES * s
        h2p_ref[pl.ds(s, tm, stride=PACK_ROWS), :] = _pack_bf16_pairs(h2[:, b:b + LANES], h2[:, b + LANES:b + 2 * LANES])


def _outproj(o_pats, lse_pats, o_b, x, g_sw, g_ml, w_o, gate_a, nfg, scale_f, shift_f, w_router_t):
    s, d = x.shape
    tm = min(TM_OUTPROJ, s)
    row = lambda i: (i, 0)
    half = pl.BlockSpec((tm, D_SWA), row)
    vec = lambda n: pl.BlockSpec((1, n), lambda i: (0, 0))
    return pl.pallas_call(
        functools.partial(_outproj_body, tm=tm),
        grid=(s // tm,),
        in_specs=[half] * 7 + [pl.BlockSpec((tm, d), row), vec(D_SWA), vec(D_MLA), _resident(w_o.shape),
                                vec(d), vec(d), vec(d), vec(d), _resident(w_router_t.shape)],
        out_specs=[pl.BlockSpec((tm, d), row), pl.BlockSpec((tm * PACK_ROWS, LANES), row),
                   pl.BlockSpec((N_EXPERTS, tm), lambda i: (0, i))],
        out_shape=[jax.ShapeDtypeStruct((s, d), F32), jax.ShapeDtypeStruct((s * PACK_ROWS, LANES), I32),
                   jax.ShapeDtypeStruct((N_EXPERTS, s), F32)],
        compiler_params=_params("parallel"),
        name="outproj",
    )(*o_pats, *lse_pats, o_b, x, g_sw, g_ml, w_o, gate_a, nfg, scale_f, shift_f, w_router_t)


def _first_index(hit_value, x, iota, size, axis):
    return jnp.min(jnp.where(x == hit_value, iota, size), axis=axis, keepdims=True)


def _route_body(lgt_ref, bias_ref, tri_ref, eidx_ref, wts_ref, rank_ref, cnt_ref, carry_ref, *, tn):
    @pl.when(pl.program_id(0) == 0)
    def _():
        carry_ref[...] = jnp.zeros(carry_ref.shape, F32)

    gsz = N_EXPERTS // N_GROUPS
    scores = jax.nn.sigmoid(lgt_ref[...])
    choice = scores + bias_ref[...]
    neg = jnp.float32(-jnp.inf)

    g3 = choice.reshape(N_GROUPS, gsz, tn)
    i3 = lax.broadcasted_iota(I32, g3.shape, 1)
    m1 = jnp.max(g3, axis=1, keepdims=True)
    f1 = _first_index(m1, g3, i3, gsz, 1)
    m2 = jnp.max(jnp.where(i3 == f1, neg, g3), axis=1, keepdims=True)
    gs = (m1 + m2).reshape(N_GROUPS, tn)

    ig = lax.broadcasted_iota(I32, gs.shape, 0)
    gsel = jnp.zeros(gs.shape, F32)
    for _ in range(TOPK_GROUPS):
        hit = ig == _first_index(jnp.max(gs, axis=0, keepdims=True), gs, ig, N_GROUPS, 0)
        gsel = jnp.where(hit, 1.0, gsel)
        gs = jnp.where(hit, neg, gs)
    emask = jnp.broadcast_to(gsel.reshape(N_GROUPS, 1, tn), (N_GROUPS, gsz, tn)).reshape(N_EXPERTS, tn)
    cand = jnp.where(emask > 0.0, choice, NEG_INF)

    ie = lax.broadcasted_iota(I32, cand.shape, 0)
    picks, wsel = [], []
    onehot = jnp.zeros(cand.shape, F32)
    for _ in range(TOP_K):
        f = _first_index(jnp.max(cand, axis=0, keepdims=True), cand, ie, N_EXPERTS, 0)
        hit = ie == f
        picks.append(f)
        wsel.append(jnp.sum(jnp.where(hit, scores, 0.0), axis=0, keepdims=True))
        onehot = jnp.where(hit, 1.0, onehot)
        cand = jnp.where(hit, neg, cand)

    rank = carry_ref[...] + jnp.dot(onehot.astype(BF16), tri_ref[...], preferred_element_type=F32)
    carry_ref[...] = carry_ref[...] + jnp.sum(onehot, axis=1, keepdims=True)
    cnt_ref[...] = carry_ref[...]

    w = jnp.concatenate(wsel, axis=0)
    wts_ref[...] = w / jnp.sum(w, axis=0, keepdims=True) * ROUTED_SCALE
    eidx_ref[...] = jnp.concatenate(picks, axis=0)
    rank_ref[...] = jnp.concatenate(
        [jnp.sum(jnp.where(ie == f, rank, 0.0), axis=0, keepdims=True) for f in picks], axis=0).astype(I32)


def _route(logits_t, router_bias):
    e, n = logits_t.shape
    tn = min(TN_ROUTE, n)
    tri = (lax.broadcasted_iota(I32, (tn, tn), 0) < lax.broadcasted_iota(I32, (tn, tn), 1)).astype(BF16)
    col = lambda i: (0, i)
    return pl.pallas_call(
        functools.partial(_route_body, tn=tn),
        grid=(n // tn,),
        in_specs=[pl.BlockSpec((e, tn), col), pl.BlockSpec((e, 1), lambda i: (0, 0)),
                  pl.BlockSpec((tn, tn), lambda i: (0, 0))],
        out_specs=[pl.BlockSpec((TOP_K, tn), col), pl.BlockSpec((TOP_K, tn), col), pl.BlockSpec((TOP_K, tn), col),
                   pl.BlockSpec((e, 1), lambda i: (0, 0))],
        out_shape=[jax.ShapeDtypeStruct((TOP_K, n), I32), jax.ShapeDtypeStruct((TOP_K, n), F32),
                   jax.ShapeDtypeStruct((TOP_K, n), I32), jax.ShapeDtypeStruct((e, 1), F32)],
        scratch_shapes=[pltpu.VMEM((e, 1), F32)],
        compiler_params=_params("arbitrary"),
        name="route",
    )(logits_t, router_bias.reshape(e, 1), tri)


def _dispatch_body(dest_ref, h_ref, init_ref, xs_ref, sem, *, tb):
    del init_ref

    def row_copy(t, k):
        src = h_ref.at[pl.ds(pl.multiple_of(t * PACK_ROWS, PACK_ROWS), PACK_ROWS), :]
        dst = xs_ref.at[pl.ds(pl.multiple_of(dest_ref[k, t] * PACK_ROWS, PACK_ROWS), PACK_ROWS), :]
        return pltpu.make_async_copy(src, dst, sem)

    def issue(t, carry):
        for k in range(TOP_K):
            row_copy(t, k).start()
        return carry

    def drain(t, carry):
        for k in range(TOP_K):
            row_copy(t, k).wait()
        return carry

    lax.fori_loop(0, tb, issue, 0)
    lax.fori_loop(0, tb, drain, 0)


def _dispatch(dest, h2p, n_slots):
    n = dest.shape[1]
    tb = min(TB_DISPATCH, n)
    init = jnp.zeros((n_slots * PACK_ROWS, LANES), I32)
    return pl.pallas_call(
        functools.partial(_dispatch_body, tb=tb),
        grid=(n // tb,),
        in_specs=[pl.BlockSpec((TOP_K, tb), lambda i: (0, i), memory_space=pltpu.SMEM),
                  pl.BlockSpec((tb * PACK_ROWS, LANES), lambda i: (i, 0)),
                  pl.BlockSpec(memory_space=pl.ANY)],
        out_specs=pl.BlockSpec(memory_space=pl.ANY),
        out_shape=jax.ShapeDtypeStruct(init.shape, I32),
        scratch_shapes=[pltpu.SemaphoreType.DMA(())],
        input_output_aliases={2: 0},
        compiler_params=_params("arbitrary"),
        name="dispatch",
    )(dest, h2p, init)


def _swiglu_packed(xp_ref, wg, wu, wd, rows):
    g = jnp.zeros((rows, D_EXPERT), F32)
    u = jnp.zeros((rows, D_EXPERT), F32)
    for s in range(PACK_ROWS):
        xk = _packed_chunk(xp_ref, s, rows)
        ks = slice(2 * LANES * s, 2 * LANES * (s + 1))
        g = g + jnp.dot(xk, wg[ks, :], preferred_element_type=F32)
        u = u + jnp.dot(xk, wu[ks, :], preferred_element_type=F32)
    a = (g * jax.nn.sigmoid(g) * u).astype(BF16)
    return jnp.dot(a, wd[...], preferred_element_type=F32)


def _experts_body(be_ref, nv_ref, xs_ref, wg_ref, wu_ref, wd_ref, y_ref, *, tr):
    del be_ref
    b = pl.program_id(0)

    @pl.when(nv_ref[b] > 0)
    def _():
        y = _swiglu_packed(xs_ref, wg_ref.at[0], wu_ref.at[0], wd_ref.at[0], tr)
        for s in range(Y_ROWS):
            y_ref[pl.ds(s, tr, stride=Y_ROWS), :] = y[:, s * LANES:(s + 1) * LANES]

    @pl.when(nv_ref[b] == 0)
    def _():
        y_ref[...] = jnp.zeros(y_ref.shape, F32)


def _experts(blk_e, blk_nv, xs, w_gate, w_up, w_down):
    tr = TR_EXPERT
    nb = blk_e.shape[0]
    d, f = w_gate.shape[1], w_gate.shape[2]
    grid_spec = pltpu.PrefetchScalarGridSpec(
        num_scalar_prefetch=2,
        grid=(nb,),
        in_specs=[pl.BlockSpec((tr * PACK_ROWS, LANES), lambda b, be, nv: (b, 0)),
                  pl.BlockSpec((1, d, f), lambda b, be, nv: (be[b], 0, 0)),
                  pl.BlockSpec((1, d, f), lambda b, be, nv: (be[b], 0, 0)),
                  pl.BlockSpec((1, f, d), lambda b, be, nv: (be[b], 0, 0))],
        out_specs=pl.BlockSpec((tr * Y_ROWS, LANES), lambda b, be, nv: (b, 0)),
    )
    return pl.pallas_call(
        functools.partial(_experts_body, tr=tr),
        grid_spec=grid_spec,
        out_shape=jax.ShapeDtypeStruct((nb * tr * Y_ROWS, LANES), F32),
        compiler_params=_params("arbitrary"),
        name="experts",
    )(blk_e, blk_nv, xs, w_gate, w_up, w_down)


def _combine_body(dest_ref, wts_ref, h2p_ref, x1_ref, gf_ref, fg_ref, wsg_ref, wsu_ref, wsd_ref, y_ref,
                  out_ref, ybuf, sem, *, tb):
    def row_copy(t, k):
        src = y_ref.at[pl.ds(pl.multiple_of(dest_ref[k, t] * Y_ROWS, Y_ROWS), Y_ROWS), :]
        dst = ybuf.at[pl.ds(pl.multiple_of((k * tb + t) * Y_ROWS, Y_ROWS), Y_ROWS), :]
        return pltpu.make_async_copy(src, dst, sem)

    def issue(t, carry):
        for k in range(TOP_K):
            row_copy(t, k).start()
        return carry

    def drain(t, carry):
        for k in range(TOP_K):
            row_copy(t, k).wait()
        return carry

    lax.fori_loop(0, tb, issue, 0)
    shared = _swiglu_packed(h2p_ref, wsg_ref, wsu_ref, wsd_ref, tb)
    lax.fori_loop(0, tb, drain, 0)

    wts = wts_ref[...]
    cols = []
    for s in range(Y_ROWS):
        acc = shared[:, s * LANES:(s + 1) * LANES]
        for k in range(TOP_K):
            acc = acc + wts[:, k:k + 1] * ybuf[pl.ds(k * tb * Y_ROWS + s, tb, stride=Y_ROWS), :]
        cols.append(acc)
    moe = jnp.concatenate(cols, axis=1)
    out_ref[...] = _rms(x1_ref[...] + gf_ref[...] * moe, fg_ref[...])


def _combine(dest, wts_t, h2p, x1, gate_f, final_g, w_sg, w_su, w_sd, y):
    n, d = x1.shape
    tb = min(TB_COMBINE, n)
    row = lambda i: (i, 0)
    vec = pl.BlockSpec((1, d), lambda i: (0, 0))
    return pl.pallas_call(
        functools.partial(_combine_body, tb=tb),
        grid=(n // tb,),
        in_specs=[pl.BlockSpec((TOP_K, tb), lambda i: (0, i), memory_space=pltpu.SMEM),
                  pl.BlockSpec((tb, TOP_K), row),
                  pl.BlockSpec((tb * PACK_ROWS, LANES), row),
                  pl.BlockSpec((tb, d), row), vec, vec,
                  _resident(w_sg.shape), _resident(w_su.shape), _resident(w_sd.shape),
                  pl.BlockSpec(memory_space=pl.ANY)],
        out_specs=pl.BlockSpec((tb, d), row),
        out_shape=jax.ShapeDtypeStruct((n, d), F32),
        scratch_shapes=[pltpu.VMEM((TOP_K * tb * Y_ROWS, LANES), F32), pltpu.SemaphoreType.DMA(())],
        compiler_params=_params("arbitrary"),
        name="combine",
    )(dest, wts_t, h2p, x1, gate_f, final_g, w_sg, w_su, w_sd, y)


def _rope_tables(pos):
    half = QK_ROPE_DIM // 2
    inv_freq = ROPE_THETA ** (-jnp.arange(half, dtype=F32) / half)
    ang = pos.astype(F32)[:, None] * inv_freq
    cos, sin = jnp.cos(ang), jnp.sin(ang)
    z = jnp.zeros_like(cos)
    c = jnp.concatenate([cos, cos, z, z], axis=1)
    s1 = jnp.concatenate([z, sin, z, z], axis=1)
    s2 = jnp.concatenate([-sin, z, z, z], axis=1)
    return c, s1, s2


def _layer(x, c, pos, norm_attn_g, w_ada, b_ada, w_in, g_q, w_uq, g_kv, w_ukv, g_out_swa, g_out_mla, w_o,
           norm_ffn_g, w_router, router_bias, w_exp_gate, w_exp_up, w_exp_down, w_sh_gate, w_sh_up, w_sh_down,
           final_g):
    s, d = x.shape
    row = lambda a: a.reshape(1, -1)

    mod = _ada(c, w_ada, b_ada)
    shift_a, scale_a, gate_a, shift_f, scale_f, gate_f = [mod[:, i * d:(i + 1) * d] for i in range(N_ADA)]

    n_qkv = 3 * D_SWA
    w_qkv = w_in[:, :n_qkv].astype(BF16)
    w_rest = jnp.pad(w_in[:, n_qkv:], ((0, 0), (0, LANES - QK_ROPE_DIM))).astype(BF16)
    qkv, rest = _inproj(x, row(norm_attn_g), scale_a, shift_a, w_qkv, w_rest)

    dq = QK_NOPE_DIM + QK_ROPE_DIM
    w_uq_p = jnp.pad(w_uq.reshape(Q_LORA_RANK, N_HEADS_MLA, dq), ((0, 0), (0, 0), (0, MLA_QK_PAD - dq)))
    w_uq_p = w_uq_p.reshape(Q_LORA_RANK, N_HEADS_MLA * MLA_QK_PAD).astype(BF16)
    rc, rs1, rs2 = _rope_tables(pos)
    q_m, k_m, v_m = _mlaproj(rest, row(g_q), row(g_kv), w_uq_p, w_ukv.astype(BF16), rc, rs1, rs2)
    o_b = _mla(q_m, k_m, v_m)

    posf = pos.astype(F32)
    pos_lanes = jnp.broadcast_to(posf[:, None], (s, LANES))
    o_pats, lse_pats = [], []
    for window, dil in SWA_PATTERNS:
        assert window // dil == SWA_BLOCK and s % (dil * SWA_BLOCK) == 0
        pos_rows = posf.reshape(s // dil, dil).T.reshape(dil, 1, s // dil)
        o_p, lse_p = _dilated(qkv, pos_lanes, pos_rows, dil)
        o_pats.append(o_p)
        lse_pats.append(lse_p)

    x1, h2p, logits_t = _outproj(o_pats, lse_pats, o_b, x, row(g_out_swa), row(g_out_mla), w_o.astype(BF16),
                                 gate_a, row(norm_ffn_g), scale_f, shift_f, w_router.T)

    eidx, wts, rank, cnt = _route(logits_t, router_bias)
    tr = TR_EXPERT
    counts = cnt[:, 0].astype(I32)
    padded = (counts + tr - 1) // tr * tr
    pad_end = jnp.cumsum(padded)
    pad_start = pad_end - padded
    dest = pad_start[eidx] + rank
    n_slots = s * TOP_K + N_EXPERTS * tr
    blk_start = jnp.arange(n_slots // tr, dtype=I32) * tr
    blk_e = jnp.minimum(jnp.searchsorted(pad_end, blk_start, side='right'), N_EXPERTS - 1).astype(I32)
    blk_nv = jnp.clip(counts[blk_e] - (blk_start - pad_start[blk_e]), 0, tr).astype(I32)

    xs = _dispatch(dest, h2p, n_slots)
    y = _experts(blk_e, blk_nv, xs, w_exp_gate.astype(BF16), w_exp_up.astype(BF16), w_exp_down.astype(BF16))
    return _combine(dest, wts.T, h2p, x1, gate_f, row(final_g), w_sh_gate.astype(BF16), w_sh_up.astype(BF16),
                    w_sh_down.astype(BF16), y)


def kernel(x, c, positions, norm_attn_g, w_ada, b_ada, w_in, g_q, w_uq, g_kv, w_ukv, g_out_swa, g_out_mla, w_o,
           norm_ffn_g, w_router, router_bias, w_exp_gate, w_exp_up, w_exp_down, w_sh_gate, w_sh_up, w_sh_down,
           final_norm_g):
    assert x.shape[0] == 1 and w_ada.shape[0] == 1
    out = _layer(x[0], c[0], positions[0], norm_attn_g[0], w_ada[0], b_ada[0], w_in[0], g_q[0], w_uq[0], g_kv[0],
                 w_ukv[0], g_out_swa[0], g_out_mla[0], w_o[0], norm_ffn_g[0], w_router[0], router_bias[0],
                 w_exp_gate[0], w_exp_up[0], w_exp_down[0], w_sh_gate[0], w_sh_up[0], w_sh_down[0], final_norm_g)
    return out[None]
```

```python
import functools

import jax
import jax.numpy as jnp
from jax import lax
from jax.experimental import pallas as pl
from jax.experimental.pallas import tpu as pltpu

F32 = jnp.float32
BF16 = jnp.bfloat16
I32 = jnp.int32

D_MODEL = 2048
N_HEADS_SWA = 8
HEAD_DIM_SWA = 128
SWA_PATTERNS = ((128, 1), (512, 4), (2048, 16))
SWA_BLOCK = 128
N_HEADS_MLA = 8
Q_LORA_RANK = 512
KV_LORA_RANK = 256
QK_NOPE_DIM = 128
QK_ROPE_DIM = 64
V_HEAD_DIM = 128
ROPE_THETA = 10000.0
D_SWA = N_HEADS_SWA * HEAD_DIM_SWA
D_MLA = N_HEADS_MLA * V_HEAD_DIM
N_EXPERTS = 64
N_GROUPS = 8
TOPK_GROUPS = 4
TOP_K = 8
D_EXPERT = 512
ROUTED_SCALE = 2.5
N_ADA = 6
EPS = 1e-6
NEG_INF = -1e30
LOG2E = 1.4426950408889634

LANES = 128
MLA_QK_PAD = 256
PACK_ROWS = D_MODEL // (2 * LANES)
Y_ROWS = D_MODEL // LANES
VMEM_LIMIT = 56 * 1024 * 1024

TM_INPROJ = 512
TM_MLAPROJ = 512
T_MLA = 1024
TM_OUTPROJ = 256
TN_ROUTE = 512
TB_DISPATCH = 256
TR_EXPERT = 256
TB_COMBINE = 128


def _params(*sem):
    return pltpu.CompilerParams(dimension_semantics=sem, vmem_limit_bytes=VMEM_LIMIT)


def _rms(x, g):
    return x * lax.rsqrt(jnp.mean(x * x, axis=-1, keepdims=True) + EPS) * g


def _resident(shape):
    nd = len(shape)
    return pl.BlockSpec(shape, lambda *_: (0,) * nd, pipeline_mode=pl.Buffered(1))


def _pack_bf16_pairs(a, b):
    ua = lax.bitcast_convert_type(a.astype(BF16).astype(F32), I32)
    ub = lax.bitcast_convert_type(b.astype(BF16).astype(F32), I32)
    return lax.shift_right_logical(ua, jnp.int32(16)) | (ub & jnp.int32(-65536))


def _unpack_bf16_pairs(w):
    lo = lax.bitcast_convert_type(lax.shift_left(w, jnp.int32(16)), F32).astype(BF16)
    hi = lax.bitcast_convert_type(w & jnp.int32(-65536), F32).astype(BF16)
    return jnp.concatenate([lo, hi], axis=1)


def _packed_chunk(ref, s, rows):
    return _unpack_bf16_pairs(ref[pl.ds(s, rows, stride=PACK_ROWS), :])


def _ada_body(c_ref, w_ref, b_ref, o_ref):
    c = c_ref[...]
    a = c * jax.nn.sigmoid(c)
    o_ref[...] = jnp.sum(w_ref[...] * a, axis=0, keepdims=True) + b_ref[...]


def _ada(c, w_ada, b_ada):
    d, n = w_ada.shape
    tn = 512
    return pl.pallas_call(
        _ada_body,
        grid=(n // tn,),
        in_specs=[pl.BlockSpec((d, 1), lambda j: (0, 0)),
                  pl.BlockSpec((d, tn), lambda j: (0, j)),
                  pl.BlockSpec((1, tn), lambda j: (0, j))],
        out_specs=pl.BlockSpec((1, tn), lambda j: (0, j)),
        out_shape=jax.ShapeDtypeStruct((1, n), F32),
        compiler_params=_params("parallel"),
        name="ada",
    )(c.reshape(d, 1), w_ada, b_ada.reshape(1, n))


def _inproj_body(x_ref, g_ref, sc_ref, sh_ref, wqkv_ref, wr_ref, qkv_ref, rest_ref):
    h = (_rms(x_ref[...], g_ref[...]) * (1.0 + sc_ref[...]) + sh_ref[...]).astype(BF16)
    qkv_ref[...] = jnp.dot(h, wqkv_ref[...], preferred_element_type=F32).astype(BF16)
    rest_ref[...] = jnp.dot(h, wr_ref[...], preferred_element_type=F32)


def _inproj(x, g, scale, shift, w_qkv, w_rest):
    s, d = x.shape
    tm = min(TM_INPROJ, s)
    n1, n2 = w_qkv.shape[1], w_rest.shape[1]
    row = lambda i: (i, 0)
    vec = pl.BlockSpec((1, d), lambda i: (0, 0))
    return pl.pallas_call(
        _inproj_body,
        grid=(s // tm,),
        in_specs=[pl.BlockSpec((tm, d), row), vec, vec, vec, _resident((d, n1)), _resident((d, n2))],
        out_specs=[pl.BlockSpec((tm, n1), row), pl.BlockSpec((tm, n2), row)],
        out_shape=[jax.ShapeDtypeStruct((s, n1), BF16), jax.ShapeDtypeStruct((s, n2), F32)],
        compiler_params=_params("parallel"),
        name="inproj",
    )(x, g, scale, shift, w_qkv, w_rest)


def _rope_tail(t, c, s1, s2):
    return t * c + pltpu.roll(t, 32, 1) * s1 + pltpu.roll(t, 96, 1) * s2


def _mlaproj_body(rest_ref, gq_ref, gkv_ref, wuq_ref, wukv_ref, c_ref, s1_ref, s2_ref,
                  q_ref, k_ref, v_ref, *, scale):
    rest = rest_ref[...]
    c, s1, s2 = c_ref[...], s1_ref[...], s2_ref[...]
    cq = _rms(rest[:, :Q_LORA_RANK], gq_ref[...]).astype(BF16)
    ckv = _rms(rest[:, Q_LORA_RANK:Q_LORA_RANK + KV_LORA_RANK], gkv_ref[...]).astype(BF16)
    ktail = _rope_tail(rest[:, Q_LORA_RANK + KV_LORA_RANK:], c, s1, s2).astype(BF16)
    q = jnp.dot(cq, wuq_ref[...], preferred_element_type=F32)
    kv = jnp.dot(ckv, wukv_ref[...], preferred_element_type=F32)
    for h in range(N_HEADS_MLA):
        b = h * MLA_QK_PAD
        q_ref[:, b:b + LANES] = (q[:, b:b + LANES] * scale).astype(BF16)
        q_ref[:, b + LANES:b + 2 * LANES] = (_rope_tail(q[:, b + LANES:b + 2 * LANES], c, s1, s2) * scale).astype(BF16)
        k_ref[:, b:b + LANES] = kv[:, b:b + LANES].astype(BF16)
        k_ref[:, b + LANES:b + 2 * LANES] = ktail
        v_ref[:, b:b + LANES] = kv[:, b + LANES:b + 2 * LANES].astype(BF16)
        v_ref[:, b + LANES:b + 2 * LANES] = jnp.ones((q.shape[0], LANES), BF16)


def _mlaproj(rest, g_q, g_kv, w_uq, w_ukv, rc, rs1, rs2):
    s, nr = rest.shape
    tm = min(TM_MLAPROJ, s)
    nq = N_HEADS_MLA * MLA_QK_PAD
    row = lambda i: (i, 0)
    tab = pl.BlockSpec((tm, LANES), row)
    scale = float(QK_NOPE_DIM + QK_ROPE_DIM) ** -0.5 * LOG2E
    return pl.pallas_call(
        functools.partial(_mlaproj_body, scale=scale),
        grid=(s // tm,),
        in_specs=[pl.BlockSpec((tm, nr), row),
                  pl.BlockSpec((1, Q_LORA_RANK), lambda i: (0, 0)),
                  pl.BlockSpec((1, KV_LORA_RANK), lambda i: (0, 0)),
                  _resident(w_uq.shape), _resident(w_ukv.shape), tab, tab, tab],
        out_specs=[pl.BlockSpec((tm, nq), row)] * 3,
        out_shape=[jax.ShapeDtypeStruct((s, nq), BF16)] * 3,
        compiler_params=_params("parallel"),
        name="mlaproj",
    )(rest, g_q, g_kv, w_uq, w_ukv, rc, rs1, rs2)


def _mla_body(q_ref, k_ref, v_ref, o_ref, m_ref, acc_ref, *, t):
    qi = pl.program_id(1)
    q = q_ref[...]
    m_ref[...] = jnp.full(m_ref.shape, NEG_INF, F32)
    acc_ref[...] = jnp.zeros(acc_ref.shape, F32)

    def update(j, masked):
        start = pl.multiple_of(j * t, t)
        k = k_ref[pl.ds(start, t), :]
        v = v_ref[pl.ds(start, t), :]
        s = lax.dot_general(q, k, (((1,), (1,)), ((), ())), preferred_element_type=F32)
        if masked:
            r = lax.broadcasted_iota(I32, (t, t), 0)
            cidx = lax.broadcasted_iota(I32, (t, t), 1)
            s = jnp.where(cidx <= r, s, NEG_INF)
        m_old = m_ref[...]
        m_new = jnp.maximum(m_old, jnp.max(s, axis=-1, keepdims=True))
        p = jnp.exp2(s - m_new).astype(BF16)
        acc_ref[...] = jnp.exp2(m_old - m_new) * acc_ref[...] + jnp.dot(p, v, preferred_element_type=F32)
        m_ref[...] = m_new

    def body(j, carry):
        update(j, False)
        return carry

    lax.fori_loop(0, qi, body, 0)
    update(qi, True)
    acc = acc_ref[...]
    o_ref[...] = acc[:, :V_HEAD_DIM] / acc[:, V_HEAD_DIM:]


def _mla(q, k, v):
    s = q.shape[0]
    t = min(T_MLA, s)
    head_cols = lambda h, i: (0, h)
    return pl.pallas_call(
        functools.partial(_mla_body, t=t),
        grid=(N_HEADS_MLA, s // t),
        in_specs=[pl.BlockSpec((t, MLA_QK_PAD), lambda h, i: (i, h)),
                  pl.BlockSpec((s, MLA_QK_PAD), head_cols, pipeline_mode=pl.Buffered(1)),
                  pl.BlockSpec((s, MLA_QK_PAD), head_cols, pipeline_mode=pl.Buffered(1))],
        out_specs=pl.BlockSpec((t, V_HEAD_DIM), lambda h, i: (i, h)),
        out_shape=jax.ShapeDtypeStruct((s, D_MLA), F32),
        scratch_shapes=[pltpu.VMEM((t, 1), F32), pltpu.VMEM((t, MLA_QK_PAD), F32)],
        compiler_params=_params("parallel", "arbitrary"),
        name="mla",
    )(q, k, v)


def _dilated_body(q_ref, kc_ref, kp_ref, vc_ref, vp_ref, pq_ref, pkc_ref, pkp_ref, o_ref, lse_ref):
    n = pl.program_id(1)
    blk = SWA_BLOCK
    i = lax.broadcasted_iota(I32, (blk, blk), 0)
    j = lax.broadcasted_iota(I32, (blk, blk), 1)
    ok_cur = j <= i
    ok_prev = (j >= i) & (n > 0)
    pq = pq_ref[...]
    dist_cur = jnp.abs(pq - pkc_ref[0])
    dist_prev = jnp.abs(pq - pkp_ref[0])
    scale = float(HEAD_DIM_SWA) ** -0.5
    nt = (((1,), (1,)), ((), ()))
    for h in range(N_HEADS_SWA):
        hs = slice(h * HEAD_DIM_SWA, (h + 1) * HEAD_DIM_SWA)
        slope = 2.0 ** (-8.0 * (h + 1) / N_HEADS_SWA)
        q = q_ref[:, hs]
        sc = lax.dot_general(q, kc_ref[:, hs], nt, preferred_element_type=F32) * scale - slope * dist_cur
        sp = lax.dot_general(q, kp_ref[:, hs], nt, preferred_element_type=F32) * scale - slope * dist_prev
        sc = jnp.where(ok_cur, sc, NEG_INF)
        sp = jnp.where(ok_prev, sp, NEG_INF)
        m = jnp.maximum(jnp.max(sc, axis=-1, keepdims=True), jnp.max(sp, axis=-1, keepdims=True))
        pc = jnp.exp(sc - m)
        pp = jnp.exp(sp - m)
        den = jnp.sum(pc, axis=-1, keepdims=True) + jnp.sum(pp, axis=-1, keepdims=True)
        o = (jnp.dot(pc.astype(BF16), vc_ref[:, hs], preferred_element_type=F32)
             + jnp.dot(pp.astype(BF16), vp_ref[:, hs], preferred_element_type=F32))
        o_ref[:, hs] = o / den
        lse_ref[:, hs] = jnp.broadcast_to(m + jnp.log(den), (blk, HEAD_DIM_SWA))


def _dilated(qkv, pos_lanes, pos_rows, dil):
    s = qkv.shape[0]
    sd = s // dil
    nb = sd // SWA_BLOCK
    blk = SWA_BLOCK
    qkv_v = qkv.reshape(sd, dil * 3 * D_SWA)
    pq_v = pos_lanes.reshape(sd, dil * LANES)
    prev = lambda n: jnp.maximum(n - 1, 0)
    wide = (blk, D_SWA)
    o, lse = pl.pallas_call(
        _dilated_body,
        grid=(dil, nb),
        in_specs=[pl.BlockSpec(wide, lambda r, n: (n, 3 * r)),
                  pl.BlockSpec(wide, lambda r, n: (n, 3 * r + 1)),
                  pl.BlockSpec(wide, lambda r, n: (prev(n), 3 * r + 1)),
                  pl.BlockSpec(wide, lambda r, n: (n, 3 * r + 2)),
                  pl.BlockSpec(wide, lambda r, n: (prev(n), 3 * r + 2)),
                  pl.BlockSpec((blk, LANES), lambda r, n: (n, r)),
                  pl.BlockSpec((1, 1, blk), lambda r, n: (r, 0, n)),
                  pl.BlockSpec((1, 1, blk), lambda r, n: (r, 0, prev(n)))],
        out_specs=[pl.BlockSpec(wide, lambda r, n: (n, r)), pl.BlockSpec(wide, lambda r, n: (n, r))],
        out_shape=[jax.ShapeDtypeStruct((sd, dil * D_SWA), F32), jax.ShapeDtypeStruct((sd, dil * D_SWA), F32)],
        compiler_params=_params("parallel", "parallel"),
        name=f"dil{dil}",
    )(qkv_v, qkv_v, qkv_v, qkv_v, qkv_v, pq_v, pos_rows, pos_rows)
    return o.reshape(s, D_SWA), lse.reshape(s, D_SWA)


def _outproj_body(o1_ref, o2_ref, o3_ref, l1_ref, l2_ref, l3_ref, ob_ref, x_ref, gsw_ref, gml_ref, wo_ref,
                  ga_ref, nfg_ref, scf_ref, shf_ref, wrt_ref, x1_ref, h2p_ref, lgt_ref, *, tm):
    l1, l2, l3 = l1_ref[...], l2_ref[...], l3_ref[...]
    m = jnp.maximum(jnp.maximum(l1, l2), l3)
    e1, e2, e3 = jnp.exp(l1 - m), jnp.exp(l2 - m), jnp.exp(l3 - m)
    oa = (e1 * o1_ref[...] + e2 * o2_ref[...] + e3 * o3_ref[...]) / (e1 + e2 + e3)
    mix = jnp.concatenate([_rms(oa, gsw_ref[...]), _rms(ob_ref[...], gml_ref[...])], axis=-1).astype(BF16)
    x1 = x_ref[...] + ga_ref[...] * jnp.dot(mix, wo_ref[...], preferred_element_type=F32)
    x1_ref[...] = x1
    h2 = _rms(x1, nfg_ref[...]) * (1.0 + scf_ref[...]) + shf_ref[...]
    lgt_ref[...] = lax.dot_general(wrt_ref[...], h2, (((1,), (1,)), ((), ())),
                                   precision=lax.Precision.HIGHEST, preferred_element_type=F32)
    for s in range(PACK_ROWS):
        b = 2 * LANES * s
        h2p_ref[pl.ds(s, tm, stride=PACK_ROWS), :] = _pack_bf16_pairs(h2[:, b:b + LANES], h2[:, b + LANES:b + 2 * LANES])


def _outproj(o_pats, lse_pats, o_b, x, g_sw, g_ml, w_o, gate_a, nfg, scale_f, shift_f, w_router_t):
    s, d = x.shape
    tm = min(TM_OUTPROJ, s)
    row = lambda i: (i, 0)
    half = pl.BlockSpec((tm, D_SWA), row)
    vec = lambda n: pl.BlockSpec((1, n), lambda i: (0, 0))
    return pl.pallas_call(
        functools.partial(_outproj_body, tm=tm),
        grid=(s // tm,),
        in_specs=[half] * 7 + [pl.BlockSpec((tm, d), row), vec(D_SWA), vec(D_MLA), _resident(w_o.shape),
                                vec(d), vec(d), vec(d), vec(d), _resident(w_router_t.shape)],
        out_specs=[pl.BlockSpec((tm, d), row), pl.BlockSpec((tm * PACK_ROWS, LANES), row),
                   pl.BlockSpec((N_EXPERTS, tm), lambda i: (0, i))],
        out_shape=[jax.ShapeDtypeStruct((s, d), F32), jax.ShapeDtypeStruct((s * PACK_ROWS, LANES), I32),
                   jax.ShapeDtypeStruct((N_EXPERTS, s), F32)],
        compiler_params=_params("parallel"),
        name="outproj",
    )(*o_pats, *lse_pats, o_b, x, g_sw, g_ml, w_o, gate_a, nfg, scale_f, shift_f, w_router_t)


def _first_index(hit_value, x, iota, size, axis):
    return jnp.min(jnp.where(x == hit_value, iota, size), axis=axis, keepdims=True)


def _route_body(lgt_ref, bias_ref, tri_ref, eidx_ref, wts_ref, rank_ref, cnt_ref, carry_ref, *, tn):
    @pl.when(pl.program_id(0) == 0)
    def _():
        carry_ref[...] = jnp.zeros(carry_ref.shape, F32)

    gsz = N_EXPERTS // N_GROUPS
    scores = jax.nn.sigmoid(lgt_ref[...])
    choice = scores + bias_ref[...]
    neg = jnp.float32(-jnp.inf)

    g3 = choice.reshape(N_GROUPS, gsz, tn)
    i3 = lax.broadcasted_iota(I32, g3.shape, 1)
    m1 = jnp.max(g3, axis=1, keepdims=True)
    f1 = _first_index(m1, g3, i3, gsz, 1)
    m2 = jnp.max(jnp.where(i3 == f1, neg, g3), axis=1, keepdims=True)
    gs = (m1 + m2).reshape(N_GROUPS, tn)

    ig = lax.broadcasted_iota(I32, gs.shape, 0)
    gsel = jnp.zeros(gs.shape, F32)
    for _ in range(TOPK_GROUPS):
        hit = ig == _first_index(jnp.max(gs, axis=0, keepdims=True), gs, ig, N_GROUPS, 0)
        gsel = jnp.where(hit, 1.0, gsel)
        gs = jnp.where(hit, neg, gs)
    emask = jnp.broadcast_to(gsel.reshape(N_GROUPS, 1, tn), (N_GROUPS, gsz, tn)).reshape(N_EXPERTS, tn)
    cand = jnp.where(emask > 0.0, choice, NEG_INF)

    ie = lax.broadcasted_iota(I32, cand.shape, 0)
    picks, wsel = [], []
    onehot = jnp.zeros(cand.shape, F32)
    for _ in range(TOP_K):
        f = _first_index(jnp.max(cand, axis=0, keepdims=True), cand, ie, N_EXPERTS, 0)
        hit = ie == f
        picks.append(f)
        wsel.append(jnp.sum(jnp.where(hit, scores, 0.0), axis=0, keepdims=True))
        onehot = jnp.where(hit, 1.0, onehot)
        cand = jnp.where(hit, neg, cand)

    rank = carry_ref[...] + jnp.dot(onehot.astype(BF16), tri_ref[...], preferred_element_type=F32)
    carry_ref[...] = carry_ref[...] + jnp.sum(onehot, axis=1, keepdims=True)
    cnt_ref[...] = carry_ref[...]

    w = jnp.concatenate(wsel, axis=0)
    wts_ref[...] = w / jnp.sum(w, axis=0, keepdims=True) * ROUTED_SCALE
    eidx_ref[...] = jnp.concatenate(picks, axis=0)
    rank_ref[...] = jnp.concatenate(
        [jnp.sum(jnp.where(ie == f, rank, 0.0), axis=0, keepdims=True) for f in picks], axis=0).astype(I32)


def _route(logits_t, router_bias):
    e, n = logits_t.shape
    tn = min(TN_ROUTE, n)
    tri = (lax.broadcasted_iota(I32, (tn, tn), 0) < lax.broadcasted_iota(I32, (tn, tn), 1)).astype(BF16)
    col = lambda i: (0, i)
    return pl.pallas_call(
        functools.partial(_route_body, tn=tn),
        grid=(n // tn,),
        in_specs=[pl.BlockSpec((e, tn), col), pl.BlockSpec((e, 1), lambda i: (0, 0)),
                  pl.BlockSpec((tn, tn), lambda i: (0, 0))],
        out_specs=[pl.BlockSpec((TOP_K, tn), col), pl.BlockSpec((TOP_K, tn), col), pl.BlockSpec((TOP_K, tn), col),
                   pl.BlockSpec((e, 1), lambda i: (0, 0))],
        out_shape=[jax.ShapeDtypeStruct((TOP_K, n), I32), jax.ShapeDtypeStruct((TOP_K, n), F32),
                   jax.ShapeDtypeStruct((TOP_K, n), I32), jax.ShapeDtypeStruct((e, 1), F32)],
        scratch_shapes=[pltpu.VMEM((e, 1), F32)],
        compiler_params=_params("arbitrary"),
        name="route",
    )(logits_t, router_bias.reshape(e, 1), tri)


def _dispatch_body(dest_ref, h_ref, init_ref, xs_ref, sem, *, tb):
    del init_ref

    def row_copy(t, k):
        src = h_ref.at[pl.ds(pl.multiple_of(t * PACK_ROWS, PACK_ROWS), PACK_ROWS), :]
        dst = xs_ref.at[pl.ds(pl.multiple_of(dest_ref[k, t] * PACK_ROWS, PACK_ROWS), PACK_ROWS), :]
        return pltpu.make_async_copy(src, dst, sem)

    def issue(t, carry):
        for k in range(TOP_K):
            row_copy(t, k).start()
        return carry

    def drain(t, carry):
        for k in range(TOP_K):
            row_copy(t, k).wait()
        return carry

    lax.fori_loop(0, tb, issue, 0)
    lax.fori_loop(0, tb, drain, 0)


def _dispatch(dest, h2p, n_slots):
    n = dest.shape[1]
    tb = min(TB_DISPATCH, n)
    init = jnp.zeros((n_slots * PACK_ROWS, LANES), I32)
    return pl.pallas_call(
        functools.partial(_dispatch_body, tb=tb),
        grid=(n // tb,),
        in_specs=[pl.BlockSpec((TOP_K, tb), lambda i: (0, i), memory_space=pltpu.SMEM),
                  pl.BlockSpec((tb * PACK_ROWS, LANES), lambda i: (i, 0)),
                  pl.BlockSpec(memory_space=pl.ANY)],
        out_specs=pl.BlockSpec(memory_space=pl.ANY),
        out_shape=jax.ShapeDtypeStruct(init.shape, I32),
        scratch_shapes=[pltpu.SemaphoreType.DMA(())],
        input_output_aliases={2: 0},
        compiler_params=_params("arbitrary"),
        name="dispatch",
    )(dest, h2p, init)


def _swiglu_packed(xp_ref, wg, wu, wd, rows):
    g = jnp.zeros((rows, D_EXPERT), F32)
    u = jnp.zeros((rows, D_EXPERT), F32)
    for s in range(PACK_ROWS):
        xk = _packed_chunk(xp_ref, s, rows)
        ks = slice(2 * LANES * s, 2 * LANES * (s + 1))
        g = g + jnp.dot(xk, wg[ks, :], preferred_element_type=F32)
        u = u + jnp.dot(xk, wu[ks, :], preferred_element_type=F32)
    a = (g * jax.nn.sigmoid(g) * u).astype(BF16)
    return jnp.dot(a, wd[...], preferred_element_type=F32)


def _experts_body(be_ref, nv_ref, new_ref, xs_ref, wg_ref, wu_ref, wd_ref, y_ref, wg_s, wu_s, wd_s, *, tr):
    del be_ref
    b = pl.program_id(0)

    @pl.when(new_ref[b] > 0)
    def _():
        wg_s[...] = wg_ref[0].astype(BF16)
        wu_s[...] = wu_ref[0].astype(BF16)
        wd_s[...] = wd_ref[0].astype(BF16)

    @pl.when(nv_ref[b] > 0)
    def _():
        y = _swiglu_packed(xs_ref, wg_s, wu_s, wd_s, tr)
        for s in range(Y_ROWS):
            y_ref[pl.ds(s, tr, stride=Y_ROWS), :] = y[:, s * LANES:(s + 1) * LANES]

    @pl.when(nv_ref[b] == 0)
    def _():
        y_ref[...] = jnp.zeros(y_ref.shape, F32)


def _experts(blk_e, blk_nv, blk_new, xs, w_gate, w_up, w_down):
    tr = TR_EXPERT
    nb = blk_e.shape[0]
    d, f = w_gate.shape[1], w_gate.shape[2]
    grid_spec = pltpu.PrefetchScalarGridSpec(
        num_scalar_prefetch=3,
        grid=(nb,),
        in_specs=[pl.BlockSpec((tr * PACK_ROWS, LANES), lambda b, be, nv, nw: (b, 0)),
                  pl.BlockSpec((1, d, f), lambda b, be, nv, nw: (be[b], 0, 0)),
                  pl.BlockSpec((1, d, f), lambda b, be, nv, nw: (be[b], 0, 0)),
                  pl.BlockSpec((1, f, d), lambda b, be, nv, nw: (be[b], 0, 0))],
        out_specs=pl.BlockSpec((tr * Y_ROWS, LANES), lambda b, be, nv, nw: (b, 0)),
        scratch_shapes=[pltpu.VMEM((d, f), BF16), pltpu.VMEM((d, f), BF16), pltpu.VMEM((f, d), BF16)],
    )
    return pl.pallas_call(
        functools.partial(_experts_body, tr=tr),
        grid_spec=grid_spec,
        out_shape=jax.ShapeDtypeStruct((nb * tr * Y_ROWS, LANES), F32),
        compiler_params=_params("arbitrary"),
        name="experts",
    )(blk_e, blk_nv, blk_new, xs, w_gate, w_up, w_down)


def _combine_body(dest_ref, wts_ref, h2p_ref, x1_ref, gf_ref, fg_ref, wsg_ref, wsu_ref, wsd_ref, y_ref,
                  out_ref, ybuf, sem, *, tb):
    def row_copy(t, k):
        src = y_ref.at[pl.ds(pl.multiple_of(dest_ref[k, t] * Y_ROWS, Y_ROWS), Y_ROWS), :]
        dst = ybuf.at[pl.ds(pl.multiple_of((k * tb + t) * Y_ROWS, Y_ROWS), Y_ROWS), :]
        return pltpu.make_async_copy(src, dst, sem)

    def issue(t, carry):
        for k in range(TOP_K):
            row_copy(t, k).start()
        return carry

    def drain(t, carry):
        for k in range(TOP_K):
            row_copy(t, k).wait()
        return carry

    lax.fori_loop(0, tb, issue, 0)
    shared = _swiglu_packed(h2p_ref, wsg_ref, wsu_ref, wsd_ref, tb)
    lax.fori_loop(0, tb, drain, 0)

    wts = wts_ref[...]
    cols = []
    for s in range(Y_ROWS):
        acc = shared[:, s * LANES:(s + 1) * LANES]
        for k in range(TOP_K):
            acc = acc + wts[:, k:k + 1] * ybuf[pl.ds(k * tb * Y_ROWS + s, tb, stride=Y_ROWS), :]
        cols.append(acc)
    moe = jnp.concatenate(cols, axis=1)
    out_ref[...] = _rms(x1_ref[...] + gf_ref[...] * moe, fg_ref[...])


def _combine(dest, wts_t, h2p, x1, gate_f, final_g, w_sg, w_su, w_sd, y):
    n, d = x1.shape
    tb = min(TB_COMBINE, n)
    row = lambda i: (i, 0)
    vec = pl.BlockSpec((1, d), lambda i: (0, 0))
    return pl.pallas_call(
        functools.partial(_combine_body, tb=tb),
        grid=(n // tb,),
        in_specs=[pl.BlockSpec((TOP_K, tb), lambda i: (0, i), memory_space=pltpu.SMEM),
                  pl.BlockSpec((tb, TOP_K), row),
                  pl.BlockSpec((tb * PACK_ROWS, LANES), row),
                  pl.BlockSpec((tb, d), row), vec, vec,
                  _resident(w_sg.shape), _resident(w_su.shape), _resident(w_sd.shape),
                  pl.BlockSpec(memory_space=pl.ANY)],
        out_specs=pl.BlockSpec((tb, d), row),
        out_shape=jax.ShapeDtypeStruct((n, d), F32),
        scratch_shapes=[pltpu.VMEM((TOP_K * tb * Y_ROWS, LANES), F32), pltpu.SemaphoreType.DMA(())],
        compiler_params=_params("arbitrary"),
        name="combine",
    )(dest, wts_t, h2p, x1, gate_f, final_g, w_sg, w_su, w_sd, y)


def _rope_tables(pos):
    half = QK_ROPE_DIM // 2
    inv_freq = ROPE_THETA ** (-jnp.arange(half, dtype=F32) / half)
    ang = pos.astype(F32)[:, None] * inv_freq
    cos, sin = jnp.cos(ang), jnp.sin(ang)
    z = jnp.zeros_like(cos)
    c = jnp.concatenate([cos, cos, z, z], axis=1)
    s1 = jnp.concatenate([z, sin, z, z], axis=1)
    s2 = jnp.concatenate([-sin, z, z, z], axis=1)
    return c, s1, s2


def _layer(x, c, pos, norm_attn_g, w_ada, b_ada, w_in, g_q, w_uq, g_kv, w_ukv, g_out_swa, g_out_mla, w_o,
           norm_ffn_g, w_router, router_bias, w_exp_gate, w_exp_up, w_exp_down, w_sh_gate, w_sh_up, w_sh_down,
           final_g):
    s, d = x.shape
    row = lambda a: a.reshape(1, -1)

    mod = _ada(c, w_ada, b_ada)
    shift_a, scale_a, gate_a, shift_f, scale_f, gate_f = [mod[:, i * d:(i + 1) * d] for i in range(N_ADA)]

    n_qkv = 3 * D_SWA
    w_qkv = w_in[:, :n_qkv].astype(BF16)
    w_rest = jnp.pad(w_in[:, n_qkv:], ((0, 0), (0, LANES - QK_ROPE_DIM))).astype(BF16)
    qkv, rest = _inproj(x, row(norm_attn_g), scale_a, shift_a, w_qkv, w_rest)

    dq = QK_NOPE_DIM + QK_ROPE_DIM
    w_uq_p = jnp.pad(w_uq.reshape(Q_LORA_RANK, N_HEADS_MLA, dq), ((0, 0), (0, 0), (0, MLA_QK_PAD - dq)))
    w_uq_p = w_uq_p.reshape(Q_LORA_RANK, N_HEADS_MLA * MLA_QK_PAD).astype(BF16)
    rc, rs1, rs2 = _rope_tables(pos)
    q_m, k_m, v_m = _mlaproj(rest, row(g_q), row(g_kv), w_uq_p, w_ukv.astype(BF16), rc, rs1, rs2)
    o_b = _mla(q_m, k_m, v_m)

    posf = pos.astype(F32)
    pos_lanes = jnp.broadcast_to(posf[:, None], (s, LANES))
    o_pats, lse_pats = [], []
    for window, dil in SWA_PATTERNS:
        assert window // dil == SWA_BLOCK and s % (dil * SWA_BLOCK) == 0
        pos_rows = posf.reshape(s // dil, dil).T.reshape(dil, 1, s // dil)
        o_p, lse_p = _dilated(qkv, pos_lanes, pos_rows, dil)
        o_pats.append(o_p)
        lse_pats.append(lse_p)

    x1, h2p, logits_t = _outproj(o_pats, lse_pats, o_b, x, row(g_out_swa), row(g_out_mla), w_o.astype(BF16),
                                 gate_a, row(norm_ffn_g), scale_f, shift_f, w_router.T)

    eidx, wts, rank, cnt = _route(logits_t, router_bias)
    tr = TR_EXPERT
    counts = cnt[:, 0].astype(I32)
    padded = (counts + tr - 1) // tr * tr
    e_ids = jnp.arange(N_EXPERTS, dtype=I32)
    pad_end = jnp.sum(jnp.where(e_ids[None, :] <= e_ids[:, None], padded[None, :], 0), axis=1)
    pad_start = pad_end - padded
    lookup = lambda table, idx: jnp.sum(jnp.where(idx[..., None] == e_ids, table, 0), axis=-1)
    dest = lookup(pad_start, eidx) + rank
    n_slots = s * TOP_K + N_EXPERTS * tr
    blk_start = jnp.arange(n_slots // tr, dtype=I32) * tr
    blk_e = jnp.minimum(jnp.sum((pad_end[None, :] <= blk_start[:, None]).astype(I32), axis=1), N_EXPERTS - 1)
    blk_nv = jnp.clip(lookup(counts, blk_e) - (blk_start - lookup(pad_start, blk_e)), 0, tr)
    blk_new = ((blk_nv > 0) & (blk_start == lookup(pad_start, blk_e))).astype(I32)

    xs = _dispatch(dest, h2p, n_slots)
    y = _experts(blk_e, blk_nv, blk_new, xs, w_exp_gate, w_exp_up, w_exp_down)
    return _combine(dest, wts.T, h2p, x1, gate_f, row(final_g), w_sh_gate.astype(BF16), w_sh_up.astype(BF16),
                    w_sh_down.astype(BF16), y)


def kernel(x, c, positions, norm_attn_g, w_ada, b_ada, w_in, g_q, w_uq, g_kv, w_ukv, g_out_swa, g_out_mla, w_o,
           norm_ffn_g, w_router, router_bias, w_exp_gate, w_exp_up, w_exp_down, w_sh_gate, w_sh_up, w_sh_down,
           final_norm_g):
    assert x.shape[0] == 1 and w_ada.shape[0] == 1
    out = _layer(x[0], c[0], positions[0], norm_attn_g[0], w_ada[0], b_ada[0], w_in[0], g_q[0], w_uq[0], g_kv[0],
                 w_ukv[0], g_out_swa[0], g_out_mla[0], w_o[0], norm_ffn_g[0], w_router[0], router_bias[0],
                 w_exp_gate[0], w_exp_up[0], w_exp_down[0], w_sh_gate[0], w_sh_up[0], w_sh_down[0], final_norm_g)
    return out[None]
```

```python
import functools

import jax
import jax.numpy as jnp
from jax import lax
from jax.experimental import pallas as pl
from jax.experimental.pallas import tpu as pltpu

F32 = jnp.float32
BF16 = jnp.bfloat16
I32 = jnp.int32

D_MODEL = 2048
N_HEADS_SWA = 8
HEAD_DIM_SWA = 128
SWA_PATTERNS = ((128, 1), (512, 4), (2048, 16))
SWA_BLOCK = 128
N_HEADS_MLA = 8
Q_LORA_RANK = 512
KV_LORA_RANK = 256
QK_NOPE_DIM = 128
QK_ROPE_DIM = 64
V_HEAD_DIM = 128
ROPE_THETA = 10000.0
D_SWA = N_HEADS_SWA * HEAD_DIM_SWA
D_MLA = N_HEADS_MLA * V_HEAD_DIM
N_EXPERTS = 64
N_GROUPS = 8
TOPK_GROUPS = 4
TOP_K = 8
D_EXPERT = 512
ROUTED_SCALE = 2.5
N_ADA = 6
EPS = 1e-6
NEG_INF = -1e30
LOG2E = 1.4426950408889634

LANES = 128
MLA_QK_PAD = 256
PACK_ROWS = D_MODEL // (2 * LANES)
VMEM_LIMIT = 56 * 1024 * 1024

TM_INPROJ = 512
TM_MLAPROJ = 512
T_MLA = 1024
TM_OUTPROJ = 256
TN_ROUTE = 512
TB_DISPATCH = 256
TR_EXPERT = 256
TB_COMBINE = 128


def _params(*sem):
    return pltpu.CompilerParams(dimension_semantics=sem, vmem_limit_bytes=VMEM_LIMIT)


def _rms(x, g):
    return x * lax.rsqrt(jnp.mean(x * x, axis=-1, keepdims=True) + EPS) * g


def _resident(shape):
    nd = len(shape)
    return pl.BlockSpec(shape, lambda *_: (0,) * nd, pipeline_mode=pl.Buffered(1))


def _pack_bf16_pairs(a, b):
    ua = lax.bitcast_convert_type(a.astype(BF16).astype(F32), I32)
    ub = lax.bitcast_convert_type(b.astype(BF16).astype(F32), I32)
    return lax.shift_right_logical(ua, jnp.int32(16)) | (ub & jnp.int32(-65536))


def _unpack_bf16_pairs(w):
    lo = lax.bitcast_convert_type(lax.shift_left(w, jnp.int32(16)), F32).astype(BF16)
    hi = lax.bitcast_convert_type(w & jnp.int32(-65536), F32).astype(BF16)
    return jnp.concatenate([lo, hi], axis=1)


def _packed_chunk(ref, s, rows):
    return _unpack_bf16_pairs(ref[pl.ds(s, rows, stride=PACK_ROWS), :])


def _ada_body(c_ref, w_ref, b_ref, o_ref):
    c = c_ref[...]
    a = c * jax.nn.sigmoid(c)
    o_ref[...] = jnp.sum(w_ref[...] * a, axis=0, keepdims=True) + b_ref[...]


def _ada(c, w_ada, b_ada):
    d, n = w_ada.shape
    tn = 512
    return pl.pallas_call(
        _ada_body,
        grid=(n // tn,),
        in_specs=[pl.BlockSpec((d, 1), lambda j: (0, 0)),
                  pl.BlockSpec((d, tn), lambda j: (0, j)),
                  pl.BlockSpec((1, tn), lambda j: (0, j))],
        out_specs=pl.BlockSpec((1, tn), lambda j: (0, j)),
        out_shape=jax.ShapeDtypeStruct((1, n), F32),
        compiler_params=_params("parallel"),
        name="ada",
    )(c.reshape(d, 1), w_ada, b_ada.reshape(1, n))


def _inproj_body(x_ref, g_ref, sc_ref, sh_ref, wqkv_ref, wr_ref, qkv_ref, rest_ref):
    h = (_rms(x_ref[...], g_ref[...]) * (1.0 + sc_ref[...]) + sh_ref[...]).astype(BF16)
    qkv_ref[...] = jnp.dot(h, wqkv_ref[...], preferred_element_type=F32).astype(BF16)
    rest_ref[...] = jnp.dot(h, wr_ref[...], preferred_element_type=F32)


def _inproj(x, g, scale, shift, w_qkv, w_rest):
    s, d = x.shape
    tm = min(TM_INPROJ, s)
    n1, n2 = w_qkv.shape[1], w_rest.shape[1]
    row = lambda i: (i, 0)
    vec = pl.BlockSpec((1, d), lambda i: (0, 0))
    return pl.pallas_call(
        _inproj_body,
        grid=(s // tm,),
        in_specs=[pl.BlockSpec((tm, d), row), vec, vec, vec, _resident((d, n1)), _resident((d, n2))],
        out_specs=[pl.BlockSpec((tm, n1), row), pl.BlockSpec((tm, n2), row)],
        out_shape=[jax.ShapeDtypeStruct((s, n1), BF16), jax.ShapeDtypeStruct((s, n2), F32)],
        compiler_params=_params("parallel"),
        name="inproj",
    )(x, g, scale, shift, w_qkv, w_rest)


def _rope_tail(t, c, s1, s2):
    return t * c + pltpu.roll(t, 32, 1) * s1 + pltpu.roll(t, 96, 1) * s2


def _mlaproj_body(rest_ref, gq_ref, gkv_ref, wuq_ref, wukv_ref, c_ref, s1_ref, s2_ref,
                  q_ref, k_ref, v_ref, *, scale):
    rest = rest_ref[...]
    c, s1, s2 = c_ref[...], s1_ref[...], s2_ref[...]
    cq = _rms(rest[:, :Q_LORA_RANK], gq_ref[...]).astype(BF16)
    ckv = _rms(rest[:, Q_LORA_RANK:Q_LORA_RANK + KV_LORA_RANK], gkv_ref[...]).astype(BF16)
    ktail = _rope_tail(rest[:, Q_LORA_RANK + KV_LORA_RANK:], c, s1, s2).astype(BF16)
    q = jnp.dot(cq, wuq_ref[...], preferred_element_type=F32)
    kv = jnp.dot(ckv, wukv_ref[...], preferred_element_type=F32)
    for h in range(N_HEADS_MLA):
        b = h * MLA_QK_PAD
        q_ref[:, b:b + LANES] = (q[:, b:b + LANES] * scale).astype(BF16)
        q_ref[:, b + LANES:b + 2 * LANES] = (_rope_tail(q[:, b + LANES:b + 2 * LANES], c, s1, s2) * scale).astype(BF16)
        k_ref[:, b:b + LANES] = kv[:, b:b + LANES].astype(BF16)
        k_ref[:, b + LANES:b + 2 * LANES] = ktail
        v_ref[:, b:b + LANES] = kv[:, b + LANES:b + 2 * LANES].astype(BF16)
        v_ref[:, b + LANES:b + 2 * LANES] = jnp.ones((q.shape[0], LANES), BF16)


def _mlaproj(rest, g_q, g_kv, w_uq, w_ukv, rc, rs1, rs2):
    s, nr = rest.shape
    tm = min(TM_MLAPROJ, s)
    nq = N_HEADS_MLA * MLA_QK_PAD
    row = lambda i: (i, 0)
    tab = pl.BlockSpec((tm, LANES), row)
    scale = float(QK_NOPE_DIM + QK_ROPE_DIM) ** -0.5 * LOG2E
    return pl.pallas_call(
        functools.partial(_mlaproj_body, scale=scale),
        grid=(s // tm,),
        in_specs=[pl.BlockSpec((tm, nr), row),
                  pl.BlockSpec((1, Q_LORA_RANK), lambda i: (0, 0)),
                  pl.BlockSpec((1, KV_LORA_RANK), lambda i: (0, 0)),
                  _resident(w_uq.shape), _resident(w_ukv.shape), tab, tab, tab],
        out_specs=[pl.BlockSpec((tm, nq), row)] * 3,
        out_shape=[jax.ShapeDtypeStruct((s, nq), BF16)] * 3,
        compiler_params=_params("parallel"),
        name="mlaproj",
    )(rest, g_q, g_kv, w_uq, w_ukv, rc, rs1, rs2)


def _mla_body(q_ref, k_ref, v_ref, o_ref, m_ref, acc_ref, sa_ref, sb_ref, *, t):
    qi = pl.program_id(1)
    q = q_ref[...]
    m_ref[...] = jnp.full(m_ref.shape, NEG_INF, F32)
    acc_ref[...] = jnp.zeros(acc_ref.shape, F32)

    def scores(j, dst):
        k = k_ref[pl.ds(pl.multiple_of(j * t, t), t), :]
        dst[...] = lax.dot_general(q, k, (((1,), (1,)), ((), ())), preferred_element_type=F32)

    def absorb(j, src, masked):
        s = src[...]
        if masked:
            r = lax.broadcasted_iota(I32, (t, t), 0)
            cidx = lax.broadcasted_iota(I32, (t, t), 1)
            s = jnp.where(cidx <= r, s, NEG_INF)
        v = v_ref[pl.ds(pl.multiple_of(j * t, t), t), :]
        m_old = m_ref[...]
        m_new = jnp.maximum(m_old, jnp.max(s, axis=-1, keepdims=True))
        p = jnp.exp2(s - m_new).astype(BF16)
        acc_ref[...] = jnp.exp2(m_old - m_new) * acc_ref[...] + jnp.dot(p, v, preferred_element_type=F32)
        m_ref[...] = m_new

    scores(0, sa_ref)

    def pair(i, carry):
        j = 2 * i
        scores(j + 1, sb_ref)
        absorb(j, sa_ref, False)
        scores(j + 2, sa_ref)
        absorb(j + 1, sb_ref, False)
        return carry

    lax.fori_loop(0, qi // 2, pair, 0)

    @pl.when(qi % 2 == 0)
    def _():
        absorb(qi, sa_ref, True)

    @pl.when(qi % 2 == 1)
    def _():
        scores(qi, sb_ref)
        absorb(qi - 1, sa_ref, False)
        absorb(qi, sb_ref, True)

    acc = acc_ref[...]
    o_ref[...] = acc[:, :V_HEAD_DIM] / acc[:, V_HEAD_DIM:]


def _mla(q, k, v):
    s = q.shape[0]
    t = min(T_MLA, s)
    head_cols = lambda h, i: (0, h)
    return pl.pallas_call(
        functools.partial(_mla_body, t=t),
        grid=(N_HEADS_MLA, s // t),
        in_specs=[pl.BlockSpec((t, MLA_QK_PAD), lambda h, i: (i, h)),
                  pl.BlockSpec((s, MLA_QK_PAD), head_cols, pipeline_mode=pl.Buffered(1)),
                  pl.BlockSpec((s, MLA_QK_PAD), head_cols, pipeline_mode=pl.Buffered(1))],
        out_specs=pl.BlockSpec((t, V_HEAD_DIM), lambda h, i: (i, h)),
        out_shape=jax.ShapeDtypeStruct((s, D_MLA), F32),
        scratch_shapes=[pltpu.VMEM((t, 1), F32), pltpu.VMEM((t, MLA_QK_PAD), F32),
                        pltpu.VMEM((t, t), F32), pltpu.VMEM((t, t), F32)],
        compiler_params=_params("parallel", "arbitrary"),
        name="mla",
    )(q, k, v)


def _dilated_body(q_ref, kc_ref, kp_ref, vc_ref, vp_ref, pq_ref, pkc_ref, pkp_ref, o_ref, lse_ref):
    n = pl.program_id(1)
    blk = SWA_BLOCK
    i = lax.broadcasted_iota(I32, (blk, blk), 0)
    j = lax.broadcasted_iota(I32, (blk, blk), 1)
    ok_cur = j <= i
    ok_prev = (j >= i) & (n > 0)
    pq = pq_ref[...]
    dist_cur = jnp.abs(pq - pkc_ref[0])
    dist_prev = jnp.abs(pq - pkp_ref[0])
    scale = float(HEAD_DIM_SWA) ** -0.5
    nt = (((1,), (1,)), ((), ()))
    for h in range(N_HEADS_SWA):
        hs = slice(h * HEAD_DIM_SWA, (h + 1) * HEAD_DIM_SWA)
        slope = 2.0 ** (-8.0 * (h + 1) / N_HEADS_SWA)
        q = q_ref[:, hs]
        sc = lax.dot_general(q, kc_ref[:, hs], nt, preferred_element_type=F32) * scale - slope * dist_cur
        sp = lax.dot_general(q, kp_ref[:, hs], nt, preferred_element_type=F32) * scale - slope * dist_prev
        sc = jnp.where(ok_cur, sc, NEG_INF)
        sp = jnp.where(ok_prev, sp, NEG_INF)
        m = jnp.maximum(jnp.max(sc, axis=-1, keepdims=True), jnp.max(sp, axis=-1, keepdims=True))
        pc = jnp.exp(sc - m)
        pp = jnp.exp(sp - m)
        den = jnp.sum(pc, axis=-1, keepdims=True) + jnp.sum(pp, axis=-1, keepdims=True)
        o = (jnp.dot(pc.astype(BF16), vc_ref[:, hs], preferred_element_type=F32)
             + jnp.dot(pp.astype(BF16), vp_ref[:, hs], preferred_element_type=F32))
        o_ref[:, hs] = o / den
        lse_ref[:, hs] = jnp.broadcast_to(m + jnp.log(den), (blk, HEAD_DIM_SWA))


def _dilated(qkv, pos_lanes, pos_rows, dil):
    s = qkv.shape[0]
    sd = s // dil
    nb = sd // SWA_BLOCK
    blk = SWA_BLOCK
    qkv_v = qkv.reshape(sd, dil * 3 * D_SWA)
    pq_v = pos_lanes.reshape(sd, dil * LANES)
    prev = lambda n: jnp.maximum(n - 1, 0)
    wide = (blk, D_SWA)
    o, lse = pl.pallas_call(
        _dilated_body,
        grid=(dil, nb),
        in_specs=[pl.BlockSpec(wide, lambda r, n: (n, 3 * r)),
                  pl.BlockSpec(wide, lambda r, n: (n, 3 * r + 1)),
                  pl.BlockSpec(wide, lambda r, n: (prev(n), 3 * r + 1)),
                  pl.BlockSpec(wide, lambda r, n: (n, 3 * r + 2)),
                  pl.BlockSpec(wide, lambda r, n: (prev(n), 3 * r + 2)),
                  pl.BlockSpec((blk, LANES), lambda r, n: (n, r)),
                  pl.BlockSpec((1, 1, blk), lambda r, n: (r, 0, n)),
                  pl.BlockSpec((1, 1, blk), lambda r, n: (r, 0, prev(n)))],
        out_specs=[pl.BlockSpec(wide, lambda r, n: (n, r)), pl.BlockSpec(wide, lambda r, n: (n, r))],
        out_shape=[jax.ShapeDtypeStruct((sd, dil * D_SWA), F32), jax.ShapeDtypeStruct((sd, dil * D_SWA), F32)],
        compiler_params=_params("parallel", "parallel"),
        name=f"dil{dil}",
    )(qkv_v, qkv_v, qkv_v, qkv_v, qkv_v, pq_v, pos_rows, pos_rows)
    return o.reshape(s, D_SWA), lse.reshape(s, D_SWA)


def _outproj_body(o1_ref, o2_ref, o3_ref, l1_ref, l2_ref, l3_ref, ob_ref, x_ref, gsw_ref, gml_ref, wo_ref,
                  ga_ref, nfg_ref, scf_ref, shf_ref, wrt_ref, x1_ref, h2p_ref, lgt_ref, *, tm):
    l1, l2, l3 = l1_ref[...], l2_ref[...], l3_ref[...]
    m = jnp.maximum(jnp.maximum(l1, l2), l3)
    e1, e2, e3 = jnp.exp(l1 - m), jnp.exp(l2 - m), jnp.exp(l3 - m)
    oa = (e1 * o1_ref[...] + e2 * o2_ref[...] + e3 * o3_ref[...]) / (e1 + e2 + e3)
    mix = jnp.concatenate([_rms(oa, gsw_ref[...]), _rms(ob_ref[...], gml_ref[...])], axis=-1).astype(BF16)
    x1 = x_ref[...] + ga_ref[...] * jnp.dot(mix, wo_ref[...], preferred_element_type=F32)
    x1_ref[...] = x1
    h2 = _rms(x1, nfg_ref[...]) * (1.0 + scf_ref[...]) + shf_ref[...]
    lgt_ref[...] = lax.dot_general(wrt_ref[...], h2, (((1,), (1,)), ((), ())),
                                   precision=lax.Precision.HIGHEST, preferred_element_type=F32)
    for s in range(PACK_ROWS):
        b = 2 * LANES * s
        h2p_ref[pl.ds(s, tm, stride=PACK_ROWS), :] = _pack_bf16_pairs(h2[:, b:b + LANES], h2[:, b + LANES:b + 2 * LANES])


def _outproj(o_pats, lse_pats, o_b, x, g_sw, g_ml, w_o, gate_a, nfg, scale_f, shift_f, w_router_t):
    s, d = x.shape
    tm = min(TM_OUTPROJ, s)
    row = lambda i: (i, 0)
    half = pl.BlockSpec((tm, D_SWA), row)
    vec = lambda n: pl.BlockSpec((1, n), lambda i: (0, 0))
    return pl.pallas_call(
        functools.partial(_outproj_body, tm=tm),
        grid=(s // tm,),
        in_specs=[half] * 7 + [pl.BlockSpec((tm, d), row), vec(D_SWA), vec(D_MLA), _resident(w_o.shape),
                                vec(d), vec(d), vec(d), vec(d), _resident(w_router_t.shape)],
        out_specs=[pl.BlockSpec((tm, d), row), pl.BlockSpec((tm * PACK_ROWS, LANES), row),
                   pl.BlockSpec((N_EXPERTS, tm), lambda i: (0, i))],
        out_shape=[jax.ShapeDtypeStruct((s, d), F32), jax.ShapeDtypeStruct((s * PACK_ROWS, LANES), I32),
                   jax.ShapeDtypeStruct((N_EXPERTS, s), F32)],
        compiler_params=_params("parallel"),
        name="outproj",
    )(*o_pats, *lse_pats, o_b, x, g_sw, g_ml, w_o, gate_a, nfg, scale_f, shift_f, w_router_t)


def _first_index(hit_value, x, iota, size, axis):
    return jnp.min(jnp.where(x == hit_value, iota, size), axis=axis, keepdims=True)


def _route_body(lgt_ref, bias_ref, tri_ref, eidx_ref, wts_ref, rank_ref, cnt_ref, carry_ref, *, tn):
    @pl.when(pl.program_id(0) == 0)
    def _():
        carry_ref[...] = jnp.zeros(carry_ref.shape, F32)

    gsz = N_EXPERTS // N_GROUPS
    scores = jax.nn.sigmoid(lgt_ref[...])
    choice = scores + bias_ref[...]
    neg = jnp.float32(-jnp.inf)

    g3 = choice.reshape(N_GROUPS, gsz, tn)
    i3 = lax.broadcasted_iota(I32, g3.shape, 1)
    m1 = jnp.max(g3, axis=1, keepdims=True)
    f1 = _first_index(m1, g3, i3, gsz, 1)
    m2 = jnp.max(jnp.where(i3 == f1, neg, g3), axis=1, keepdims=True)
    gs = (m1 + m2).reshape(N_GROUPS, tn)

    ig = lax.broadcasted_iota(I32, gs.shape, 0)
    gsel = jnp.zeros(gs.shape, F32)
    for _ in range(TOPK_GROUPS):
        hit = ig == _first_index(jnp.max(gs, axis=0, keepdims=True), gs, ig, N_GROUPS, 0)
        gsel = jnp.where(hit, 1.0, gsel)
        gs = jnp.where(hit, neg, gs)
    emask = jnp.broadcast_to(gsel.reshape(N_GROUPS, 1, tn), (N_GROUPS, gsz, tn)).reshape(N_EXPERTS, tn)
    cand = jnp.where(emask > 0.0, choice, NEG_INF)

    ie = lax.broadcasted_iota(I32, cand.shape, 0)
    picks, wsel = [], []
    onehot = jnp.zeros(cand.shape, F32)
    for _ in range(TOP_K):
        f = _first_index(jnp.max(cand, axis=0, keepdims=True), cand, ie, N_EXPERTS, 0)
        hit = ie == f
        picks.append(f)
        wsel.append(jnp.sum(jnp.where(hit, scores, 0.0), axis=0, keepdims=True))
        onehot = jnp.where(hit, 1.0, onehot)
        cand = jnp.where(hit, neg, cand)

    rank = carry_ref[...] + jnp.dot(onehot.astype(BF16), tri_ref[...], preferred_element_type=F32)
    carry_ref[...] = carry_ref[...] + jnp.sum(onehot, axis=1, keepdims=True)
    cnt_ref[...] = carry_ref[...]

    w = jnp.concatenate(wsel, axis=0)
    wts_ref[...] = w / jnp.sum(w, axis=0, keepdims=True) * ROUTED_SCALE
    eidx_ref[...] = jnp.concatenate(picks, axis=0)
    rank_ref[...] = jnp.concatenate(
        [jnp.sum(jnp.where(ie == f, rank, 0.0), axis=0, keepdims=True) for f in picks], axis=0).astype(I32)


def _route(logits_t, router_bias):
    e, n = logits_t.shape
    tn = min(TN_ROUTE, n)
    tri = (lax.broadcasted_iota(I32, (tn, tn), 0) < lax.broadcasted_iota(I32, (tn, tn), 1)).astype(BF16)
    col = lambda i: (0, i)
    return pl.pallas_call(
        functools.partial(_route_body, tn=tn),
        grid=(n // tn,),
        in_specs=[pl.BlockSpec((e, tn), col), pl.BlockSpec((e, 1), lambda i: (0, 0)),
                  pl.BlockSpec((tn, tn), lambda i: (0, 0))],
        out_specs=[pl.BlockSpec((TOP_K, tn), col), pl.BlockSpec((TOP_K, tn), col), pl.BlockSpec((TOP_K, tn), col),
                   pl.BlockSpec((e, 1), lambda i: (0, 0))],
        out_shape=[jax.ShapeDtypeStruct((TOP_K, n), I32), jax.ShapeDtypeStruct((TOP_K, n), F32),
                   jax.ShapeDtypeStruct((TOP_K, n), I32), jax.ShapeDtypeStruct((e, 1), F32)],
        scratch_shapes=[pltpu.VMEM((e, 1), F32)],
        compiler_params=_params("arbitrary"),
        name="route",
    )(logits_t, router_bias.reshape(e, 1), tri)


def _dispatch_body(dest_ref, h_ref, init_ref, xs_ref, sem, *, tb):
    del init_ref

    def row_copy(t, k):
        src = h_ref.at[pl.ds(pl.multiple_of(t * PACK_ROWS, PACK_ROWS), PACK_ROWS), :]
        dst = xs_ref.at[pl.ds(pl.multiple_of(dest_ref[k, t] * PACK_ROWS, PACK_ROWS), PACK_ROWS), :]
        return pltpu.make_async_copy(src, dst, sem)

    def issue(t, carry):
        for k in range(TOP_K):
            row_copy(t, k).start()
        return carry

    def drain(t, carry):
        for k in range(TOP_K):
            row_copy(t, k).wait()
        return carry

    lax.fori_loop(0, tb, issue, 0)
    lax.fori_loop(0, tb, drain, 0)


def _dispatch(dest, h2p, n_slots):
    n = dest.shape[1]
    tb = min(TB_DISPATCH, n)
    init = jnp.zeros((n_slots * PACK_ROWS, LANES), I32)
    return pl.pallas_call(
        functools.partial(_dispatch_body, tb=tb),
        grid=(n // tb,),
        in_specs=[pl.BlockSpec((TOP_K, tb), lambda i: (0, i), memory_space=pltpu.SMEM),
                  pl.BlockSpec((tb * PACK_ROWS, LANES), lambda i: (i, 0)),
                  pl.BlockSpec(memory_space=pl.ANY)],
        out_specs=pl.BlockSpec(memory_space=pl.ANY),
        out_shape=jax.ShapeDtypeStruct(init.shape, I32),
        scratch_shapes=[pltpu.SemaphoreType.DMA(())],
        input_output_aliases={2: 0},
        compiler_params=_params("arbitrary"),
        name="dispatch",
    )(dest, h2p, init)


def _swiglu_packed(xp_ref, wgu_ref, wd_ref, rows):
    x = jnp.concatenate([_packed_chunk(xp_ref, s, rows) for s in range(PACK_ROWS)], axis=1)
    h = jnp.dot(x, wgu_ref[...], preferred_element_type=F32)
    g, u = h[:, :D_EXPERT], h[:, D_EXPERT:]
    a = (g * jax.nn.sigmoid(g) * u).astype(BF16)
    return jnp.dot(a, wd_ref[...], preferred_element_type=F32)


def _experts_body(be_ref, nv_ref, new_ref, xs_ref, wg_ref, wu_ref, wd_ref, y_ref, wgu_s, wd_s, *, tr):
    del be_ref
    b = pl.program_id(0)

    @pl.when(new_ref[b] > 0)
    def _():
        wgu_s[:, :D_EXPERT] = wg_ref[0].astype(BF16)
        wgu_s[:, D_EXPERT:] = wu_ref[0].astype(BF16)
        wd_s[...] = wd_ref[0].astype(BF16)

    @pl.when(nv_ref[b] > 0)
    def _():
        y = _swiglu_packed(xs_ref, wgu_s, wd_s, tr)
        for s in range(PACK_ROWS):
            c = 2 * LANES * s
            y_ref[pl.ds(s, tr, stride=PACK_ROWS), :] = _pack_bf16_pairs(y[:, c:c + LANES], y[:, c + LANES:c + 2 * LANES])

    @pl.when(nv_ref[b] == 0)
    def _():
        y_ref[...] = jnp.zeros(y_ref.shape, I32)


def _experts(blk_e, blk_nv, blk_new, xs, w_gate, w_up, w_down):
    tr = TR_EXPERT
    nb = blk_e.shape[0]
    d, f = w_gate.shape[1], w_gate.shape[2]
    grid_spec = pltpu.PrefetchScalarGridSpec(
        num_scalar_prefetch=3,
        grid=(nb,),
        in_specs=[pl.BlockSpec((tr * PACK_ROWS, LANES), lambda b, be, nv, nw: (b, 0)),
                  pl.BlockSpec((1, d, f), lambda b, be, nv, nw: (be[b], 0, 0)),
                  pl.BlockSpec((1, d, f), lambda b, be, nv, nw: (be[b], 0, 0)),
                  pl.BlockSpec((1, f, d), lambda b, be, nv, nw: (be[b], 0, 0))],
        out_specs=pl.BlockSpec((tr * PACK_ROWS, LANES), lambda b, be, nv, nw: (b, 0)),
        scratch_shapes=[pltpu.VMEM((d, 2 * f), BF16), pltpu.VMEM((f, d), BF16)],
    )
    return pl.pallas_call(
        functools.partial(_experts_body, tr=tr),
        grid_spec=grid_spec,
        out_shape=jax.ShapeDtypeStruct((nb * tr * PACK_ROWS, LANES), I32),
        compiler_params=_params("arbitrary"),
        name="experts",
    )(blk_e, blk_nv, blk_new, xs, w_gate, w_up, w_down)


def _combine_body(dest_ref, wts_ref, h2p_ref, x1_ref, gf_ref, fg_ref, wsgu_ref, wsd_ref, y_ref,
                  out_ref, ybuf, sem, *, tb):
    def row_copy(t, k):
        src = y_ref.at[pl.ds(pl.multiple_of(dest_ref[k, t] * PACK_ROWS, PACK_ROWS), PACK_ROWS), :]
        dst = ybuf.at[pl.ds(pl.multiple_of((k * tb + t) * PACK_ROWS, PACK_ROWS), PACK_ROWS), :]
        return pltpu.make_async_copy(src, dst, sem)

    def issue(t, carry):
        for k in range(TOP_K):
            row_copy(t, k).start()
        return carry

    def drain(t, carry):
        for k in range(TOP_K):
            row_copy(t, k).wait()
        return carry

    lax.fori_loop(0, tb, issue, 0)
    shared = _swiglu_packed(h2p_ref, wsgu_ref, wsd_ref, tb)
    lax.fori_loop(0, tb, drain, 0)

    wts = wts_ref[...]
    wb = [jnp.broadcast_to(wts[:, k:k + 1], (tb, LANES)) for k in range(TOP_K)]
    cols = []
    for s in range(PACK_ROWS):
        c = 2 * LANES * s
        lo, hi = shared[:, c:c + LANES], shared[:, c + LANES:c + 2 * LANES]
        for k in range(TOP_K):
            w = ybuf[pl.ds(k * tb * PACK_ROWS + s, tb, stride=PACK_ROWS), :]
            lo = lo + wb[k] * lax.bitcast_convert_type(lax.shift_left(w, jnp.int32(16)), F32)
            hi = hi + wb[k] * lax.bitcast_convert_type(w & jnp.int32(-65536), F32)
        cols += [lo, hi]
    moe = jnp.concatenate(cols, axis=1)
    out_ref[...] = _rms(x1_ref[...] + gf_ref[...] * moe, fg_ref[...])


def _combine(dest, wts_t, h2p, x1, gate_f, final_g, w_sgu, w_sd, y):
    n, d = x1.shape
    tb = min(TB_COMBINE, n)
    row = lambda i: (i, 0)
    vec = pl.BlockSpec((1, d), lambda i: (0, 0))
    return pl.pallas_call(
        functools.partial(_combine_body, tb=tb),
        grid=(n // tb,),
        in_specs=[pl.BlockSpec((TOP_K, tb), lambda i: (0, i), memory_space=pltpu.SMEM),
                  pl.BlockSpec((tb, TOP_K), row),
                  pl.BlockSpec((tb * PACK_ROWS, LANES), row),
                  pl.BlockSpec((tb, d), row), vec, vec,
                  _resident(w_sgu.shape), _resident(w_sd.shape),
                  pl.BlockSpec(memory_space=pl.ANY)],
        out_specs=pl.BlockSpec((tb, d), row),
        out_shape=jax.ShapeDtypeStruct((n, d), F32),
        scratch_shapes=[pltpu.VMEM((TOP_K * tb * PACK_ROWS, LANES), I32), pltpu.SemaphoreType.DMA(())],
        compiler_params=_params("arbitrary"),
        name="combine",
    )(dest, wts_t, h2p, x1, gate_f, final_g, w_sgu, w_sd, y)


def _rope_tables(pos):
    half = QK_ROPE_DIM // 2
    inv_freq = ROPE_THETA ** (-jnp.arange(half, dtype=F32) / half)
    ang = pos.astype(F32)[:, None] * inv_freq
    cos, sin = jnp.cos(ang), jnp.sin(ang)
    z = jnp.zeros_like(cos)
    c = jnp.concatenate([cos, cos, z, z], axis=1)
    s1 = jnp.concatenate([z, sin, z, z], axis=1)
    s2 = jnp.concatenate([-sin, z, z, z], axis=1)
    return c, s1, s2


def _layer(x, c, pos, norm_attn_g, w_ada, b_ada, w_in, g_q, w_uq, g_kv, w_ukv, g_out_swa, g_out_mla, w_o,
           norm_ffn_g, w_router, router_bias, w_exp_gate, w_exp_up, w_exp_down, w_sh_gate, w_sh_up, w_sh_down,
           final_g):
    s, d = x.shape
    row = lambda a: a.reshape(1, -1)

    mod = _ada(c, w_ada, b_ada)
    shift_a, scale_a, gate_a, shift_f, scale_f, gate_f = [mod[:, i * d:(i + 1) * d] for i in range(N_ADA)]

    n_qkv = 3 * D_SWA
    w_qkv = w_in[:, :n_qkv].astype(BF16)
    w_rest = jnp.pad(w_in[:, n_qkv:], ((0, 0), (0, LANES - QK_ROPE_DIM))).astype(BF16)
    qkv, rest = _inproj(x, row(norm_attn_g), scale_a, shift_a, w_qkv, w_rest)

    dq = QK_NOPE_DIM + QK_ROPE_DIM
    w_uq_p = jnp.pad(w_uq.reshape(Q_LORA_RANK, N_HEADS_MLA, dq), ((0, 0), (0, 0), (0, MLA_QK_PAD - dq)))
    w_uq_p = w_uq_p.reshape(Q_LORA_RANK, N_HEADS_MLA * MLA_QK_PAD).astype(BF16)
    rc, rs1, rs2 = _rope_tables(pos)
    q_m, k_m, v_m = _mlaproj(rest, row(g_q), row(g_kv), w_uq_p, w_ukv.astype(BF16), rc, rs1, rs2)
    o_b = _mla(q_m, k_m, v_m)

    posf = pos.astype(F32)
    pos_lanes = jnp.broadcast_to(posf[:, None], (s, LANES))
    o_pats, lse_pats = [], []
    for window, dil in SWA_PATTERNS:
        assert window // dil == SWA_BLOCK and s % (dil * SWA_BLOCK) == 0
        pos_rows = posf.reshape(s // dil, dil).T.reshape(dil, 1, s // dil)
        o_p, lse_p = _dilated(qkv, pos_lanes, pos_rows, dil)
        o_pats.append(o_p)
        lse_pats.append(lse_p)

    x1, h2p, logits_t = _outproj(o_pats, lse_pats, o_b, x, row(g_out_swa), row(g_out_mla), w_o.astype(BF16),
                                 gate_a, row(norm_ffn_g), scale_f, shift_f, w_router.T)

    eidx, wts, rank, cnt = _route(logits_t, router_bias)
    tr = TR_EXPERT
    counts = cnt[:, 0].astype(I32)
    padded = (counts + tr - 1) // tr * tr
    e_ids = jnp.arange(N_EXPERTS, dtype=I32)
    pad_end = jnp.sum(jnp.where(e_ids[None, :] <= e_ids[:, None], padded[None, :], 0), axis=1)
    pad_start = pad_end - padded
    lookup = lambda table, idx: jnp.sum(jnp.where(idx[..., None] == e_ids, table, 0), axis=-1)
    dest = lookup(pad_start, eidx) + rank
    n_slots = s * TOP_K + N_EXPERTS * tr
    blk_start = jnp.arange(n_slots // tr, dtype=I32) * tr
    blk_e = jnp.minimum(jnp.sum((pad_end[None, :] <= blk_start[:, None]).astype(I32), axis=1), N_EXPERTS - 1)
    blk_nv = jnp.clip(lookup(counts, blk_e) - (blk_start - lookup(pad_start, blk_e)), 0, tr)
    blk_new = ((blk_nv > 0) & (blk_start == lookup(pad_start, blk_e))).astype(I32)

    xs = _dispatch(dest, h2p, n_slots)
    y = _experts(blk_e, blk_nv, blk_new, xs, w_exp_gate, w_exp_up, w_exp_down)
    w_sgu = jnp.concatenate([w_sh_gate, w_sh_up], axis=1).astype(BF16)
    return _combine(dest, wts.T, h2p, x1, gate_f, row(final_g), w_sgu, w_sh_down.astype(BF16), y)


def kernel(x, c, positions, norm_attn_g, w_ada, b_ada, w_in, g_q, w_uq, g_kv, w_ukv, g_out_swa, g_out_mla, w_o,
           norm_ffn_g, w_router, router_bias, w_exp_gate, w_exp_up, w_exp_down, w_sh_gate, w_sh_up, w_sh_down,
           final_norm_g):
    assert x.shape[0] == 1 and w_ada.shape[0] == 1
    out = _layer(x[0], c[0], positions[0], norm_attn_g[0], w_ada[0], b_ada[0], w_in[0], g_q[0], w_uq[0], g_kv[0],
                 w_ukv[0], g_out_swa[0], g_out_mla[0], w_o[0], norm_ffn_g[0], w_router[0], router_bias[0],
                 w_exp_gate[0], w_exp_up[0], w_exp_down[0], w_sh_gate[0], w_sh_up[0], w_sh_down[0], final_norm_g)
    return out[None]
```

```python
import functools

import jax
import jax.numpy as jnp
from jax import lax
from jax.experimental import pallas as pl
from jax.experimental.pallas import tpu as pltpu

F32 = jnp.float32
BF16 = jnp.bfloat16
I32 = jnp.int32

D_MODEL = 2048
N_HEADS_SWA = 8
HEAD_DIM_SWA = 128
SWA_PATTERNS = ((128, 1), (512, 4), (2048, 16))
SWA_BLOCK = 128
N_HEADS_MLA = 8
Q_LORA_RANK = 512
KV_LORA_RANK = 256
QK_NOPE_DIM = 128
QK_ROPE_DIM = 64
V_HEAD_DIM = 128
ROPE_THETA = 10000.0
D_SWA = N_HEADS_SWA * HEAD_DIM_SWA
D_MLA = N_HEADS_MLA * V_HEAD_DIM
N_EXPERTS = 64
N_GROUPS = 8
TOPK_GROUPS = 4
TOP_K = 8
D_EXPERT = 512
ROUTED_SCALE = 2.5
N_ADA = 6
EPS = 1e-6
NEG_INF = -1e30
LOG2E = 1.4426950408889634

LANES = 128
MLA_QK_PAD = 256
PACK_ROWS = D_MODEL // (2 * LANES)
VMEM_LIMIT = 56 * 1024 * 1024

TM_INPROJ = 512
TM_MLAPROJ = 512
T_MLA = 1024
TM_OUTPROJ = 256
OUTPROJ_SUBTILES = 2
TN_ROUTE = 512
TB_DISPATCH = 256
TR_EXPERT = 256
EXPERT_SUBTILES = 2
TB_COMBINE = 128


def _params(*sem):
    return pltpu.CompilerParams(dimension_semantics=sem, vmem_limit_bytes=VMEM_LIMIT)


def _rms(x, g):
    return x * lax.rsqrt(jnp.mean(x * x, axis=-1, keepdims=True) + EPS) * g


def _resident(shape):
    nd = len(shape)
    return pl.BlockSpec(shape, lambda *_: (0,) * nd, pipeline_mode=pl.Buffered(1))


def _pack_bf16_pairs(a, b):
    ua = lax.bitcast_convert_type(a.astype(BF16).astype(F32), I32)
    ub = lax.bitcast_convert_type(b.astype(BF16).astype(F32), I32)
    return lax.shift_right_logical(ua, jnp.int32(16)) | (ub & jnp.int32(-65536))


def _unpack_bf16_pairs(w):
    lo = lax.bitcast_convert_type(lax.shift_left(w, jnp.int32(16)), F32).astype(BF16)
    hi = lax.bitcast_convert_type(w & jnp.int32(-65536), F32).astype(BF16)
    return jnp.concatenate([lo, hi], axis=1)


def _packed_chunk(ref, s, rows):
    return _unpack_bf16_pairs(ref[pl.ds(s, rows, stride=PACK_ROWS), :])


def _ada_body(c_ref, w_ref, b_ref, o_ref):
    c = c_ref[...]
    a = c * jax.nn.sigmoid(c)
    o_ref[...] = jnp.sum(w_ref[...] * a, axis=0, keepdims=True) + b_ref[...]


def _ada(c, w_ada, b_ada):
    d, n = w_ada.shape
    tn = 512
    return pl.pallas_call(
        _ada_body,
        grid=(n // tn,),
        in_specs=[pl.BlockSpec((d, 1), lambda j: (0, 0)),
                  pl.BlockSpec((d, tn), lambda j: (0, j)),
                  pl.BlockSpec((1, tn), lambda j: (0, j))],
        out_specs=pl.BlockSpec((1, tn), lambda j: (0, j)),
        out_shape=jax.ShapeDtypeStruct((1, n), F32),
        compiler_params=_params("parallel"),
        name="ada",
    )(c.reshape(d, 1), w_ada, b_ada.reshape(1, n))


def _inproj_body(x_ref, g_ref, sc_ref, sh_ref, wqkv_ref, wr_ref, qkv_ref, rest_ref):
    h = (_rms(x_ref[...], g_ref[...]) * (1.0 + sc_ref[...]) + sh_ref[...]).astype(BF16)
    qkv_ref[...] = jnp.dot(h, wqkv_ref[...], preferred_element_type=F32).astype(BF16)
    rest_ref[...] = jnp.dot(h, wr_ref[...], preferred_element_type=F32)


def _inproj(x, g, scale, shift, w_qkv, w_rest):
    s, d = x.shape
    tm = min(TM_INPROJ, s)
    n1, n2 = w_qkv.shape[1], w_rest.shape[1]
    row = lambda i: (i, 0)
    vec = pl.BlockSpec((1, d), lambda i: (0, 0))
    return pl.pallas_call(
        _inproj_body,
        grid=(s // tm,),
        in_specs=[pl.BlockSpec((tm, d), row), vec, vec, vec, _resident((d, n1)), _resident((d, n2))],
        out_specs=[pl.BlockSpec((tm, n1), row), pl.BlockSpec((tm, n2), row)],
        out_shape=[jax.ShapeDtypeStruct((s, n1), BF16), jax.ShapeDtypeStruct((s, n2), F32)],
        compiler_params=_params("parallel"),
        name="inproj",
    )(x, g, scale, shift, w_qkv, w_rest)


def _rope_tail(t, c, s1, s2):
    return t * c + pltpu.roll(t, 32, 1) * s1 + pltpu.roll(t, 96, 1) * s2


def _mlaproj_body(rest_ref, gq_ref, gkv_ref, wuq_ref, wukv_ref, c_ref, s1_ref, s2_ref,
                  q_ref, k_ref, v_ref, *, scale):
    rest = rest_ref[...]
    c, s1, s2 = c_ref[...], s1_ref[...], s2_ref[...]
    cq = _rms(rest[:, :Q_LORA_RANK], gq_ref[...]).astype(BF16)
    ckv = _rms(rest[:, Q_LORA_RANK:Q_LORA_RANK + KV_LORA_RANK], gkv_ref[...]).astype(BF16)
    ktail = _rope_tail(rest[:, Q_LORA_RANK + KV_LORA_RANK:], c, s1, s2).astype(BF16)
    q = jnp.dot(cq, wuq_ref[...], preferred_element_type=F32)
    kv = jnp.dot(ckv, wukv_ref[...], preferred_element_type=F32)
    for h in range(N_HEADS_MLA):
        b = h * MLA_QK_PAD
        q_ref[:, b:b + LANES] = (q[:, b:b + LANES] * scale).astype(BF16)
        q_ref[:, b + LANES:b + 2 * LANES] = (_rope_tail(q[:, b + LANES:b + 2 * LANES], c, s1, s2) * scale).astype(BF16)
        k_ref[:, b:b + LANES] = kv[:, b:b + LANES].astype(BF16)
        k_ref[:, b + LANES:b + 2 * LANES] = ktail
        v_ref[:, b:b + LANES] = kv[:, b + LANES:b + 2 * LANES].astype(BF16)
        v_ref[:, b + LANES:b + 2 * LANES] = jnp.ones((q.shape[0], LANES), BF16)


def _mlaproj(rest, g_q, g_kv, w_uq, w_ukv, rc, rs1, rs2):
    s, nr = rest.shape
    tm = min(TM_MLAPROJ, s)
    nq = N_HEADS_MLA * MLA_QK_PAD
    row = lambda i: (i, 0)
    tab = pl.BlockSpec((tm, LANES), row)
    scale = float(QK_NOPE_DIM + QK_ROPE_DIM) ** -0.5 * LOG2E
    return pl.pallas_call(
        functools.partial(_mlaproj_body, scale=scale),
        grid=(s // tm,),
        in_specs=[pl.BlockSpec((tm, nr), row),
                  pl.BlockSpec((1, Q_LORA_RANK), lambda i: (0, 0)),
                  pl.BlockSpec((1, KV_LORA_RANK), lambda i: (0, 0)),
                  _resident(w_uq.shape), _resident(w_ukv.shape), tab, tab, tab],
        out_specs=[pl.BlockSpec((tm, nq), row)] * 3,
        out_shape=[jax.ShapeDtypeStruct((s, nq), BF16)] * 3,
        compiler_params=_params("parallel"),
        name="mlaproj",
    )(rest, g_q, g_kv, w_uq, w_ukv, rc, rs1, rs2)


def _mla_body(q_ref, k_ref, v_ref, o_ref, m_ref, acc_ref, sa_ref, sb_ref, *, t):
    qi = pl.program_id(1)
    q = q_ref[...]
    m_ref[...] = jnp.full(m_ref.shape, NEG_INF, F32)
    acc_ref[...] = jnp.zeros(acc_ref.shape, F32)

    def scores(j, dst):
        k = k_ref[pl.ds(pl.multiple_of(j * t, t), t), :]
        dst[...] = lax.dot_general(q, k, (((1,), (1,)), ((), ())), preferred_element_type=F32)

    def absorb(j, src, masked):
        s = src[...]
        if masked:
            r = lax.broadcasted_iota(I32, (t, t), 0)
            cidx = lax.broadcasted_iota(I32, (t, t), 1)
            s = jnp.where(cidx <= r, s, NEG_INF)
        v = v_ref[pl.ds(pl.multiple_of(j * t, t), t), :]
        m_old = m_ref[...]
        m_new = jnp.maximum(m_old, jnp.max(s, axis=-1, keepdims=True))
        p = jnp.exp2(s - m_new).astype(BF16)
        acc_ref[...] = jnp.exp2(m_old - m_new) * acc_ref[...] + jnp.dot(p, v, preferred_element_type=F32)
        m_ref[...] = m_new

    scores(0, sa_ref)

    def pair(i, carry):
        j = 2 * i
        scores(j + 1, sb_ref)
        absorb(j, sa_ref, False)
        scores(j + 2, sa_ref)
        absorb(j + 1, sb_ref, False)
        return carry

    lax.fori_loop(0, qi // 2, pair, 0)

    @pl.when(qi % 2 == 0)
    def _():
        absorb(qi, sa_ref, True)

    @pl.when(qi % 2 == 1)
    def _():
        scores(qi, sb_ref)
        absorb(qi - 1, sa_ref, False)
        absorb(qi, sb_ref, True)

    acc = acc_ref[...]
    o_ref[...] = acc[:, :V_HEAD_DIM] / acc[:, V_HEAD_DIM:]


def _mla(q, k, v):
    s = q.shape[0]
    t = min(T_MLA, s)
    head_cols = lambda h, i: (0, h)
    return pl.pallas_call(
        functools.partial(_mla_body, t=t),
        grid=(N_HEADS_MLA, s // t),
        in_specs=[pl.BlockSpec((t, MLA_QK_PAD), lambda h, i: (i, h)),
                  pl.BlockSpec((s, MLA_QK_PAD), head_cols, pipeline_mode=pl.Buffered(1)),
                  pl.BlockSpec((s, MLA_QK_PAD), head_cols, pipeline_mode=pl.Buffered(1))],
        out_specs=pl.BlockSpec((t, V_HEAD_DIM), lambda h, i: (i, h)),
        out_shape=jax.ShapeDtypeStruct((s, D_MLA), F32),
        scratch_shapes=[pltpu.VMEM((t, 1), F32), pltpu.VMEM((t, MLA_QK_PAD), F32),
                        pltpu.VMEM((t, t), F32), pltpu.VMEM((t, t), F32)],
        compiler_params=_params("parallel", "arbitrary"),
        name="mla",
    )(q, k, v)


def _dilated_body(q_ref, kc_ref, kp_ref, vc_ref, vp_ref, pq_ref, pkc_ref, pkp_ref, o_ref, lse_ref):
    n = pl.program_id(1)
    blk = SWA_BLOCK
    i = lax.broadcasted_iota(I32, (blk, blk), 0)
    j = lax.broadcasted_iota(I32, (blk, blk), 1)
    ok_cur = j <= i
    ok_prev = (j >= i) & (n > 0)
    pq = pq_ref[...]
    dist_cur = jnp.abs(pq - pkc_ref[0])
    dist_prev = jnp.abs(pq - pkp_ref[0])
    scale = float(HEAD_DIM_SWA) ** -0.5
    nt = (((1,), (1,)), ((), ()))
    heads = range(N_HEADS_SWA)
    hs = [slice(h * HEAD_DIM_SWA, (h + 1) * HEAD_DIM_SWA) for h in heads]
    slope = [2.0 ** (-8.0 * (h + 1) / N_HEADS_SWA) for h in heads]
    ones = jnp.ones((blk, HEAD_DIM_SWA), BF16)
    sc = [jnp.where(ok_cur, lax.dot_general(q_ref[:, hs[h]], kc_ref[:, hs[h]], nt, preferred_element_type=F32)
                    * scale - slope[h] * dist_cur, NEG_INF) for h in heads]
    sp = [jnp.where(ok_prev, lax.dot_general(q_ref[:, hs[h]], kp_ref[:, hs[h]], nt, preferred_element_type=F32)
                    * scale - slope[h] * dist_prev, NEG_INF) for h in heads]
    m = [jnp.max(jnp.maximum(sc[h], sp[h]), axis=-1, keepdims=True) for h in heads]
    pc = [jnp.exp(sc[h] - m[h]).astype(BF16) for h in heads]
    pp = [jnp.exp(sp[h] - m[h]).astype(BF16) for h in heads]
    acc = [jnp.dot(pc[h], jnp.concatenate([vc_ref[:, hs[h]], ones], axis=1), preferred_element_type=F32)
           + jnp.dot(pp[h], jnp.concatenate([vp_ref[:, hs[h]], ones], axis=1), preferred_element_type=F32)
           for h in heads]
    for h in heads:
        den = acc[h][:, HEAD_DIM_SWA:]
        o_ref[:, hs[h]] = acc[h][:, :HEAD_DIM_SWA] / den
        lse_ref[:, hs[h]] = m[h] + jnp.log(den)


def _dilated(qkv, pos_lanes, pos_rows, dil):
    s = qkv.shape[0]
    sd = s // dil
    nb = sd // SWA_BLOCK
    blk = SWA_BLOCK
    qkv_v = qkv.reshape(sd, dil * 3 * D_SWA)
    pq_v = pos_lanes.reshape(sd, dil * LANES)
    prev = lambda n: jnp.maximum(n - 1, 0)
    wide = (blk, D_SWA)
    o, lse = pl.pallas_call(
        _dilated_body,
        grid=(dil, nb),
        in_specs=[pl.BlockSpec(wide, lambda r, n: (n, 3 * r)),
                  pl.BlockSpec(wide, lambda r, n: (n, 3 * r + 1)),
                  pl.BlockSpec(wide, lambda r, n: (prev(n), 3 * r + 1)),
                  pl.BlockSpec(wide, lambda r, n: (n, 3 * r + 2)),
                  pl.BlockSpec(wide, lambda r, n: (prev(n), 3 * r + 2)),
                  pl.BlockSpec((blk, LANES), lambda r, n: (n, r)),
                  pl.BlockSpec((1, 1, blk), lambda r, n: (r, 0, n)),
                  pl.BlockSpec((1, 1, blk), lambda r, n: (r, 0, prev(n)))],
        out_specs=[pl.BlockSpec(wide, lambda r, n: (n, r)), pl.BlockSpec(wide, lambda r, n: (n, r))],
        out_shape=[jax.ShapeDtypeStruct((sd, dil * D_SWA), F32), jax.ShapeDtypeStruct((sd, dil * D_SWA), F32)],
        compiler_params=_params("parallel", "parallel"),
        name=f"dil{dil}",
    )(qkv_v, qkv_v, qkv_v, qkv_v, qkv_v, pq_v, pos_rows, pos_rows)
    return o.reshape(s, D_SWA), lse.reshape(s, D_SWA)


def _outproj_body(o1_ref, o2_ref, o3_ref, l1_ref, l2_ref, l3_ref, ob_ref, x_ref, gsw_ref, gml_ref, wo_ref,
                  ga_ref, nfg_ref, scf_ref, shf_ref, wrt_ref, x1_ref, h2p_ref, lgt_ref, *, tm):
    nsub = OUTPROJ_SUBTILES
    r = tm // nsub
    subs = range(nsub)
    rows = [slice(i * r, (i + 1) * r) for i in subs]

    def merged(rs):
        l1, l2, l3 = l1_ref[rs, :], l2_ref[rs, :], l3_ref[rs, :]
        m = jnp.maximum(jnp.maximum(l1, l2), l3)
        e1, e2, e3 = jnp.exp(l1 - m), jnp.exp(l2 - m), jnp.exp(l3 - m)
        return (e1 * o1_ref[rs, :] + e2 * o2_ref[rs, :] + e3 * o3_ref[rs, :]) / (e1 + e2 + e3)

    mix = [jnp.concatenate([_rms(merged(rs), gsw_ref[...]), _rms(ob_ref[rs, :], gml_ref[...])],
                           axis=-1).astype(BF16) for rs in rows]
    proj = [jnp.dot(mix[i], wo_ref[...], preferred_element_type=F32) for i in subs]
    x1 = [x_ref[rows[i], :] + ga_ref[...] * proj[i] for i in subs]
    h2 = [_rms(x1[i], nfg_ref[...]) * (1.0 + scf_ref[...]) + shf_ref[...] for i in subs]
    for i in subs:
        x1_ref[rows[i], :] = x1[i]
        lgt_ref[:, rows[i]] = lax.dot_general(wrt_ref[...], h2[i], (((1,), (1,)), ((), ())),
                                              precision=lax.Precision.HIGHEST, preferred_element_type=F32)
    for i in subs:
        for s in range(PACK_ROWS):
            b = 2 * LANES * s
            h2p_ref[pl.ds(i * r * PACK_ROWS + s, r, stride=PACK_ROWS), :] = _pack_bf16_pairs(
                h2[i][:, b:b + LANES], h2[i][:, b + LANES:b + 2 * LANES])


def _outproj(o_pats, lse_pats, o_b, x, g_sw, g_ml, w_o, gate_a, nfg, scale_f, shift_f, w_router_t):
    s, d = x.shape
    tm = min(TM_OUTPROJ, s)
    row = lambda i: (i, 0)
    half = pl.BlockSpec((tm, D_SWA), row)
    vec = lambda n: pl.BlockSpec((1, n), lambda i: (0, 0))
    return pl.pallas_call(
        functools.partial(_outproj_body, tm=tm),
        grid=(s // tm,),
        in_specs=[half] * 7 + [pl.BlockSpec((tm, d), row), vec(D_SWA), vec(D_MLA), _resident(w_o.shape),
                                vec(d), vec(d), vec(d), vec(d), _resident(w_router_t.shape)],
        out_specs=[pl.BlockSpec((tm, d), row), pl.BlockSpec((tm * PACK_ROWS, LANES), row),
                   pl.BlockSpec((N_EXPERTS, tm), lambda i: (0, i))],
        out_shape=[jax.ShapeDtypeStruct((s, d), F32), jax.ShapeDtypeStruct((s * PACK_ROWS, LANES), I32),
                   jax.ShapeDtypeStruct((N_EXPERTS, s), F32)],
        compiler_params=_params("parallel"),
        name="outproj",
    )(*o_pats, *lse_pats, o_b, x, g_sw, g_ml, w_o, gate_a, nfg, scale_f, shift_f, w_router_t)


def _first_index(hit_value, x, iota, size, axis):
    return jnp.min(jnp.where(x == hit_value, iota, size), axis=axis, keepdims=True)


def _route_body(lgt_ref, bias_ref, tri_ref, eidx_ref, wts_ref, rank_ref, cnt_ref, carry_ref, *, tn):
    @pl.when(pl.program_id(0) == 0)
    def _():
        carry_ref[...] = jnp.zeros(carry_ref.shape, F32)

    gsz = N_EXPERTS // N_GROUPS
    scores = jax.nn.sigmoid(lgt_ref[...])
    choice = scores + bias_ref[...]
    neg = jnp.float32(-jnp.inf)

    g3 = choice.reshape(N_GROUPS, gsz, tn)
    i3 = lax.broadcasted_iota(I32, g3.shape, 1)
    m1 = jnp.max(g3, axis=1, keepdims=True)
    f1 = _first_index(m1, g3, i3, gsz, 1)
    m2 = jnp.max(jnp.where(i3 == f1, neg, g3), axis=1, keepdims=True)
    gs = (m1 + m2).reshape(N_GROUPS, tn)

    ig = lax.broadcasted_iota(I32, gs.shape, 0)
    gsel = jnp.zeros(gs.shape, F32)
    for _ in range(TOPK_GROUPS):
        hit = ig == _first_index(jnp.max(gs, axis=0, keepdims=True), gs, ig, N_GROUPS, 0)
        gsel = jnp.where(hit, 1.0, gsel)
        gs = jnp.where(hit, neg, gs)
    emask = jnp.broadcast_to(gsel.reshape(N_GROUPS, 1, tn), (N_GROUPS, gsz, tn)).reshape(N_EXPERTS, tn)
    cand = jnp.where(emask > 0.0, choice, NEG_INF)

    ie = lax.broadcasted_iota(I32, cand.shape, 0)
    picks, wsel = [], []
    onehot = jnp.zeros(cand.shape, F32)
    for _ in range(TOP_K):
        f = _first_index(jnp.max(cand, axis=0, keepdims=True), cand, ie, N_EXPERTS, 0)
        hit = ie == f
        picks.append(f)
        wsel.append(jnp.sum(jnp.where(hit, scores, 0.0), axis=0, keepdims=True))
        onehot = jnp.where(hit, 1.0, onehot)
        cand = jnp.where(hit, neg, cand)

    rank = carry_ref[...] + jnp.dot(onehot.astype(BF16), tri_ref[...], preferred_element_type=F32)
    carry_ref[...] = carry_ref[...] + jnp.sum(onehot, axis=1, keepdims=True)
    cnt_ref[...] = carry_ref[...]

    w = jnp.concatenate(wsel, axis=0)
    wts_ref[...] = w / jnp.sum(w, axis=0, keepdims=True) * ROUTED_SCALE
    eidx_ref[...] = jnp.concatenate(picks, axis=0)
    rank_ref[...] = jnp.concatenate(
        [jnp.sum(jnp.where(ie == f, rank, 0.0), axis=0, keepdims=True) for f in picks], axis=0).astype(I32)


def _route(logits_t, router_bias):
    e, n = logits_t.shape
    tn = min(TN_ROUTE, n)
    tri = (lax.broadcasted_iota(I32, (tn, tn), 0) < lax.broadcasted_iota(I32, (tn, tn), 1)).astype(BF16)
    col = lambda i: (0, i)
    return pl.pallas_call(
        functools.partial(_route_body, tn=tn),
        grid=(n // tn,),
        in_specs=[pl.BlockSpec((e, tn), col), pl.BlockSpec((e, 1), lambda i: (0, 0)),
                  pl.BlockSpec((tn, tn), lambda i: (0, 0))],
        out_specs=[pl.BlockSpec((TOP_K, tn), col), pl.BlockSpec((TOP_K, tn), col), pl.BlockSpec((TOP_K, tn), col),
                   pl.BlockSpec((e, 1), lambda i: (0, 0))],
        out_shape=[jax.ShapeDtypeStruct((TOP_K, n), I32), jax.ShapeDtypeStruct((TOP_K, n), F32),
                   jax.ShapeDtypeStruct((TOP_K, n), I32), jax.ShapeDtypeStruct((e, 1), F32)],
        scratch_shapes=[pltpu.VMEM((e, 1), F32)],
        compiler_params=_params("arbitrary"),
        name="route",
    )(logits_t, router_bias.reshape(e, 1), tri)


def _dispatch_body(dest_ref, h_ref, init_ref, xs_ref, sem, *, tb):
    del init_ref

    def row_copy(t, k):
        src = h_ref.at[pl.ds(pl.multiple_of(t * PACK_ROWS, PACK_ROWS), PACK_ROWS), :]
        dst = xs_ref.at[pl.ds(pl.multiple_of(dest_ref[k, t] * PACK_ROWS, PACK_ROWS), PACK_ROWS), :]
        return pltpu.make_async_copy(src, dst, sem)

    def issue(t, carry):
        for k in range(TOP_K):
            row_copy(t, k).start()
        return carry

    def drain(t, carry):
        for k in range(TOP_K):
            row_copy(t, k).wait()
        return carry

    lax.fori_loop(0, tb, issue, 0)
    lax.fori_loop(0, tb, drain, 0)


def _dispatch(dest, h2p, n_slots):
    n = dest.shape[1]
    tb = min(TB_DISPATCH, n)
    init = jnp.zeros((n_slots * PACK_ROWS, LANES), I32)
    return pl.pallas_call(
        functools.partial(_dispatch_body, tb=tb),
        grid=(n // tb,),
        in_specs=[pl.BlockSpec((TOP_K, tb), lambda i: (0, i), memory_space=pltpu.SMEM),
                  pl.BlockSpec((tb * PACK_ROWS, LANES), lambda i: (i, 0)),
                  pl.BlockSpec(memory_space=pl.ANY)],
        out_specs=pl.BlockSpec(memory_space=pl.ANY),
        out_shape=jax.ShapeDtypeStruct(init.shape, I32),
        scratch_shapes=[pltpu.SemaphoreType.DMA(())],
        input_output_aliases={2: 0},
        compiler_params=_params("arbitrary"),
        name="dispatch",
    )(dest, h2p, init)


def _swiglu_packed(xp_ref, wgu_ref, wd_ref, rows, nsub=1):
    r = rows // nsub
    subs = range(nsub)
    x = [jnp.concatenate([_unpack_bf16_pairs(xp_ref[pl.ds(i * r * PACK_ROWS + s, r, stride=PACK_ROWS), :])
                          for s in range(PACK_ROWS)], axis=1) for i in subs]
    h = [jnp.dot(x[i], wgu_ref[...], preferred_element_type=F32) for i in subs]
    a = [(h[i][:, :D_EXPERT] * jax.nn.sigmoid(h[i][:, :D_EXPERT]) * h[i][:, D_EXPERT:]).astype(BF16) for i in subs]
    return [jnp.dot(a[i], wd_ref[...], preferred_element_type=F32) for i in subs]


def _experts_body(be_ref, nv_ref, new_ref, xs_ref, wg_ref, wu_ref, wd_ref, y_ref, wgu_s, wd_s, *, tr):
    del be_ref
    b = pl.program_id(0)

    @pl.when(new_ref[b] > 0)
    def _():
        wgu_s[:, :D_EXPERT] = wg_ref[0].astype(BF16)
        wgu_s[:, D_EXPERT:] = wu_ref[0].astype(BF16)
        wd_s[...] = wd_ref[0].astype(BF16)

    @pl.when(nv_ref[b] > 0)
    def _():
        nsub = EXPERT_SUBTILES
        r = tr // nsub
        ys = _swiglu_packed(xs_ref, wgu_s, wd_s, tr, nsub)
        for i, y in enumerate(ys):
            for s in range(PACK_ROWS):
                c = 2 * LANES * s
                y_ref[pl.ds(i * r * PACK_ROWS + s, r, stride=PACK_ROWS), :] = _pack_bf16_pairs(
                    y[:, c:c + LANES], y[:, c + LANES:c + 2 * LANES])

    @pl.when(nv_ref[b] == 0)
    def _():
        y_ref[...] = jnp.zeros(y_ref.shape, I32)


def _experts(blk_e, blk_nv, blk_new, xs, w_gate, w_up, w_down):
    tr = TR_EXPERT
    nb = blk_e.shape[0]
    d, f = w_gate.shape[1], w_gate.shape[2]
    grid_spec = pltpu.PrefetchScalarGridSpec(
        num_scalar_prefetch=3,
        grid=(nb,),
        in_specs=[pl.BlockSpec((tr * PACK_ROWS, LANES), lambda b, be, nv, nw: (b, 0)),
                  pl.BlockSpec((1, d, f), lambda b, be, nv, nw: (be[b], 0, 0)),
                  pl.BlockSpec((1, d, f), lambda b, be, nv, nw: (be[b], 0, 0)),
                  pl.BlockSpec((1, f, d), lambda b, be, nv, nw: (be[b], 0, 0))],
        out_specs=pl.BlockSpec((tr * PACK_ROWS, LANES), lambda b, be, nv, nw: (b, 0)),
        scratch_shapes=[pltpu.VMEM((d, 2 * f), BF16), pltpu.VMEM((f, d), BF16)],
    )
    return pl.pallas_call(
        functools.partial(_experts_body, tr=tr),
        grid_spec=grid_spec,
        out_shape=jax.ShapeDtypeStruct((nb * tr * PACK_ROWS, LANES), I32),
        compiler_params=_params("arbitrary"),
        name="experts",
    )(blk_e, blk_nv, blk_new, xs, w_gate, w_up, w_down)


def _combine_body(dest_ref, wts_ref, h2p_ref, x1_ref, gf_ref, fg_ref, wsgu_ref, wsd_ref, y_ref,
                  out_ref, ybuf, sem, *, tb):
    def row_copy(t, k):
        src = y_ref.at[pl.ds(pl.multiple_of(dest_ref[k, t] * PACK_ROWS, PACK_ROWS), PACK_ROWS), :]
        dst = ybuf.at[pl.ds(pl.multiple_of((k * tb + t) * PACK_ROWS, PACK_ROWS), PACK_ROWS), :]
        return pltpu.make_async_copy(src, dst, sem)

    def issue(t, carry):
        for k in range(TOP_K):
            row_copy(t, k).start()
        return carry

    def drain(t, carry):
        for k in range(TOP_K):
            row_copy(t, k).wait()
        return carry

    lax.fori_loop(0, tb, issue, 0)
    shared = _swiglu_packed(h2p_ref, wsgu_ref, wsd_ref, tb)[0]
    lax.fori_loop(0, tb, drain, 0)

    wts = wts_ref[...]
    wb = [jnp.broadcast_to(wts[:, k:k + 1], (tb, LANES)) for k in range(TOP_K)]
    cols = []
    for s in range(PACK_ROWS):
        c = 2 * LANES * s
        lo, hi = shared[:, c:c + LANES], shared[:, c + LANES:c + 2 * LANES]
        for k in range(TOP_K):
            w = ybuf[pl.ds(k * tb * PACK_ROWS + s, tb, stride=PACK_ROWS), :]
            lo = lo + wb[k] * lax.bitcast_convert_type(lax.shift_left(w, jnp.int32(16)), F32)
            hi = hi + wb[k] * lax.bitcast_convert_type(w & jnp.int32(-65536), F32)
        cols += [lo, hi]
    moe = jnp.concatenate(cols, axis=1)
    out_ref[...] = _rms(x1_ref[...] + gf_ref[...] * moe, fg_ref[...])


def _combine(dest, wts_t, h2p, x1, gate_f, final_g, w_sgu, w_sd, y):
    n, d = x1.shape
    tb = min(TB_COMBINE, n)
    row = lambda i: (i, 0)
    vec = pl.BlockSpec((1, d), lambda i: (0, 0))
    return pl.pallas_call(
        functools.partial(_combine_body, tb=tb),
        grid=(n // tb,),
        in_specs=[pl.BlockSpec((TOP_K, tb), lambda i: (0, i), memory_space=pltpu.SMEM),
                  pl.BlockSpec((tb, TOP_K), row),
                  pl.BlockSpec((tb * PACK_ROWS, LANES), row),
                  pl.BlockSpec((tb, d), row), vec, vec,
                  _resident(w_sgu.shape), _resident(w_sd.shape),
                  pl.BlockSpec(memory_space=pl.ANY)],
        out_specs=pl.BlockSpec((tb, d), row),
        out_shape=jax.ShapeDtypeStruct((n, d), F32),
        scratch_shapes=[pltpu.VMEM((TOP_K * tb * PACK_ROWS, LANES), I32), pltpu.SemaphoreType.DMA(())],
        compiler_params=_params("arbitrary"),
        name="combine",
    )(dest, wts_t, h2p, x1, gate_f, final_g, w_sgu, w_sd, y)


def _rope_tables(pos):
    half = QK_ROPE_DIM // 2
    inv_freq = ROPE_THETA ** (-jnp.arange(half, dtype=F32) / half)
    ang = pos.astype(F32)[:, None] * inv_freq
    cos, sin = jnp.cos(ang), jnp.sin(ang)
    z = jnp.zeros_like(cos)
    c = jnp.concatenate([cos, cos, z, z], axis=1)
    s1 = jnp.concatenate([z, sin, z, z], axis=1)
    s2 = jnp.concatenate([-sin, z, z, z], axis=1)
    return c, s1, s2


def _layer(x, c, pos, norm_attn_g, w_ada, b_ada, w_in, g_q, w_uq, g_kv, w_ukv, g_out_swa, g_out_mla, w_o,
           norm_ffn_g, w_router, router_bias, w_exp_gate, w_exp_up, w_exp_down, w_sh_gate, w_sh_up, w_sh_down,
           final_g):
    s, d = x.shape
    row = lambda a: a.reshape(1, -1)

    mod = _ada(c, w_ada, b_ada)
    shift_a, scale_a, gate_a, shift_f, scale_f, gate_f = [mod[:, i * d:(i + 1) * d] for i in range(N_ADA)]

    n_qkv = 3 * D_SWA
    w_qkv = w_in[:, :n_qkv].astype(BF16)
    w_rest = jnp.pad(w_in[:, n_qkv:], ((0, 0), (0, LANES - QK_ROPE_DIM))).astype(BF16)
    qkv, rest = _inproj(x, row(norm_attn_g), scale_a, shift_a, w_qkv, w_rest)

    dq = QK_NOPE_DIM + QK_ROPE_DIM
    w_uq_p = jnp.pad(w_uq.reshape(Q_LORA_RANK, N_HEADS_MLA, dq), ((0, 0), (0, 0), (0, MLA_QK_PAD - dq)))
    w_uq_p = w_uq_p.reshape(Q_LORA_RANK, N_HEADS_MLA * MLA_QK_PAD).astype(BF16)
    rc, rs1, rs2 = _rope_tables(pos)
    q_m, k_m, v_m = _mlaproj(rest, row(g_q), row(g_kv), w_uq_p, w_ukv.astype(BF16), rc, rs1, rs2)
    o_b = _mla(q_m, k_m, v_m)

    posf = pos.astype(F32)
    pos_lanes = jnp.broadcast_to(posf[:, None], (s, LANES))
    o_pats, lse_pats = [], []
    for window, dil in SWA_PATTERNS:
        assert window // dil == SWA_BLOCK and s % (dil * SWA_BLOCK) == 0
        pos_rows = posf.reshape(s // dil, dil).T.reshape(dil, 1, s // dil)
        o_p, lse_p = _dilated(qkv, pos_lanes, pos_rows, dil)
        o_pats.append(o_p)
        lse_pats.append(lse_p)

    x1, h2p, logits_t = _outproj(o_pats, lse_pats, o_b, x, row(g_out_swa), row(g_out_mla), w_o.astype(BF16),
                                 gate_a, row(norm_ffn_g), scale_f, shift_f, w_router.T)

    eidx, wts, rank, cnt = _route(logits_t, router_bias)
    tr = TR_EXPERT
    counts = cnt[:, 0].astype(I32)
    padded = (counts + tr - 1) // tr * tr
    e_ids = jnp.arange(N_EXPERTS, dtype=I32)
    pad_end = jnp.sum(jnp.where(e_ids[None, :] <= e_ids[:, None], padded[None, :], 0), axis=1)
    pad_start = pad_end - padded
    lookup = lambda table, idx: jnp.sum(jnp.where(idx[..., None] == e_ids, table, 0), axis=-1)
    dest = lookup(pad_start, eidx) + rank
    n_slots = s * TOP_K + N_EXPERTS * tr
    blk_start = jnp.arange(n_slots // tr, dtype=I32) * tr
    blk_e = jnp.minimum(jnp.sum((pad_end[None, :] <= blk_start[:, None]).astype(I32), axis=1), N_EXPERTS - 1)
    blk_nv = jnp.clip(lookup(counts, blk_e) - (blk_start - lookup(pad_start, blk_e)), 0, tr)
    blk_new = ((blk_nv > 0) & (blk_start == lookup(pad_start, blk_e))).astype(I32)

    xs = _dispatch(dest, h2p, n_slots)
    y = _experts(blk_e, blk_nv, blk_new, xs, w_exp_gate, w_exp_up, w_exp_down)
    w_sgu = jnp.concatenate([w_sh_gate, w_sh_up], axis=1).astype(BF16)
    return _combine(dest, wts.T, h2p, x1, gate_f, row(final_g), w_sgu, w_sh_down.astype(BF16), y)


def kernel(x, c, positions, norm_attn_g, w_ada, b_ada, w_in, g_q, w_uq, g_kv, w_ukv, g_out_swa, g_out_mla, w_o,
           norm_ffn_g, w_router, router_bias, w_exp_gate, w_exp_up, w_exp_down, w_sh_gate, w_sh_up, w_sh_down,
           final_norm_g):
    assert x.shape[0] == 1 and w_ada.shape[0] == 1
    out = _layer(x[0], c[0], positions[0], norm_attn_g[0], w_ada[0], b_ada[0], w_in[0], g_q[0], w_uq[0], g_kv[0],
                 w_ukv[0], g_out_swa[0], g_out_mla[0], w_o[0], norm_ffn_g[0], w_router[0], router_bias[0],
                 w_exp_gate[0], w_exp_up[0], w_exp_down[0], w_sh_gate[0], w_sh_up[0], w_sh_down[0], final_norm_g)
    return out[None]
```

```python
import functools

import jax
import jax.numpy as jnp
from jax import lax
from jax.experimental import pallas as pl
from jax.experimental.pallas import tpu as pltpu

F32 = jnp.float32
BF16 = jnp.bfloat16
I32 = jnp.int32

D_MODEL = 2048
N_HEADS_SWA = 8
HEAD_DIM_SWA = 128
SWA_PATTERNS = ((128, 1), (512, 4), (2048, 16))
SWA_BLOCK = 128
N_HEADS_MLA = 8
Q_LORA_RANK = 512
KV_LORA_RANK = 256
QK_NOPE_DIM = 128
QK_ROPE_DIM = 64
V_HEAD_DIM = 128
ROPE_THETA = 10000.0
D_SWA = N_HEADS_SWA * HEAD_DIM_SWA
D_MLA = N_HEADS_MLA * V_HEAD_DIM
N_EXPERTS = 64
N_GROUPS = 8
TOPK_GROUPS = 4
TOP_K = 8
D_EXPERT = 512
ROUTED_SCALE = 2.5
N_ADA = 6
EPS = 1e-6
NEG_INF = -1e30
LOG2E = 1.4426950408889634

LANES = 128
MLA_QK_PAD = 256
PACK_ROWS = D_MODEL // (2 * LANES)
VMEM_LIMIT = 56 * 1024 * 1024

TM_INPROJ = 256
TM_MLAPROJ = 512
T_MLA = 1024
TM_OUTPROJ = 256
OUTPROJ_SUBTILES = 2
TN_ROUTE = 512
TB_DISPATCH = 256
TR_EXPERT = 256
EXPERT_SUBTILES = 2
TB_COMBINE = 128


def _params(*sem):
    return pltpu.CompilerParams(dimension_semantics=sem, vmem_limit_bytes=VMEM_LIMIT)


def _rms(x, g):
    return x * lax.rsqrt(jnp.mean(x * x, axis=-1, keepdims=True) + EPS) * g


def _resident(shape):
    nd = len(shape)
    return pl.BlockSpec(shape, lambda *_: (0,) * nd, pipeline_mode=pl.Buffered(1))


def _pack_bf16_pairs(a, b):
    ua = lax.bitcast_convert_type(a.astype(BF16).astype(F32), I32)
    ub = lax.bitcast_convert_type(b.astype(BF16).astype(F32), I32)
    return lax.shift_right_logical(ua, jnp.int32(16)) | (ub & jnp.int32(-65536))


def _unpack_bf16_pairs(w):
    lo = lax.bitcast_convert_type(lax.shift_left(w, jnp.int32(16)), F32).astype(BF16)
    hi = lax.bitcast_convert_type(w & jnp.int32(-65536), F32).astype(BF16)
    return jnp.concatenate([lo, hi], axis=1)


def _packed_chunk(ref, s, rows):
    return _unpack_bf16_pairs(ref[pl.ds(s, rows, stride=PACK_ROWS), :])


def _ada_body(c_ref, w_ref, b_ref, o_ref):
    c = c_ref[...]
    a = c * jax.nn.sigmoid(c)
    o_ref[...] = jnp.sum(w_ref[...] * a, axis=0, keepdims=True) + b_ref[...]


def _ada(c, w_ada, b_ada):
    d, n = w_ada.shape
    tn = 512
    return pl.pallas_call(
        _ada_body,
        grid=(n // tn,),
        in_specs=[pl.BlockSpec((d, 1), lambda j: (0, 0)),
                  pl.BlockSpec((d, tn), lambda j: (0, j)),
                  pl.BlockSpec((1, tn), lambda j: (0, j))],
        out_specs=pl.BlockSpec((1, tn), lambda j: (0, j)),
        out_shape=jax.ShapeDtypeStruct((1, n), F32),
        compiler_params=_params("parallel"),
        name="ada",
    )(c.reshape(d, 1), w_ada, b_ada.reshape(1, n))


def _inproj_body(x_ref, g_ref, sc_ref, sh_ref, wqkv_ref, wr_ref, rest_ref, *out_and_scratch, tm, dils):
    view_refs, res_ref = out_and_scratch[:-1], out_and_scratch[-1]
    n = wqkv_ref.shape[1]
    h = (_rms(x_ref[...], g_ref[...]) * (1.0 + sc_ref[...]) + sh_ref[...]).astype(BF16)
    rest_ref[...] = jnp.dot(h, wr_ref[...], preferred_element_type=F32)
    res = jnp.dot(h, wqkv_ref[...], preferred_element_type=F32)
    chunks = range(n // LANES)
    for c in chunks:
        res_ref[c] = res[:, c * LANES:(c + 1) * LANES]
    for dil, v_ref in zip(dils, view_refs):
        if dil == 1:
            v_ref[...] = res.astype(BF16)
            continue
        for r in range(dil):
            for c in chunks:
                b = r * n + c * LANES
                v_ref[:, b:b + LANES] = res_ref[c, pl.ds(r, tm // dil, stride=dil), :].astype(BF16)


def _inproj(x, g, scale, shift, w_qkv, w_rest, dils):
    s, d = x.shape
    tm = min(TM_INPROJ, s)
    n1, n2 = w_qkv.shape[1], w_rest.shape[1]
    row = lambda i: (i, 0)
    vec = pl.BlockSpec((1, d), lambda i: (0, 0))
    outs = pl.pallas_call(
        functools.partial(_inproj_body, tm=tm, dils=dils),
        grid=(s // tm,),
        in_specs=[pl.BlockSpec((tm, d), row), vec, vec, vec, _resident((d, n1)), _resident((d, n2))],
        out_specs=[pl.BlockSpec((tm, n2), row)] + [pl.BlockSpec((tm // dil, dil * n1), row) for dil in dils],
        out_shape=[jax.ShapeDtypeStruct((s, n2), F32)]
        + [jax.ShapeDtypeStruct((s // dil, dil * n1), BF16) for dil in dils],
        scratch_shapes=[pltpu.VMEM((n1 // LANES, tm, LANES), F32)],
        compiler_params=_params("parallel"),
        name="inproj",
    )(x, g, scale, shift, w_qkv, w_rest)
    return outs[0], outs[1:]


def _rope_tail(t, c, s1, s2):
    return t * c + pltpu.roll(t, 32, 1) * s1 + pltpu.roll(t, 96, 1) * s2


def _mlaproj_body(rest_ref, gq_ref, gkv_ref, wuq_ref, wukv_ref, c_ref, s1_ref, s2_ref,
                  q_ref, k_ref, v_ref, *, scale):
    rest = rest_ref[...]
    c, s1, s2 = c_ref[...], s1_ref[...], s2_ref[...]
    cq = _rms(rest[:, :Q_LORA_RANK], gq_ref[...]).astype(BF16)
    ckv = _rms(rest[:, Q_LORA_RANK:Q_LORA_RANK + KV_LORA_RANK], gkv_ref[...]).astype(BF16)
    ktail = _rope_tail(rest[:, Q_LORA_RANK + KV_LORA_RANK:], c, s1, s2).astype(BF16)
    q = jnp.dot(cq, wuq_ref[...], preferred_element_type=F32)
    kv = jnp.dot(ckv, wukv_ref[...], preferred_element_type=F32)
    for h in range(N_HEADS_MLA):
        b = h * MLA_QK_PAD
        q_ref[:, b:b + LANES] = (q[:, b:b + LANES] * scale).astype(BF16)
        q_ref[:, b + LANES:b + 2 * LANES] = (_rope_tail(q[:, b + LANES:b + 2 * LANES], c, s1, s2) * scale).astype(BF16)
        k_ref[:, b:b + LANES] = kv[:, b:b + LANES].astype(BF16)
        k_ref[:, b + LANES:b + 2 * LANES] = ktail
        v_ref[:, b:b + LANES] = kv[:, b + LANES:b + 2 * LANES].astype(BF16)
        v_ref[:, b + LANES:b + 2 * LANES] = jnp.ones((q.shape[0], LANES), BF16)


def _mlaproj(rest, g_q, g_kv, w_uq, w_ukv, rc, rs1, rs2):
    s, nr = rest.shape
    tm = min(TM_MLAPROJ, s)
    nq = N_HEADS_MLA * MLA_QK_PAD
    row = lambda i: (i, 0)
    tab = pl.BlockSpec((tm, LANES), row)
    scale = float(QK_NOPE_DIM + QK_ROPE_DIM) ** -0.5 * LOG2E
    return pl.pallas_call(
        functools.partial(_mlaproj_body, scale=scale),
        grid=(s // tm,),
        in_specs=[pl.BlockSpec((tm, nr), row),
                  pl.BlockSpec((1, Q_LORA_RANK), lambda i: (0, 0)),
                  pl.BlockSpec((1, KV_LORA_RANK), lambda i: (0, 0)),
                  _resident(w_uq.shape), _resident(w_ukv.shape), tab, tab, tab],
        out_specs=[pl.BlockSpec((tm, nq), row)] * 3,
        out_shape=[jax.ShapeDtypeStruct((s, nq), BF16)] * 3,
        compiler_params=_params("parallel"),
        name="mlaproj",
    )(rest, g_q, g_kv, w_uq, w_ukv, rc, rs1, rs2)


def _mla_body(q_ref, k_ref, v_ref, o_ref, m_ref, acc_ref, sa_ref, sb_ref, *, t):
    qi = pl.program_id(1)
    q = q_ref[...]
    m_ref[...] = jnp.full(m_ref.shape, NEG_INF, F32)
    acc_ref[...] = jnp.zeros(acc_ref.shape, F32)

    def scores(j, dst):
        k = k_ref[pl.ds(pl.multiple_of(j * t, t), t), :]
        dst[...] = lax.dot_general(q, k, (((1,), (1,)), ((), ())), preferred_element_type=F32)

    def absorb(j, src, masked):
        s = src[...]
        if masked:
            r = lax.broadcasted_iota(I32, (t, t), 0)
            cidx = lax.broadcasted_iota(I32, (t, t), 1)
            s = jnp.where(cidx <= r, s, NEG_INF)
        v = v_ref[pl.ds(pl.multiple_of(j * t, t), t), :]
        m_old = m_ref[...]
        m_new = jnp.maximum(m_old, jnp.max(s, axis=-1, keepdims=True))
        p = jnp.exp2(s - m_new).astype(BF16)
        acc_ref[...] = jnp.exp2(m_old - m_new) * acc_ref[...] + jnp.dot(p, v, preferred_element_type=F32)
        m_ref[...] = m_new

    scores(0, sa_ref)

    def pair(i, carry):
        j = 2 * i
        scores(j + 1, sb_ref)
        absorb(j, sa_ref, False)
        scores(j + 2, sa_ref)
        absorb(j + 1, sb_ref, False)
        return carry

    lax.fori_loop(0, qi // 2, pair, 0)

    @pl.when(qi % 2 == 0)
    def _():
        absorb(qi, sa_ref, True)

    @pl.when(qi % 2 == 1)
    def _():
        scores(qi, sb_ref)
        absorb(qi - 1, sa_ref, False)
        absorb(qi, sb_ref, True)

    acc = acc_ref[...]
    o_ref[...] = acc[:, :V_HEAD_DIM] / acc[:, V_HEAD_DIM:]


def _mla(q, k, v):
    s = q.shape[0]
    t = min(T_MLA, s)
    head_cols = lambda h, i: (0, h)
    return pl.pallas_call(
        functools.partial(_mla_body, t=t),
        grid=(N_HEADS_MLA, s // t),
        in_specs=[pl.BlockSpec((t, MLA_QK_PAD), lambda h, i: (i, h)),
                  pl.BlockSpec((s, MLA_QK_PAD), head_cols, pipeline_mode=pl.Buffered(1)),
                  pl.BlockSpec((s, MLA_QK_PAD), head_cols, pipeline_mode=pl.Buffered(1))],
        out_specs=pl.BlockSpec((t, V_HEAD_DIM), lambda h, i: (i, h)),
        out_shape=jax.ShapeDtypeStruct((s, D_MLA), F32),
        scratch_shapes=[pltpu.VMEM((t, 1), F32), pltpu.VMEM((t, MLA_QK_PAD), F32),
                        pltpu.VMEM((t, t), F32), pltpu.VMEM((t, t), F32)],
        compiler_params=_params("parallel", "arbitrary"),
        name="mla",
    )(q, k, v)


def _dilated_body(q_ref, kc_ref, kp_ref, vc_ref, vp_ref, pq_ref, pkc_ref, pkp_ref, o_ref, lse_ref):
    n = pl.program_id(1)
    blk = SWA_BLOCK
    i = lax.broadcasted_iota(I32, (blk, blk), 0)
    j = lax.broadcasted_iota(I32, (blk, blk), 1)
    ok_cur = j <= i
    ok_prev = (j >= i) & (n > 0)
    pq = pq_ref[...]
    dist_cur = jnp.abs(pq - pkc_ref[0])
    dist_prev = jnp.abs(pq - pkp_ref[0])
    scale = float(HEAD_DIM_SWA) ** -0.5
    nt = (((1,), (1,)), ((), ()))
    heads = range(N_HEADS_SWA)
    hs = [slice(h * HEAD_DIM_SWA, (h + 1) * HEAD_DIM_SWA) for h in heads]
    slope = [2.0 ** (-8.0 * (h + 1) / N_HEADS_SWA) for h in heads]
    ones = jnp.ones((blk, HEAD_DIM_SWA), BF16)
    sc = [jnp.where(ok_cur, lax.dot_general(q_ref[:, hs[h]], kc_ref[:, hs[h]], nt, preferred_element_type=F32)
                    * scale - slope[h] * dist_cur, NEG_INF) for h in heads]
    sp = [jnp.where(ok_prev, lax.dot_general(q_ref[:, hs[h]], kp_ref[:, hs[h]], nt, preferred_element_type=F32)
                    * scale - slope[h] * dist_prev, NEG_INF) for h in heads]
    m = [jnp.max(jnp.maximum(sc[h], sp[h]), axis=-1, keepdims=True) for h in heads]
    pc = [jnp.exp(sc[h] - m[h]).astype(BF16) for h in heads]
    pp = [jnp.exp(sp[h] - m[h]).astype(BF16) for h in heads]
    acc = [jnp.dot(pc[h], jnp.concatenate([vc_ref[:, hs[h]], ones], axis=1), preferred_element_type=F32)
           + jnp.dot(pp[h], jnp.concatenate([vp_ref[:, hs[h]], ones], axis=1), preferred_element_type=F32)
           for h in heads]
    for h in heads:
        den = acc[h][:, HEAD_DIM_SWA:]
        o_ref[:, hs[h]] = acc[h][:, :HEAD_DIM_SWA] / den
        lse_ref[:, hs[h]] = m[h] + jnp.log(den)


def _dilated(qkv_v, posf, dil):
    sd = qkv_v.shape[0]
    nb = sd // SWA_BLOCK
    blk = SWA_BLOCK
    pos_v = posf.reshape(sd, dil)
    pq_v = jnp.repeat(pos_v, LANES, axis=1)
    pos_rows = pos_v.T.reshape(dil, 1, sd)
    prev = lambda n: jnp.maximum(n - 1, 0)
    wide = (blk, D_SWA)
    return pl.pallas_call(
        _dilated_body,
        grid=(dil, nb),
        in_specs=[pl.BlockSpec(wide, lambda r, n: (n, 3 * r)),
                  pl.BlockSpec(wide, lambda r, n: (n, 3 * r + 1)),
                  pl.BlockSpec(wide, lambda r, n: (prev(n), 3 * r + 1)),
                  pl.BlockSpec(wide, lambda r, n: (n, 3 * r + 2)),
                  pl.BlockSpec(wide, lambda r, n: (prev(n), 3 * r + 2)),
                  pl.BlockSpec((blk, LANES), lambda r, n: (n, r)),
                  pl.BlockSpec((1, 1, blk), lambda r, n: (r, 0, n)),
                  pl.BlockSpec((1, 1, blk), lambda r, n: (r, 0, prev(n)))],
        out_specs=[pl.BlockSpec(wide, lambda r, n: (n, r)), pl.BlockSpec(wide, lambda r, n: (n, r))],
        out_shape=[jax.ShapeDtypeStruct((sd, dil * D_SWA), F32), jax.ShapeDtypeStruct((sd, dil * D_SWA), F32)],
        compiler_params=_params("parallel", "parallel"),
        name=f"dil{dil}",
    )(qkv_v, qkv_v, qkv_v, qkv_v, qkv_v, pq_v, pos_rows, pos_rows)


def _outproj_body(*refs, tm, dils):
    npat = len(dils)
    o_views, l_views = refs[:npat], refs[npat:2 * npat]
    (ob_ref, x_ref, gsw_ref, gml_ref, wo_ref, ga_ref, nfg_ref, scf_ref, shf_ref, wrt_ref,
     x1_ref, h2p_ref, lgt_ref) = refs[2 * npat:2 * npat + 13]
    scratch = list(refs[2 * npat + 13:])

    chunks = range(D_SWA // LANES)

    def token_order(view_ref, dil):
        if dil == 1:
            return lambda rs: view_ref[rs, :]
        nat_ref = scratch.pop(0)
        for r in range(dil):
            for c in chunks:
                b = r * D_SWA + c * LANES
                nat_ref[c, pl.ds(r, tm // dil, stride=dil), :] = view_ref[:, b:b + LANES]
        return lambda rs: jnp.concatenate([nat_ref[c, rs, :] for c in chunks], axis=1)

    o1, o2, o3 = [token_order(v, dil) for v, dil in zip(o_views, dils)]
    l1f, l2f, l3f = [token_order(v, dil) for v, dil in zip(l_views, dils)]

    nsub = OUTPROJ_SUBTILES
    r = tm // nsub
    subs = range(nsub)
    rows = [slice(i * r, (i + 1) * r) for i in subs]

    def merged(rs):
        l1, l2, l3 = l1f(rs), l2f(rs), l3f(rs)
        m = jnp.maximum(jnp.maximum(l1, l2), l3)
        e1, e2, e3 = jnp.exp(l1 - m), jnp.exp(l2 - m), jnp.exp(l3 - m)
        return (e1 * o1(rs) + e2 * o2(rs) + e3 * o3(rs)) / (e1 + e2 + e3)

    mix = [jnp.concatenate([_rms(merged(rs), gsw_ref[...]), _rms(ob_ref[rs, :], gml_ref[...])],
                           axis=-1).astype(BF16) for rs in rows]
    proj = [jnp.dot(mix[i], wo_ref[...], preferred_element_type=F32) for i in subs]
    x1 = [x_ref[rows[i], :] + ga_ref[...] * proj[i] for i in subs]
    h2 = [_rms(x1[i], nfg_ref[...]) * (1.0 + scf_ref[...]) + shf_ref[...] for i in subs]
    for i in subs:
        x1_ref[rows[i], :] = x1[i]
        lgt_ref[:, rows[i]] = lax.dot_general(wrt_ref[...], h2[i], (((1,), (1,)), ((), ())),
                                              precision=lax.Precision.HIGHEST, preferred_element_type=F32)
    for i in subs:
        for s in range(PACK_ROWS):
            b = 2 * LANES * s
            h2p_ref[pl.ds(i * r * PACK_ROWS + s, r, stride=PACK_ROWS), :] = _pack_bf16_pairs(
                h2[i][:, b:b + LANES], h2[i][:, b + LANES:b + 2 * LANES])


def _outproj(o_pats, lse_pats, dils, o_b, x, g_sw, g_ml, w_o, gate_a, nfg, scale_f, shift_f, w_router_t):
    s, d = x.shape
    tm = min(TM_OUTPROJ, s)
    row = lambda i: (i, 0)
    views = [pl.BlockSpec((tm // dil, dil * D_SWA), row) for dil in dils]
    vec = lambda n: pl.BlockSpec((1, n), lambda i: (0, 0))
    n_reordered = 2 * sum(1 for dil in dils if dil > 1)
    return pl.pallas_call(
        functools.partial(_outproj_body, tm=tm, dils=dils),
        grid=(s // tm,),
        in_specs=views + views + [pl.BlockSpec((tm, D_MLA), row), pl.BlockSpec((tm, d), row), vec(D_SWA),
                                  vec(D_MLA), _resident(w_o.shape), vec(d), vec(d), vec(d), vec(d),
                                  _resident(w_router_t.shape)],
        out_specs=[pl.BlockSpec((tm, d), row), pl.BlockSpec((tm * PACK_ROWS, LANES), row),
                   pl.BlockSpec((N_EXPERTS, tm), lambda i: (0, i))],
        out_shape=[jax.ShapeDtypeStruct((s, d), F32), jax.ShapeDtypeStruct((s * PACK_ROWS, LANES), I32),
                   jax.ShapeDtypeStruct((N_EXPERTS, s), F32)],
        scratch_shapes=[pltpu.VMEM((D_SWA // LANES, tm, LANES), F32)] * n_reordered,
        compiler_params=_params("parallel"),
        name="outproj",
    )(*o_pats, *lse_pats, o_b, x, g_sw, g_ml, w_o, gate_a, nfg, scale_f, shift_f, w_router_t)


def _first_index(hit_value, x, iota, size, axis):
    return jnp.min(jnp.where(x == hit_value, iota, size), axis=axis, keepdims=True)


def _route_body(lgt_ref, bias_ref, tri_ref, eidx_ref, wts_ref, rank_ref, cnt_ref, carry_ref, *, tn):
    @pl.when(pl.program_id(0) == 0)
    def _():
        carry_ref[...] = jnp.zeros(carry_ref.shape, F32)

    gsz = N_EXPERTS // N_GROUPS
    scores = jax.nn.sigmoid(lgt_ref[...])
    choice = scores + bias_ref[...]
    neg = jnp.float32(-jnp.inf)

    g3 = choice.reshape(N_GROUPS, gsz, tn)
    i3 = lax.broadcasted_iota(I32, g3.shape, 1)
    m1 = jnp.max(g3, axis=1, keepdims=True)
    f1 = _first_index(m1, g3, i3, gsz, 1)
    m2 = jnp.max(jnp.where(i3 == f1, neg, g3), axis=1, keepdims=True)
    gs = (m1 + m2).reshape(N_GROUPS, tn)

    ig = lax.broadcasted_iota(I32, gs.shape, 0)
    gsel = jnp.zeros(gs.shape, F32)
    for _ in range(TOPK_GROUPS):
        hit = ig == _first_index(jnp.max(gs, axis=0, keepdims=True), gs, ig, N_GROUPS, 0)
        gsel = jnp.where(hit, 1.0, gsel)
        gs = jnp.where(hit, neg, gs)
    emask = jnp.broadcast_to(gsel.reshape(N_GROUPS, 1, tn), (N_GROUPS, gsz, tn)).reshape(N_EXPERTS, tn)
    cand = jnp.where(emask > 0.0, choice, NEG_INF)

    ie = lax.broadcasted_iota(I32, cand.shape, 0)
    picks, wsel = [], []
    onehot = jnp.zeros(cand.shape, F32)
    for _ in range(TOP_K):
        f = _first_index(jnp.max(cand, axis=0, keepdims=True), cand, ie, N_EXPERTS, 0)
        hit = ie == f
        picks.append(f)
        wsel.append(jnp.sum(jnp.where(hit, scores, 0.0), axis=0, keepdims=True))
        onehot = jnp.where(hit, 1.0, onehot)
        cand = jnp.where(hit, neg, cand)

    rank = carry_ref[...] + jnp.dot(onehot.astype(BF16), tri_ref[...], preferred_element_type=F32)
    carry_ref[...] = carry_ref[...] + jnp.sum(onehot, axis=1, keepdims=True)
    cnt_ref[...] = carry_ref[...]

    w = jnp.concatenate(wsel, axis=0)
    wts_ref[...] = w / jnp.sum(w, axis=0, keepdims=True) * ROUTED_SCALE
    eidx_ref[...] = jnp.concatenate(picks, axis=0)
    rank_ref[...] = jnp.concatenate(
        [jnp.sum(jnp.where(ie == f, rank, 0.0), axis=0, keepdims=True) for f in picks], axis=0).astype(I32)


def _route(logits_t, router_bias):
    e, n = logits_t.shape
    tn = min(TN_ROUTE, n)
    tri = (lax.broadcasted_iota(I32, (tn, tn), 0) < lax.broadcasted_iota(I32, (tn, tn), 1)).astype(BF16)
    col = lambda i: (0, i)
    return pl.pallas_call(
        functools.partial(_route_body, tn=tn),
        grid=(n // tn,),
        in_specs=[pl.BlockSpec((e, tn), col), pl.BlockSpec((e, 1), lambda i: (0, 0)),
                  pl.BlockSpec((tn, tn), lambda i: (0, 0))],
        out_specs=[pl.BlockSpec((TOP_K, tn), col), pl.BlockSpec((TOP_K, tn), col), pl.BlockSpec((TOP_K, tn), col),
                   pl.BlockSpec((e, 1), lambda i: (0, 0))],
        out_shape=[jax.ShapeDtypeStruct((TOP_K, n), I32), jax.ShapeDtypeStruct((TOP_K, n), F32),
                   jax.ShapeDtypeStruct((TOP_K, n), I32), jax.ShapeDtypeStruct((e, 1), F32)],
        scratch_shapes=[pltpu.VMEM((e, 1), F32)],
        compiler_params=_params("arbitrary"),
        name="route",
    )(logits_t, router_bias.reshape(e, 1), tri)


def _dispatch_body(dest_ref, h_ref, init_ref, xs_ref, sem, *, tb):
    del init_ref

    def row_copy(t, k):
        src = h_ref.at[pl.ds(pl.multiple_of(t * PACK_ROWS, PACK_ROWS), PACK_ROWS), :]
        dst = xs_ref.at[pl.ds(pl.multiple_of(dest_ref[k, t] * PACK_ROWS, PACK_ROWS), PACK_ROWS), :]
        return pltpu.make_async_copy(src, dst, sem)

    def issue(t, carry):
        for k in range(TOP_K):
            row_copy(t, k).start()
        return carry

    def drain(t, carry):
        for k in range(TOP_K):
            row_copy(t, k).wait()
        return carry

    lax.fori_loop(0, tb, issue, 0)
    lax.fori_loop(0, tb, drain, 0)


def _dispatch(dest, h2p, n_slots):
    n = dest.shape[1]
    tb = min(TB_DISPATCH, n)
    init = jnp.zeros((n_slots * PACK_ROWS, LANES), I32)
    return pl.pallas_call(
        functools.partial(_dispatch_body, tb=tb),
        grid=(n // tb,),
        in_specs=[pl.BlockSpec((TOP_K, tb), lambda i: (0, i), memory_space=pltpu.SMEM),
                  pl.BlockSpec((tb * PACK_ROWS, LANES), lambda i: (i, 0)),
                  pl.BlockSpec(memory_space=pl.ANY)],
        out_specs=pl.BlockSpec(memory_space=pl.ANY),
        out_shape=jax.ShapeDtypeStruct(init.shape, I32),
        scratch_shapes=[pltpu.SemaphoreType.DMA(())],
        input_output_aliases={2: 0},
        compiler_params=_params("arbitrary"),
        name="dispatch",
    )(dest, h2p, init)


def _swiglu_packed(xp_ref, wgu_ref, wd_ref, rows, nsub=1):
    r = rows // nsub
    subs = range(nsub)
    x = [jnp.concatenate([_unpack_bf16_pairs(xp_ref[pl.ds(i * r * PACK_ROWS + s, r, stride=PACK_ROWS), :])
                          for s in range(PACK_ROWS)], axis=1) for i in subs]
    h = [jnp.dot(x[i], wgu_ref[...], preferred_element_type=F32) for i in subs]
    a = [(h[i][:, :D_EXPERT] * jax.nn.sigmoid(h[i][:, :D_EXPERT]) * h[i][:, D_EXPERT:]).astype(BF16) for i in subs]
    return [jnp.dot(a[i], wd_ref[...], preferred_element_type=F32) for i in subs]


def _experts_body(be_ref, nv_ref, new_ref, xs_ref, wg_ref, wu_ref, wd_ref, y_ref, wgu_s, wd_s, *, tr):
    del be_ref
    b = pl.program_id(0)

    @pl.when(new_ref[b] > 0)
    def _():
        wgu_s[:, :D_EXPERT] = wg_ref[0].astype(BF16)
        wgu_s[:, D_EXPERT:] = wu_ref[0].astype(BF16)
        wd_s[...] = wd_ref[0].astype(BF16)

    @pl.when(nv_ref[b] > 0)
    def _():
        nsub = EXPERT_SUBTILES
        r = tr // nsub
        ys = _swiglu_packed(xs_ref, wgu_s, wd_s, tr, nsub)
        for i, y in enumerate(ys):
            for s in range(PACK_ROWS):
                c = 2 * LANES * s
                y_ref[pl.ds(i * r * PACK_ROWS + s, r, stride=PACK_ROWS), :] = _pack_bf16_pairs(
                    y[:, c:c + LANES], y[:, c + LANES:c + 2 * LANES])

    @pl.when(nv_ref[b] == 0)
    def _():
        y_ref[...] = jnp.zeros(y_ref.shape, I32)


def _experts(blk_e, blk_nv, blk_new, xs, w_gate, w_up, w_down):
    tr = TR_EXPERT
    nb = blk_e.shape[0]
    d, f = w_gate.shape[1], w_gate.shape[2]
    grid_spec = pltpu.PrefetchScalarGridSpec(
        num_scalar_prefetch=3,
        grid=(nb,),
        in_specs=[pl.BlockSpec((tr * PACK_ROWS, LANES), lambda b, be, nv, nw: (b, 0)),
                  pl.BlockSpec((1, d, f), lambda b, be, nv, nw: (be[b], 0, 0)),
                  pl.BlockSpec((1, d, f), lambda b, be, nv, nw: (be[b], 0, 0)),
                  pl.BlockSpec((1, f, d), lambda b, be, nv, nw: (be[b], 0, 0))],
        out_specs=pl.BlockSpec((tr * PACK_ROWS, LANES), lambda b, be, nv, nw: (b, 0)),
        scratch_shapes=[pltpu.VMEM((d, 2 * f), BF16), pltpu.VMEM((f, d), BF16)],
    )
    return pl.pallas_call(
        functools.partial(_experts_body, tr=tr),
        grid_spec=grid_spec,
        out_shape=jax.ShapeDtypeStruct((nb * tr * PACK_ROWS, LANES), I32),
        compiler_params=_params("arbitrary"),
        name="experts",
    )(blk_e, blk_nv, blk_new, xs, w_gate, w_up, w_down)


def _combine_body(dest_ref, wts_ref, h2p_ref, x1_ref, gf_ref, fg_ref, wsgu_ref, wsd_ref, y_ref,
                  out_ref, ybuf, sem, *, tb):
    def row_copy(t, k):
        src = y_ref.at[pl.ds(pl.multiple_of(dest_ref[k, t] * PACK_ROWS, PACK_ROWS), PACK_ROWS), :]
        dst = ybuf.at[pl.ds(pl.multiple_of((k * tb + t) * PACK_ROWS, PACK_ROWS), PACK_ROWS), :]
        return pltpu.make_async_copy(src, dst, sem)

    def issue(t, carry):
        for k in range(TOP_K):
            row_copy(t, k).start()
        return carry

    def drain(t, carry):
        for k in range(TOP_K):
            row_copy(t, k).wait()
        return carry

    lax.fori_loop(0, tb, issue, 0)
    shared = _swiglu_packed(h2p_ref, wsgu_ref, wsd_ref, tb)[0]
    lax.fori_loop(0, tb, drain, 0)

    wts = wts_ref[...]
    wb = [jnp.broadcast_to(wts[:, k:k + 1], (tb, LANES)) for k in range(TOP_K)]
    cols = []
    for s in range(PACK_ROWS):
        c = 2 * LANES * s
        lo, hi = shared[:, c:c + LANES], shared[:, c + LANES:c + 2 * LANES]
        for k in range(TOP_K):
            w = ybuf[pl.ds(k * tb * PACK_ROWS + s, tb, stride=PACK_ROWS), :]
            lo = lo + wb[k] * lax.bitcast_convert_type(lax.shift_left(w, jnp.int32(16)), F32)
            hi = hi + wb[k] * lax.bitcast_convert_type(w & jnp.int32(-65536), F32)
        cols += [lo, hi]
    moe = jnp.concatenate(cols, axis=1)
    out_ref[...] = _rms(x1_ref[...] + gf_ref[...] * moe, fg_ref[...])


def _combine(dest, wts_t, h2p, x1, gate_f, final_g, w_sgu, w_sd, y):
    n, d = x1.shape
    tb = min(TB_COMBINE, n)
    row = lambda i: (i, 0)
    vec = pl.BlockSpec((1, d), lambda i: (0, 0))
    return pl.pallas_call(
        functools.partial(_combine_body, tb=tb),
        grid=(n // tb,),
        in_specs=[pl.BlockSpec((TOP_K, tb), lambda i: (0, i), memory_space=pltpu.SMEM),
                  pl.BlockSpec((tb, TOP_K), row),
                  pl.BlockSpec((tb * PACK_ROWS, LANES), row),
                  pl.BlockSpec((tb, d), row), vec, vec,
                  _resident(w_sgu.shape), _resident(w_sd.shape),
                  pl.BlockSpec(memory_space=pl.ANY)],
        out_specs=pl.BlockSpec((tb, d), row),
        out_shape=jax.ShapeDtypeStruct((n, d), F32),
        scratch_shapes=[pltpu.VMEM((TOP_K * tb * PACK_ROWS, LANES), I32), pltpu.SemaphoreType.DMA(())],
        compiler_params=_params("arbitrary"),
        name="combine",
    )(dest, wts_t, h2p, x1, gate_f, final_g, w_sgu, w_sd, y)


def _rope_tables(pos):
    half = QK_ROPE_DIM // 2
    inv_freq = ROPE_THETA ** (-jnp.arange(half, dtype=F32) / half)
    ang = pos.astype(F32)[:, None] * inv_freq
    cos, sin = jnp.cos(ang), jnp.sin(ang)
    z = jnp.zeros_like(cos)
    c = jnp.concatenate([cos, cos, z, z], axis=1)
    s1 = jnp.concatenate([z, sin, z, z], axis=1)
    s2 = jnp.concatenate([-sin, z, z, z], axis=1)
    return c, s1, s2


def _layer(x, c, pos, norm_attn_g, w_ada, b_ada, w_in, g_q, w_uq, g_kv, w_ukv, g_out_swa, g_out_mla, w_o,
           norm_ffn_g, w_router, router_bias, w_exp_gate, w_exp_up, w_exp_down, w_sh_gate, w_sh_up, w_sh_down,
           final_g):
    s, d = x.shape
    row = lambda a: a.reshape(1, -1)

    mod = _ada(c, w_ada, b_ada)
    shift_a, scale_a, gate_a, shift_f, scale_f, gate_f = [mod[:, i * d:(i + 1) * d] for i in range(N_ADA)]

    n_qkv = 3 * D_SWA
    w_qkv = w_in[:, :n_qkv].astype(BF16)
    w_rest = jnp.pad(w_in[:, n_qkv:], ((0, 0), (0, LANES - QK_ROPE_DIM))).astype(BF16)
    dils = tuple(dil for _, dil in SWA_PATTERNS)
    assert all(window // dil == SWA_BLOCK and s % (dil * SWA_BLOCK) == 0 for window, dil in SWA_PATTERNS)
    rest, qkv_views = _inproj(x, row(norm_attn_g), scale_a, shift_a, w_qkv, w_rest, dils)

    dq = QK_NOPE_DIM + QK_ROPE_DIM
    w_uq_p = jnp.pad(w_uq.reshape(Q_LORA_RANK, N_HEADS_MLA, dq), ((0, 0), (0, 0), (0, MLA_QK_PAD - dq)))
    w_uq_p = w_uq_p.reshape(Q_LORA_RANK, N_HEADS_MLA * MLA_QK_PAD).astype(BF16)
    rc, rs1, rs2 = _rope_tables(pos)
    q_m, k_m, v_m = _mlaproj(rest, row(g_q), row(g_kv), w_uq_p, w_ukv.astype(BF16), rc, rs1, rs2)
    o_b = _mla(q_m, k_m, v_m)

    posf = pos.astype(F32)
    o_pats, lse_pats = zip(*[_dilated(qkv_v, posf, dil) for qkv_v, dil in zip(qkv_views, dils)])

    x1, h2p, logits_t = _outproj(o_pats, lse_pats, dils, o_b, x, row(g_out_swa), row(g_out_mla),
                                 w_o.astype(BF16), gate_a, row(norm_ffn_g), scale_f, shift_f, w_router.T)

    eidx, wts, rank, cnt = _route(logits_t, router_bias)
    tr = TR_EXPERT
    counts = cnt[:, 0].astype(I32)
    padded = (counts + tr - 1) // tr * tr
    e_ids = jnp.arange(N_EXPERTS, dtype=I32)
    pad_end = jnp.sum(jnp.where(e_ids[None, :] <= e_ids[:, None], padded[None, :], 0), axis=1)
    pad_start = pad_end - padded
    lookup = lambda table, idx: jnp.sum(jnp.where(idx[..., None] == e_ids, table, 0), axis=-1)
    dest = lookup(pad_start, eidx) + rank
    n_slots = s * TOP_K + N_EXPERTS * tr
    blk_start = jnp.arange(n_slots // tr, dtype=I32) * tr
    blk_e = jnp.minimum(jnp.sum((pad_end[None, :] <= blk_start[:, None]).astype(I32), axis=1), N_EXPERTS - 1)
    blk_nv = jnp.clip(lookup(counts, blk_e) - (blk_start - lookup(pad_start, blk_e)), 0, tr)
    blk_new = ((blk_nv > 0) & (blk_start == lookup(pad_start, blk_e))).astype(I32)

    xs = _dispatch(dest, h2p, n_slots)
    y = _experts(blk_e, blk_nv, blk_new, xs, w_exp_gate, w_exp_up, w_exp_down)
    w_sgu = jnp.concatenate([w_sh_gate, w_sh_up], axis=1).astype(BF16)
    return _combine(dest, wts.T, h2p, x1, gate_f, row(final_g), w_sgu, w_sh_down.astype(BF16), y)


def kernel(x, c, positions, norm_attn_g, w_ada, b_ada, w_in, g_q, w_uq, g_kv, w_ukv, g_out_swa, g_out_mla, w_o,
           norm_ffn_g, w_router, router_bias, w_exp_gate, w_exp_up, w_exp_down, w_sh_gate, w_sh_up, w_sh_down,
           final_norm_g):
    assert x.shape[0] == 1 and w_ada.shape[0] == 1
    out = _layer(x[0], c[0], positions[0], norm_attn_g[0], w_ada[0], b_ada[0], w_in[0], g_q[0], w_uq[0], g_kv[0],
                 w_ukv[0], g_out_swa[0], g_out_mla[0], w_o[0], norm_ffn_g[0], w_router[0], router_bias[0],
                 w_exp_gate[0], w_exp_up[0], w_exp_down[0], w_sh_gate[0], w_sh_up[0], w_sh_down[0], final_norm_g)
    return out[None]
```

```python
import functools

import jax
import jax.numpy as jnp
from jax import lax
from jax.experimental import pallas as pl
from jax.experimental.pallas import tpu as pltpu

F32 = jnp.float32
BF16 = jnp.bfloat16
I32 = jnp.int32

D_MODEL = 2048
N_HEADS_SWA = 8
HEAD_DIM_SWA = 128
SWA_PATTERNS = ((128, 1), (512, 4), (2048, 16))
SWA_BLOCK = 128
N_HEADS_MLA = 8
Q_LORA_RANK = 512
KV_LORA_RANK = 256
QK_NOPE_DIM = 128
QK_ROPE_DIM = 64
V_HEAD_DIM = 128
ROPE_THETA = 10000.0
D_SWA = N_HEADS_SWA * HEAD_DIM_SWA
D_MLA = N_HEADS_MLA * V_HEAD_DIM
N_EXPERTS = 64
N_GROUPS = 8
TOPK_GROUPS = 4
TOP_K = 8
D_EXPERT = 512
ROUTED_SCALE = 2.5
N_ADA = 6
EPS = 1e-6
NEG_INF = -1e30
LOG2E = 1.4426950408889634

LANES = 128
MLA_QK_PAD = 256
PACK_ROWS = D_MODEL // (2 * LANES)
VMEM_LIMIT = 56 * 1024 * 1024

TM_INPROJ = 256
TM_MLAPROJ = 512
T_MLA = 1024
TM_OUTPROJ = 256
OUTPROJ_SUBTILES = 2
TN_ROUTE = 512
TB_DISPATCH = 256
TR_EXPERT = 256
EXPERT_SUBTILES = 2
TB_COMBINE = 128


def _params(*sem):
    return pltpu.CompilerParams(dimension_semantics=sem, vmem_limit_bytes=VMEM_LIMIT)


def _rms(x, g):
    return x * lax.rsqrt(jnp.mean(x * x, axis=-1, keepdims=True) + EPS) * g


def _resident(shape):
    nd = len(shape)
    return pl.BlockSpec(shape, lambda *_: (0,) * nd, pipeline_mode=pl.Buffered(1))


def _pack_bf16_pairs(a, b):
    ua = lax.bitcast_convert_type(a.astype(BF16).astype(F32), I32)
    ub = lax.bitcast_convert_type(b.astype(BF16).astype(F32), I32)
    return lax.shift_right_logical(ua, jnp.int32(16)) | (ub & jnp.int32(-65536))


def _unpack_bf16_pairs(w):
    lo = lax.bitcast_convert_type(lax.shift_left(w, jnp.int32(16)), F32).astype(BF16)
    hi = lax.bitcast_convert_type(w & jnp.int32(-65536), F32).astype(BF16)
    return jnp.concatenate([lo, hi], axis=1)


def _packed_chunk(ref, s, rows):
    return _unpack_bf16_pairs(ref[pl.ds(s, rows, stride=PACK_ROWS), :])


def _ada_body(c_ref, w_ref, b_ref, o_ref):
    c = c_ref[...]
    a = c * jax.nn.sigmoid(c)
    o_ref[...] = jnp.sum(w_ref[...] * a, axis=0, keepdims=True) + b_ref[...]


def _ada(c, w_ada, b_ada):
    d, n = w_ada.shape
    tn = 512
    return pl.pallas_call(
        _ada_body,
        grid=(n // tn,),
        in_specs=[pl.BlockSpec((d, 1), lambda j: (0, 0)),
                  pl.BlockSpec((d, tn), lambda j: (0, j)),
                  pl.BlockSpec((1, tn), lambda j: (0, j))],
        out_specs=pl.BlockSpec((1, tn), lambda j: (0, j)),
        out_shape=jax.ShapeDtypeStruct((1, n), F32),
        compiler_params=_params("parallel"),
        name="ada",
    )(c.reshape(d, 1), w_ada, b_ada.reshape(1, n))


def _inproj_body(x_ref, g_ref, sc_ref, sh_ref, wqkv_ref, wr_ref, rest_ref, *out_and_scratch, tm, dils):
    view_refs, res_ref = out_and_scratch[:-1], out_and_scratch[-1]
    n = wqkv_ref.shape[1]
    h = (_rms(x_ref[...], g_ref[...]) * (1.0 + sc_ref[...]) + sh_ref[...]).astype(BF16)
    rest_ref[...] = jnp.dot(h, wr_ref[...], preferred_element_type=F32)
    res = jnp.dot(h, wqkv_ref[...], preferred_element_type=F32)
    chunks = range(n // LANES)
    for c in chunks:
        res_ref[c] = res[:, c * LANES:(c + 1) * LANES]
    for dil, v_ref in zip(dils, view_refs):
        if dil == 1:
            v_ref[...] = res.astype(BF16)
            continue
        for r in range(dil):
            for c in chunks:
                b = r * n + c * LANES
                v_ref[:, b:b + LANES] = res_ref[c, pl.ds(r, tm // dil, stride=dil), :].astype(BF16)


def _inproj(x, g, scale, shift, w_qkv, w_rest, dils):
    s, d = x.shape
    tm = min(TM_INPROJ, s)
    n1, n2 = w_qkv.shape[1], w_rest.shape[1]
    row = lambda i: (i, 0)
    vec = pl.BlockSpec((1, d), lambda i: (0, 0))
    outs = pl.pallas_call(
        functools.partial(_inproj_body, tm=tm, dils=dils),
        grid=(s // tm,),
        in_specs=[pl.BlockSpec((tm, d), row), vec, vec, vec, _resident((d, n1)), _resident((d, n2))],
        out_specs=[pl.BlockSpec((tm, n2), row)] + [pl.BlockSpec((tm // dil, dil * n1), row) for dil in dils],
        out_shape=[jax.ShapeDtypeStruct((s, n2), F32)]
        + [jax.ShapeDtypeStruct((s // dil, dil * n1), BF16) for dil in dils],
        scratch_shapes=[pltpu.VMEM((n1 // LANES, tm, LANES), F32)],
        compiler_params=_params("parallel"),
        name="inproj",
    )(x, g, scale, shift, w_qkv, w_rest)
    return outs[0], outs[1:]


def _rope_tail(t, c, s1, s2):
    return t * c + pltpu.roll(t, 32, 1) * s1 + pltpu.roll(t, 96, 1) * s2


def _mlaproj_body(rest_ref, gq_ref, gkv_ref, wuq_ref, wukv_ref, c_ref, s1_ref, s2_ref,
                  q_ref, k_ref, v_ref, *, scale):
    rest = rest_ref[...]
    c, s1, s2 = c_ref[...], s1_ref[...], s2_ref[...]
    cq = _rms(rest[:, :Q_LORA_RANK], gq_ref[...]).astype(BF16)
    ckv = _rms(rest[:, Q_LORA_RANK:Q_LORA_RANK + KV_LORA_RANK], gkv_ref[...]).astype(BF16)
    ktail = _rope_tail(rest[:, Q_LORA_RANK + KV_LORA_RANK:], c, s1, s2).astype(BF16)
    q = jnp.dot(cq, wuq_ref[...], preferred_element_type=F32)
    kv = jnp.dot(ckv, wukv_ref[...], preferred_element_type=F32)
    for h in range(N_HEADS_MLA):
        b = h * MLA_QK_PAD
        q_ref[:, b:b + LANES] = (q[:, b:b + LANES] * scale).astype(BF16)
        q_ref[:, b + LANES:b + 2 * LANES] = (_rope_tail(q[:, b + LANES:b + 2 * LANES], c, s1, s2) * scale).astype(BF16)
        k_ref[:, b:b + LANES] = kv[:, b:b + LANES].astype(BF16)
        k_ref[:, b + LANES:b + 2 * LANES] = ktail
        v_ref[:, b:b + LANES] = kv[:, b + LANES:b + 2 * LANES].astype(BF16)
        v_ref[:, b + LANES:b + 2 * LANES] = jnp.ones((q.shape[0], LANES), BF16)


def _mlaproj(rest, g_q, g_kv, w_uq, w_ukv, rc, rs1, rs2):
    s, nr = rest.shape
    tm = min(TM_MLAPROJ, s)
    nq = N_HEADS_MLA * MLA_QK_PAD
    row = lambda i: (i, 0)
    tab = pl.BlockSpec((tm, LANES), row)
    scale = float(QK_NOPE_DIM + QK_ROPE_DIM) ** -0.5 * LOG2E
    return pl.pallas_call(
        functools.partial(_mlaproj_body, scale=scale),
        grid=(s // tm,),
        in_specs=[pl.BlockSpec((tm, nr), row),
                  pl.BlockSpec((1, Q_LORA_RANK), lambda i: (0, 0)),
                  pl.BlockSpec((1, KV_LORA_RANK), lambda i: (0, 0)),
                  _resident(w_uq.shape), _resident(w_ukv.shape), tab, tab, tab],
        out_specs=[pl.BlockSpec((tm, nq), row)] * 3,
        out_shape=[jax.ShapeDtypeStruct((s, nq), BF16)] * 3,
        compiler_params=_params("parallel"),
        name="mlaproj",
    )(rest, g_q, g_kv, w_uq, w_ukv, rc, rs1, rs2)


def _mla_body(q_ref, k_ref, v_ref, o_ref, m_ref, acc_ref, sa_ref, sb_ref, *, t):
    qi = pl.program_id(1)
    q = q_ref[...]
    m_ref[...] = jnp.full(m_ref.shape, NEG_INF, F32)
    acc_ref[...] = jnp.zeros(acc_ref.shape, F32)

    def scores(j, dst):
        k = k_ref[pl.ds(pl.multiple_of(j * t, t), t), :]
        dst[...] = lax.dot_general(q, k, (((1,), (1,)), ((), ())), preferred_element_type=F32)

    def absorb(j, src, masked):
        s = src[...]
        if masked:
            r = lax.broadcasted_iota(I32, (t, t), 0)
            cidx = lax.broadcasted_iota(I32, (t, t), 1)
            s = jnp.where(cidx <= r, s, NEG_INF)
        v = v_ref[pl.ds(pl.multiple_of(j * t, t), t), :]
        m_old = m_ref[...]
        m_new = jnp.maximum(m_old, jnp.max(s, axis=-1, keepdims=True))
        p = jnp.exp2(s - m_new).astype(BF16)
        acc_ref[...] = jnp.exp2(m_old - m_new) * acc_ref[...] + jnp.dot(p, v, preferred_element_type=F32)
        m_ref[...] = m_new

    scores(0, sa_ref)

    def pair(i, carry):
        j = 2 * i
        scores(j + 1, sb_ref)
        absorb(j, sa_ref, False)
        scores(j + 2, sa_ref)
        absorb(j + 1, sb_ref, False)
        return carry

    lax.fori_loop(0, qi // 2, pair, 0)

    @pl.when(qi % 2 == 0)
    def _():
        absorb(qi, sa_ref, True)

    @pl.when(qi % 2 == 1)
    def _():
        scores(qi, sb_ref)
        absorb(qi - 1, sa_ref, False)
        absorb(qi, sb_ref, True)

    acc = acc_ref[...]
    o_ref[...] = acc[:, :V_HEAD_DIM] / acc[:, V_HEAD_DIM:]


def _mla(q, k, v):
    s = q.shape[0]
    t = min(T_MLA, s)
    head_cols = lambda h, i: (0, h)
    return pl.pallas_call(
        functools.partial(_mla_body, t=t),
        grid=(N_HEADS_MLA, s // t),
        in_specs=[pl.BlockSpec((t, MLA_QK_PAD), lambda h, i: (i, h)),
                  pl.BlockSpec((s, MLA_QK_PAD), head_cols, pipeline_mode=pl.Buffered(1)),
                  pl.BlockSpec((s, MLA_QK_PAD), head_cols, pipeline_mode=pl.Buffered(1))],
        out_specs=pl.BlockSpec((t, V_HEAD_DIM), lambda h, i: (i, h)),
        out_shape=jax.ShapeDtypeStruct((s, D_MLA), F32),
        scratch_shapes=[pltpu.VMEM((t, 1), F32), pltpu.VMEM((t, MLA_QK_PAD), F32),
                        pltpu.VMEM((t, t), F32), pltpu.VMEM((t, t), F32)],
        compiler_params=_params("parallel", "arbitrary"),
        name="mla",
    )(q, k, v)


def _dilated_body(q_ref, kc_ref, kp_ref, vc_ref, vp_ref, pq_ref, pkc_ref, pkp_ref, o_ref, lse_ref):
    n = pl.program_id(1)
    blk = SWA_BLOCK
    i = lax.broadcasted_iota(I32, (blk, blk), 0)
    j = lax.broadcasted_iota(I32, (blk, blk), 1)
    ok_cur = j <= i
    ok_prev = (j >= i) & (n > 0)
    pq = pq_ref[...]
    dist_cur = jnp.abs(pq - pkc_ref[0])
    dist_prev = jnp.abs(pq - pkp_ref[0])
    scale = float(HEAD_DIM_SWA) ** -0.5
    nt = (((1,), (1,)), ((), ()))
    heads = range(N_HEADS_SWA)
    hs = [slice(h * HEAD_DIM_SWA, (h + 1) * HEAD_DIM_SWA) for h in heads]
    slope = [2.0 ** (-8.0 * (h + 1) / N_HEADS_SWA) for h in heads]
    ones = jnp.ones((blk, HEAD_DIM_SWA), BF16)
    sc = [jnp.where(ok_cur, lax.dot_general(q_ref[:, hs[h]], kc_ref[:, hs[h]], nt, preferred_element_type=F32)
                    * scale - slope[h] * dist_cur, NEG_INF) for h in heads]
    sp = [jnp.where(ok_prev, lax.dot_general(q_ref[:, hs[h]], kp_ref[:, hs[h]], nt, preferred_element_type=F32)
                    * scale - slope[h] * dist_prev, NEG_INF) for h in heads]
    m = [jnp.max(jnp.maximum(sc[h], sp[h]), axis=-1, keepdims=True) for h in heads]
    pc = [jnp.exp(sc[h] - m[h]).astype(BF16) for h in heads]
    pp = [jnp.exp(sp[h] - m[h]).astype(BF16) for h in heads]
    acc = [jnp.dot(pc[h], jnp.concatenate([vc_ref[:, hs[h]], ones], axis=1), preferred_element_type=F32)
           + jnp.dot(pp[h], jnp.concatenate([vp_ref[:, hs[h]], ones], axis=1), preferred_element_type=F32)
           for h in heads]
    for h in heads:
        den = acc[h][:, HEAD_DIM_SWA:]
        o_ref[:, hs[h]] = acc[h][:, :HEAD_DIM_SWA] / den
        lse_ref[:, hs[h]] = m[h] + jnp.log(den)


def _dilated(qkv_v, posf, dil):
    sd = qkv_v.shape[0]
    nb = sd // SWA_BLOCK
    blk = SWA_BLOCK
    pos_v = posf.reshape(sd, dil)
    pq_v = jnp.repeat(pos_v, LANES, axis=1)
    pos_rows = pos_v.T.reshape(dil, 1, sd)
    prev = lambda n: jnp.maximum(n - 1, 0)
    wide = (blk, D_SWA)
    return pl.pallas_call(
        _dilated_body,
        grid=(dil, nb),
        in_specs=[pl.BlockSpec(wide, lambda r, n: (n, 3 * r)),
                  pl.BlockSpec(wide, lambda r, n: (n, 3 * r + 1)),
                  pl.BlockSpec(wide, lambda r, n: (prev(n), 3 * r + 1)),
                  pl.BlockSpec(wide, lambda r, n: (n, 3 * r + 2)),
                  pl.BlockSpec(wide, lambda r, n: (prev(n), 3 * r + 2)),
                  pl.BlockSpec((blk, LANES), lambda r, n: (n, r)),
                  pl.BlockSpec((1, 1, blk), lambda r, n: (r, 0, n)),
                  pl.BlockSpec((1, 1, blk), lambda r, n: (r, 0, prev(n)))],
        out_specs=[pl.BlockSpec(wide, lambda r, n: (n, r)), pl.BlockSpec(wide, lambda r, n: (n, r))],
        out_shape=[jax.ShapeDtypeStruct((sd, dil * D_SWA), F32), jax.ShapeDtypeStruct((sd, dil * D_SWA), F32)],
        compiler_params=_params("parallel", "parallel"),
        name=f"dil{dil}",
    )(qkv_v, qkv_v, qkv_v, qkv_v, qkv_v, pq_v, pos_rows, pos_rows)


def _outproj_body(*refs, tm, dils):
    npat = len(dils)
    o_views, l_views = refs[:npat], refs[npat:2 * npat]
    (ob_ref, x_ref, gsw_ref, gml_ref, wo_ref, ga_ref, nfg_ref, scf_ref, shf_ref, wrt_ref,
     x1_ref, h2p_ref, lgt_ref) = refs[2 * npat:2 * npat + 13]
    scratch = list(refs[2 * npat + 13:])

    chunks = range(D_SWA // LANES)

    def token_order(view_ref, dil):
        if dil == 1:
            return lambda rs: view_ref[rs, :]
        nat_ref = scratch.pop(0)
        for r in range(dil):
            for c in chunks:
                b = r * D_SWA + c * LANES
                nat_ref[c, pl.ds(r, tm // dil, stride=dil), :] = view_ref[:, b:b + LANES]
        return lambda rs: jnp.concatenate([nat_ref[c, rs, :] for c in chunks], axis=1)

    o1, o2, o3 = [token_order(v, dil) for v, dil in zip(o_views, dils)]
    l1f, l2f, l3f = [token_order(v, dil) for v, dil in zip(l_views, dils)]

    nsub = OUTPROJ_SUBTILES
    r = tm // nsub
    subs = range(nsub)
    rows = [slice(i * r, (i + 1) * r) for i in subs]

    def merged(rs):
        l1, l2, l3 = l1f(rs), l2f(rs), l3f(rs)
        m = jnp.maximum(jnp.maximum(l1, l2), l3)
        e1, e2, e3 = jnp.exp(l1 - m), jnp.exp(l2 - m), jnp.exp(l3 - m)
        return (e1 * o1(rs) + e2 * o2(rs) + e3 * o3(rs)) / (e1 + e2 + e3)

    mix = [jnp.concatenate([_rms(merged(rs), gsw_ref[...]), _rms(ob_ref[rs, :], gml_ref[...])],
                           axis=-1).astype(BF16) for rs in rows]
    proj = [jnp.dot(mix[i], wo_ref[...], preferred_element_type=F32) for i in subs]
    x1 = [x_ref[rows[i], :] + ga_ref[...] * proj[i] for i in subs]
    h2 = [_rms(x1[i], nfg_ref[...]) * (1.0 + scf_ref[...]) + shf_ref[...] for i in subs]
    for i in subs:
        x1_ref[rows[i], :] = x1[i]
        lgt_ref[:, rows[i]] = lax.dot_general(wrt_ref[...], h2[i], (((1,), (1,)), ((), ())),
                                              precision=lax.Precision.HIGHEST, preferred_element_type=F32)
    for i in subs:
        for s in range(PACK_ROWS):
            b = 2 * LANES * s
            h2p_ref[pl.ds(i * r * PACK_ROWS + s, r, stride=PACK_ROWS), :] = _pack_bf16_pairs(
                h2[i][:, b:b + LANES], h2[i][:, b + LANES:b + 2 * LANES])


def _outproj(o_pats, lse_pats, dils, o_b, x, g_sw, g_ml, w_o, gate_a, nfg, scale_f, shift_f, w_router_t):
    s, d = x.shape
    tm = min(TM_OUTPROJ, s)
    row = lambda i: (i, 0)
    views = [pl.BlockSpec((tm // dil, dil * D_SWA), row) for dil in dils]
    vec = lambda n: pl.BlockSpec((1, n), lambda i: (0, 0))
    n_reordered = 2 * sum(1 for dil in dils if dil > 1)
    return pl.pallas_call(
        functools.partial(_outproj_body, tm=tm, dils=dils),
        grid=(s // tm,),
        in_specs=views + views + [pl.BlockSpec((tm, D_MLA), row), pl.BlockSpec((tm, d), row), vec(D_SWA),
                                  vec(D_MLA), _resident(w_o.shape), vec(d), vec(d), vec(d), vec(d),
                                  _resident(w_router_t.shape)],
        out_specs=[pl.BlockSpec((tm, d), row), pl.BlockSpec((tm * PACK_ROWS, LANES), row),
                   pl.BlockSpec((N_EXPERTS, tm), lambda i: (0, i))],
        out_shape=[jax.ShapeDtypeStruct((s, d), F32), jax.ShapeDtypeStruct((s * PACK_ROWS, LANES), I32),
                   jax.ShapeDtypeStruct((N_EXPERTS, s), F32)],
        scratch_shapes=[pltpu.VMEM((D_SWA // LANES, tm, LANES), F32)] * n_reordered,
        compiler_params=_params("parallel"),
        name="outproj",
    )(*o_pats, *lse_pats, o_b, x, g_sw, g_ml, w_o, gate_a, nfg, scale_f, shift_f, w_router_t)


def _first_index(hit_value, x, iota, size, axis):
    return jnp.min(jnp.where(x == hit_value, iota, size), axis=axis, keepdims=True)


def _route_body(lgt_ref, bias_ref, tri_ref, eidx_ref, wts_ref, rank_ref, cnt_ref, carry_ref, *, tn):
    @pl.when(pl.program_id(0) == 0)
    def _():
        carry_ref[...] = jnp.zeros(carry_ref.shape, F32)

    gsz = N_EXPERTS // N_GROUPS
    scores = jax.nn.sigmoid(lgt_ref[...])
    choice = scores + bias_ref[...]
    neg = jnp.float32(-jnp.inf)

    g3 = choice.reshape(N_GROUPS, gsz, tn)
    i3 = lax.broadcasted_iota(I32, g3.shape, 1)
    m1 = jnp.max(g3, axis=1, keepdims=True)
    f1 = _first_index(m1, g3, i3, gsz, 1)
    m2 = jnp.max(jnp.where(i3 == f1, neg, g3), axis=1, keepdims=True)
    gs = (m1 + m2).reshape(N_GROUPS, tn)

    ig = lax.broadcasted_iota(I32, gs.shape, 0)
    gsel = jnp.zeros(gs.shape, F32)
    for _ in range(TOPK_GROUPS):
        hit = ig == _first_index(jnp.max(gs, axis=0, keepdims=True), gs, ig, N_GROUPS, 0)
        gsel = jnp.where(hit, 1.0, gsel)
        gs = jnp.where(hit, neg, gs)
    emask = jnp.broadcast_to(gsel.reshape(N_GROUPS, 1, tn), (N_GROUPS, gsz, tn)).reshape(N_EXPERTS, tn)
    cand = jnp.where(emask > 0.0, choice, NEG_INF)

    ie = lax.broadcasted_iota(I32, cand.shape, 0)
    picks, wsel = [], []
    onehot = jnp.zeros(cand.shape, F32)
    for _ in range(TOP_K):
        f = _first_index(jnp.max(cand, axis=0, keepdims=True), cand, ie, N_EXPERTS, 0)
        hit = ie == f
        picks.append(f)
        wsel.append(jnp.sum(jnp.where(hit, scores, 0.0), axis=0, keepdims=True))
        onehot = jnp.where(hit, 1.0, onehot)
        cand = jnp.where(hit, neg, cand)

    rank = carry_ref[...] + jnp.dot(onehot.astype(BF16), tri_ref[...], preferred_element_type=F32)
    carry_ref[...] = carry_ref[...] + jnp.sum(onehot, axis=1, keepdims=True)
    cnt_ref[...] = carry_ref[...]

    w = jnp.concatenate(wsel, axis=0)
    wts_ref[...] = w / jnp.sum(w, axis=0, keepdims=True) * ROUTED_SCALE
    eidx_ref[...] = jnp.concatenate(picks, axis=0)
    rank_ref[...] = jnp.concatenate(
        [jnp.sum(jnp.where(ie == f, rank, 0.0), axis=0, keepdims=True) for f in picks], axis=0).astype(I32)


def _route(logits_t, router_bias):
    e, n = logits_t.shape
    tn = min(TN_ROUTE, n)
    tri = (lax.broadcasted_iota(I32, (tn, tn), 0) < lax.broadcasted_iota(I32, (tn, tn), 1)).astype(BF16)
    col = lambda i: (0, i)
    return pl.pallas_call(
        functools.partial(_route_body, tn=tn),
        grid=(n // tn,),
        in_specs=[pl.BlockSpec((e, tn), col), pl.BlockSpec((e, 1), lambda i: (0, 0)),
                  pl.BlockSpec((tn, tn), lambda i: (0, 0))],
        out_specs=[pl.BlockSpec((TOP_K, tn), col), pl.BlockSpec((TOP_K, tn), col), pl.BlockSpec((TOP_K, tn), col),
                   pl.BlockSpec((e, 1), lambda i: (0, 0))],
        out_shape=[jax.ShapeDtypeStruct((TOP_K, n), I32), jax.ShapeDtypeStruct((TOP_K, n), F32),
                   jax.ShapeDtypeStruct((TOP_K, n), I32), jax.ShapeDtypeStruct((e, 1), F32)],
        scratch_shapes=[pltpu.VMEM((e, 1), F32)],
        compiler_params=_params("arbitrary"),
        name="route",
    )(logits_t, router_bias.reshape(e, 1), tri)


def _dispatch_body(tail_start_ref, tail_len_ref, dest_ref, h_ref, xs_ref, zeros_ref, sem, zsem, *, tb, tr):
    sizes = [1 << b for b in reversed(range((tr - 1).bit_length()))]

    def tail_copies(e, act):
        start, length = tail_start_ref[e], tail_len_ref[e]
        done = jnp.int32(0)
        for size in sizes:
            piece = (length & size) != 0
            dst = xs_ref.at[pl.ds(pl.multiple_of((start + done) * PACK_ROWS, PACK_ROWS), size * PACK_ROWS), :]
            copy = pltpu.make_async_copy(zeros_ref.at[pl.ds(0, size * PACK_ROWS), :], dst, zsem)
            pl.when(piece)(functools.partial(act, copy))
            done = done + (length & size)

    @pl.when(pl.program_id(0) == 0)
    def _():
        zeros_ref[...] = jnp.zeros(zeros_ref.shape, I32)

        def start_tail(e, carry):
            tail_copies(e, lambda copy: copy.start())
            return carry

        def wait_tail(e, carry):
            tail_copies(e, lambda copy: copy.wait())
            return carry

        lax.fori_loop(0, N_EXPERTS, start_tail, 0)
        lax.fori_loop(0, N_EXPERTS, wait_tail, 0)

    def row_copy(t, k):
        src = h_ref.at[pl.ds(pl.multiple_of(t * PACK_ROWS, PACK_ROWS), PACK_ROWS), :]
        dst = xs_ref.at[pl.ds(pl.multiple_of(dest_ref[k, t] * PACK_ROWS, PACK_ROWS), PACK_ROWS), :]
        return pltpu.make_async_copy(src, dst, sem)

    def issue(t, carry):
        for k in range(TOP_K):
            row_copy(t, k).start(priority=k % 2)
        return carry

    def drain(t, carry):
        for k in range(TOP_K):
            row_copy(t, k).wait()
        return carry

    lax.fori_loop(0, tb, issue, 0)
    lax.fori_loop(0, tb, drain, 0)


def _dispatch(dest, h2p, tail_start, tail_len, n_slots, tr):
    n = dest.shape[1]
    tb = min(TB_DISPATCH, n)
    max_piece = 1 << ((tr - 1).bit_length() - 1)
    grid_spec = pltpu.PrefetchScalarGridSpec(
        num_scalar_prefetch=2,
        grid=(n // tb,),
        in_specs=[pl.BlockSpec((TOP_K, tb), lambda i, ts, tl: (0, i), memory_space=pltpu.SMEM),
                  pl.BlockSpec((tb * PACK_ROWS, LANES), lambda i, ts, tl: (i, 0))],
        out_specs=pl.BlockSpec(memory_space=pl.ANY),
        scratch_shapes=[pltpu.VMEM((max_piece * PACK_ROWS, LANES), I32),
                        pltpu.SemaphoreType.DMA(()), pltpu.SemaphoreType.DMA(())],
    )
    return pl.pallas_call(
        functools.partial(_dispatch_body, tb=tb, tr=tr),
        grid_spec=grid_spec,
        out_shape=jax.ShapeDtypeStruct((n_slots * PACK_ROWS, LANES), I32),
        compiler_params=_params("arbitrary"),
        name="dispatch",
    )(tail_start, tail_len, dest, h2p)


def _swiglu_packed(xp_ref, wgu_ref, wd_ref, rows, nsub=1):
    r = rows // nsub
    subs = range(nsub)
    x = [jnp.concatenate([_unpack_bf16_pairs(xp_ref[pl.ds(i * r * PACK_ROWS + s, r, stride=PACK_ROWS), :])
                          for s in range(PACK_ROWS)], axis=1) for i in subs]
    h = [jnp.dot(x[i], wgu_ref[...], preferred_element_type=F32) for i in subs]
    a = [(h[i][:, :D_EXPERT] * jax.nn.sigmoid(h[i][:, :D_EXPERT]) * h[i][:, D_EXPERT:]).astype(BF16) for i in subs]
    return [jnp.dot(a[i], wd_ref[...], preferred_element_type=F32) for i in subs]


def _experts_body(be_ref, nv_ref, new_ref, xs_ref, wg_ref, wu_ref, wd_ref, y_ref, wgu_s, wd_s, *, tr):
    del be_ref
    b = pl.program_id(0)

    @pl.when(new_ref[b] > 0)
    def _():
        wgu_s[:, :D_EXPERT] = wg_ref[0].astype(BF16)
        wgu_s[:, D_EXPERT:] = wu_ref[0].astype(BF16)
        wd_s[...] = wd_ref[0].astype(BF16)

    @pl.when(nv_ref[b] > 0)
    def _():
        nsub = EXPERT_SUBTILES
        r = tr // nsub
        ys = _swiglu_packed(xs_ref, wgu_s, wd_s, tr, nsub)
        for i, y in enumerate(ys):
            for s in range(PACK_ROWS):
                c = 2 * LANES * s
                y_ref[pl.ds(i * r * PACK_ROWS + s, r, stride=PACK_ROWS), :] = _pack_bf16_pairs(
                    y[:, c:c + LANES], y[:, c + LANES:c + 2 * LANES])

    @pl.when(nv_ref[b] == 0)
    def _():
        y_ref[...] = jnp.zeros(y_ref.shape, I32)


def _experts(blk_e, blk_nv, blk_new, xs, w_gate, w_up, w_down):
    tr = TR_EXPERT
    nb = blk_e.shape[0]
    d, f = w_gate.shape[1], w_gate.shape[2]
    grid_spec = pltpu.PrefetchScalarGridSpec(
        num_scalar_prefetch=3,
        grid=(nb,),
        in_specs=[pl.BlockSpec((tr * PACK_ROWS, LANES), lambda b, be, nv, nw: (jnp.where(nv[b] > 0, b, 0), 0)),
                  pl.BlockSpec((1, d, f), lambda b, be, nv, nw: (be[b], 0, 0)),
                  pl.BlockSpec((1, d, f), lambda b, be, nv, nw: (be[b], 0, 0)),
                  pl.BlockSpec((1, f, d), lambda b, be, nv, nw: (be[b], 0, 0))],
        out_specs=pl.BlockSpec((tr * PACK_ROWS, LANES), lambda b, be, nv, nw: (b, 0)),
        scratch_shapes=[pltpu.VMEM((d, 2 * f), BF16), pltpu.VMEM((f, d), BF16)],
    )
    return pl.pallas_call(
        functools.partial(_experts_body, tr=tr),
        grid_spec=grid_spec,
        out_shape=jax.ShapeDtypeStruct((nb * tr * PACK_ROWS, LANES), I32),
        compiler_params=_params("arbitrary"),
        name="experts",
    )(blk_e, blk_nv, blk_new, xs, w_gate, w_up, w_down)


def _combine_body(dest_ref, wts_ref, h2p_ref, x1_ref, gf_ref, fg_ref, wsgu_ref, wsd_ref, y_ref,
                  out_ref, ybuf, sem, *, tb):
    def row_copy(t, k):
        src = y_ref.at[pl.ds(pl.multiple_of(dest_ref[k, t] * PACK_ROWS, PACK_ROWS), PACK_ROWS), :]
        dst = ybuf.at[pl.ds(pl.multiple_of((k * tb + t) * PACK_ROWS, PACK_ROWS), PACK_ROWS), :]
        return pltpu.make_async_copy(src, dst, sem)

    def issue(t, carry):
        for k in range(TOP_K):
            row_copy(t, k).start(priority=k % 2)
        return carry

    def drain(t, carry):
        for k in range(TOP_K):
            row_copy(t, k).wait()
        return carry

    lax.fori_loop(0, tb, issue, 0)
    shared = _swiglu_packed(h2p_ref, wsgu_ref, wsd_ref, tb)[0]
    lax.fori_loop(0, tb, drain, 0)

    wts = wts_ref[...]
    wb = [jnp.broadcast_to(wts[:, k:k + 1], (tb, LANES)) for k in range(TOP_K)]
    cols = []
    for s in range(PACK_ROWS):
        c = 2 * LANES * s
        lo, hi = shared[:, c:c + LANES], shared[:, c + LANES:c + 2 * LANES]
        for k in range(TOP_K):
            w = ybuf[pl.ds(k * tb * PACK_ROWS + s, tb, stride=PACK_ROWS), :]
            lo = lo + wb[k] * lax.bitcast_convert_type(lax.shift_left(w, jnp.int32(16)), F32)
            hi = hi + wb[k] * lax.bitcast_convert_type(w & jnp.int32(-65536), F32)
        cols += [lo, hi]
    moe = jnp.concatenate(cols, axis=1)
    out_ref[...] = _rms(x1_ref[...] + gf_ref[...] * moe, fg_ref[...])


def _combine(dest, wts_t, h2p, x1, gate_f, final_g, w_sgu, w_sd, y):
    n, d = x1.shape
    tb = min(TB_COMBINE, n)
    row = lambda i: (i, 0)
    vec = pl.BlockSpec((1, d), lambda i: (0, 0))
    return pl.pallas_call(
        functools.partial(_combine_body, tb=tb),
        grid=(n // tb,),
        in_specs=[pl.BlockSpec((TOP_K, tb), lambda i: (0, i), memory_space=pltpu.SMEM),
                  pl.BlockSpec((tb, TOP_K), row),
                  pl.BlockSpec((tb * PACK_ROWS, LANES), row),
                  pl.BlockSpec((tb, d), row), vec, vec,
                  _resident(w_sgu.shape), _resident(w_sd.shape),
                  pl.BlockSpec(memory_space=pl.ANY)],
        out_specs=pl.BlockSpec((tb, d), row),
        out_shape=jax.ShapeDtypeStruct((n, d), F32),
        scratch_shapes=[pltpu.VMEM((TOP_K * tb * PACK_ROWS, LANES), I32), pltpu.SemaphoreType.DMA(())],
        compiler_params=_params("arbitrary"),
        name="combine",
    )(dest, wts_t, h2p, x1, gate_f, final_g, w_sgu, w_sd, y)


def _rope_tables(pos):
    half = QK_ROPE_DIM // 2
    inv_freq = ROPE_THETA ** (-jnp.arange(half, dtype=F32) / half)
    ang = pos.astype(F32)[:, None] * inv_freq
    cos, sin = jnp.cos(ang), jnp.sin(ang)
    z = jnp.zeros_like(cos)
    c = jnp.concatenate([cos, cos, z, z], axis=1)
    s1 = jnp.concatenate([z, sin, z, z], axis=1)
    s2 = jnp.concatenate([-sin, z, z, z], axis=1)
    return c, s1, s2


def _layer(x, c, pos, norm_attn_g, w_ada, b_ada, w_in, g_q, w_uq, g_kv, w_ukv, g_out_swa, g_out_mla, w_o,
           norm_ffn_g, w_router, router_bias, w_exp_gate, w_exp_up, w_exp_down, w_sh_gate, w_sh_up, w_sh_down,
           final_g):
    s, d = x.shape
    row = lambda a: a.reshape(1, -1)

    mod = _ada(c, w_ada, b_ada)
    shift_a, scale_a, gate_a, shift_f, scale_f, gate_f = [mod[:, i * d:(i + 1) * d] for i in range(N_ADA)]

    n_qkv = 3 * D_SWA
    w_qkv = w_in[:, :n_qkv].astype(BF16)
    w_rest = jnp.pad(w_in[:, n_qkv:], ((0, 0), (0, LANES - QK_ROPE_DIM))).astype(BF16)
    dils = tuple(dil for _, dil in SWA_PATTERNS)
    assert all(window // dil == SWA_BLOCK and s % (dil * SWA_BLOCK) == 0 for window, dil in SWA_PATTERNS)
    rest, qkv_views = _inproj(x, row(norm_attn_g), scale_a, shift_a, w_qkv, w_rest, dils)

    dq = QK_NOPE_DIM + QK_ROPE_DIM
    w_uq_p = jnp.pad(w_uq.reshape(Q_LORA_RANK, N_HEADS_MLA, dq), ((0, 0), (0, 0), (0, MLA_QK_PAD - dq)))
    w_uq_p = w_uq_p.reshape(Q_LORA_RANK, N_HEADS_MLA * MLA_QK_PAD).astype(BF16)
    rc, rs1, rs2 = _rope_tables(pos)
    q_m, k_m, v_m = _mlaproj(rest, row(g_q), row(g_kv), w_uq_p, w_ukv.astype(BF16), rc, rs1, rs2)
    o_b = _mla(q_m, k_m, v_m)

    posf = pos.astype(F32)
    o_pats, lse_pats = zip(*[_dilated(qkv_v, posf, dil) for qkv_v, dil in zip(qkv_views, dils)])

    x1, h2p, logits_t = _outproj(o_pats, lse_pats, dils, o_b, x, row(g_out_swa), row(g_out_mla),
                                 w_o.astype(BF16), gate_a, row(norm_ffn_g), scale_f, shift_f, w_router.T)

    eidx, wts, rank, cnt = _route(logits_t, router_bias)
    tr = TR_EXPERT
    counts = cnt[:, 0].astype(I32)
    padded = (counts + tr - 1) // tr * tr
    e_ids = jnp.arange(N_EXPERTS, dtype=I32)
    pad_end = jnp.sum(jnp.where(e_ids[None, :] <= e_ids[:, None], padded[None, :], 0), axis=1)
    pad_start = pad_end - padded
    lookup = lambda table, idx: jnp.sum(jnp.where(idx[..., None] == e_ids, table, 0), axis=-1)
    dest = lookup(pad_start, eidx) + rank
    n_slots = s * TOP_K + N_EXPERTS * tr
    blk_start = jnp.arange(n_slots // tr, dtype=I32) * tr
    blk_e = jnp.minimum(jnp.sum((pad_end[None, :] <= blk_start[:, None]).astype(I32), axis=1), N_EXPERTS - 1)
    blk_nv = jnp.clip(lookup(counts, blk_e) - (blk_start - lookup(pad_start, blk_e)), 0, tr)
    blk_new = ((blk_nv > 0) & (blk_start == lookup(pad_start, blk_e))).astype(I32)

    xs = _dispatch(dest, h2p, pad_start + counts, padded - counts, n_slots, tr)
    y = _experts(blk_e, blk_nv, blk_new, xs, w_exp_gate, w_exp_up, w_exp_down)
    w_sgu = jnp.concatenate([w_sh_gate, w_sh_up], axis=1).astype(BF16)
    return _combine(dest, wts.T, h2p, x1, gate_f, row(final_g), w_sgu, w_sh_down.astype(BF16), y)


def kernel(x, c, positions, norm_attn_g, w_ada, b_ada, w_in, g_q, w_uq, g_kv, w_ukv, g_out_swa, g_out_mla, w_o,
           norm_ffn_g, w_router, router_bias, w_exp_gate, w_exp_up, w_exp_down, w_sh_gate, w_sh_up, w_sh_down,
           final_norm_g):
    assert x.shape[0] == 1 and w_ada.shape[0] == 1
    out = _layer(x[0], c[0], positions[0], norm_attn_g[0], w_ada[0], b_ada[0], w_in[0], g_q[0], w_uq[0], g_kv[0],
                 w_ukv[0], g_out_swa[0], g_out_mla[0], w_o[0], norm_ffn_g[0], w_router[0], router_bias[0],
                 w_exp_gate[0], w_exp_up[0], w_exp_down[0], w_sh_gate[0], w_sh_up[0], w_sh_down[0], final_norm_g)
    return out[None]
```

```python
import functools

import jax
import jax.numpy as jnp
from jax import lax
from jax.experimental import pallas as pl
from jax.experimental.pallas import tpu as pltpu

F32 = jnp.float32
BF16 = jnp.bfloat16
I32 = jnp.int32

D_MODEL = 2048
N_HEADS_SWA = 8
HEAD_DIM_SWA = 128
SWA_PATTERNS = ((128, 1), (512, 4), (2048, 16))
SWA_BLOCK = 128
N_HEADS_MLA = 8
Q_LORA_RANK = 512
KV_LORA_RANK = 256
QK_NOPE_DIM = 128
QK_ROPE_DIM = 64
V_HEAD_DIM = 128
ROPE_THETA = 10000.0
D_SWA = N_HEADS_SWA * HEAD_DIM_SWA
D_MLA = N_HEADS_MLA * V_HEAD_DIM
N_EXPERTS = 64
N_GROUPS = 8
TOPK_GROUPS = 4
TOP_K = 8
D_EXPERT = 512
ROUTED_SCALE = 2.5
N_ADA = 6
EPS = 1e-6
NEG_INF = -1e30
LOG2E = 1.4426950408889634

LANES = 128
MLA_QK_PAD = 256
PACK_ROWS = D_MODEL // (2 * LANES)
VMEM_LIMIT = 56 * 1024 * 1024

TM_INPROJ = 256
TM_MLAPROJ = 512
T_MLA = 1024
TM_OUTPROJ = 256
OUTPROJ_SUBTILES = 2
TN_ROUTE = 512
TB_DISPATCH = 256
TR_EXPERT = 512
EXPERT_SUBTILES = 2
TB_COMBINE = 256


def _params(*sem):
    return pltpu.CompilerParams(dimension_semantics=sem, vmem_limit_bytes=VMEM_LIMIT)


def _rms(x, g):
    return x * lax.rsqrt(jnp.mean(x * x, axis=-1, keepdims=True) + EPS) * g


def _resident(shape):
    nd = len(shape)
    return pl.BlockSpec(shape, lambda *_: (0,) * nd, pipeline_mode=pl.Buffered(1))


def _pack_bf16_pairs(a, b):
    ua = lax.bitcast_convert_type(a.astype(BF16).astype(F32), I32)
    ub = lax.bitcast_convert_type(b.astype(BF16).astype(F32), I32)
    return lax.shift_right_logical(ua, jnp.int32(16)) | (ub & jnp.int32(-65536))


def _unpack_bf16_pairs(w):
    lo = lax.bitcast_convert_type(lax.shift_left(w, jnp.int32(16)), F32).astype(BF16)
    hi = lax.bitcast_convert_type(w & jnp.int32(-65536), F32).astype(BF16)
    return jnp.concatenate([lo, hi], axis=1)


def _packed_chunk(ref, s, rows):
    return _unpack_bf16_pairs(ref[pl.ds(s, rows, stride=PACK_ROWS), :])


def _ada_body(c_ref, w_ref, b_ref, o_ref):
    c = c_ref[...]
    a = c * jax.nn.sigmoid(c)
    o_ref[...] = jnp.sum(w_ref[...] * a, axis=0, keepdims=True) + b_ref[...]


def _ada(c, w_ada, b_ada):
    d, n = w_ada.shape
    tn = 512
    return pl.pallas_call(
        _ada_body,
        grid=(n // tn,),
        in_specs=[pl.BlockSpec((d, 1), lambda j: (0, 0)),
                  pl.BlockSpec((d, tn), lambda j: (0, j)),
                  pl.BlockSpec((1, tn), lambda j: (0, j))],
        out_specs=pl.BlockSpec((1, tn), lambda j: (0, j)),
        out_shape=jax.ShapeDtypeStruct((1, n), F32),
        compiler_params=_params("parallel"),
        name="ada",
    )(c.reshape(d, 1), w_ada, b_ada.reshape(1, n))


def _inproj_body(x_ref, g_ref, sc_ref, sh_ref, wqkv_ref, wr_ref, rest_ref, *out_and_scratch, tm, dils):
    view_refs, res_ref = out_and_scratch[:-1], out_and_scratch[-1]
    n = wqkv_ref.shape[1]
    h = (_rms(x_ref[...], g_ref[...]) * (1.0 + sc_ref[...]) + sh_ref[...]).astype(BF16)
    rest_ref[...] = jnp.dot(h, wr_ref[...], preferred_element_type=F32)
    res = jnp.dot(h, wqkv_ref[...], preferred_element_type=F32)
    chunks = range(n // LANES)
    for c in chunks:
        res_ref[c] = res[:, c * LANES:(c + 1) * LANES]
    for dil, v_ref in zip(dils, view_refs):
        if dil == 1:
            v_ref[...] = res.astype(BF16)
            continue
        for r in range(dil):
            for c in chunks:
                b = r * n + c * LANES
                v_ref[:, b:b + LANES] = res_ref[c, pl.ds(r, tm // dil, stride=dil), :].astype(BF16)


def _inproj(x, g, scale, shift, w_qkv, w_rest, dils):
    s, d = x.shape
    tm = min(TM_INPROJ, s)
    n1, n2 = w_qkv.shape[1], w_rest.shape[1]
    row = lambda i: (i, 0)
    vec = pl.BlockSpec((1, d), lambda i: (0, 0))
    outs = pl.pallas_call(
        functools.partial(_inproj_body, tm=tm, dils=dils),
        grid=(s // tm,),
        in_specs=[pl.BlockSpec((tm, d), row), vec, vec, vec, _resident((d, n1)), _resident((d, n2))],
        out_specs=[pl.BlockSpec((tm, n2), row)] + [pl.BlockSpec((tm // dil, dil * n1), row) for dil in dils],
        out_shape=[jax.ShapeDtypeStruct((s, n2), F32)]
        + [jax.ShapeDtypeStruct((s // dil, dil * n1), BF16) for dil in dils],
        scratch_shapes=[pltpu.VMEM((n1 // LANES, tm, LANES), F32)],
        compiler_params=_params("parallel"),
        name="inproj",
    )(x, g, scale, shift, w_qkv, w_rest)
    return outs[0], outs[1:]


def _rope_tail(t, c, s1, s2):
    return t * c + pltpu.roll(t, 32, 1) * s1 + pltpu.roll(t, 96, 1) * s2


def _mlaproj_body(rest_ref, gq_ref, gkv_ref, wuq_ref, wukv_ref, c_ref, s1_ref, s2_ref,
                  q_ref, k_ref, v_ref, *, scale):
    rest = rest_ref[...]
    c, s1, s2 = c_ref[...], s1_ref[...], s2_ref[...]
    cq = _rms(rest[:, :Q_LORA_RANK], gq_ref[...]).astype(BF16)
    ckv = _rms(rest[:, Q_LORA_RANK:Q_LORA_RANK + KV_LORA_RANK], gkv_ref[...]).astype(BF16)
    ktail = _rope_tail(rest[:, Q_LORA_RANK + KV_LORA_RANK:], c, s1, s2).astype(BF16)
    q = jnp.dot(cq, wuq_ref[...], preferred_element_type=F32)
    kv = jnp.dot(ckv, wukv_ref[...], preferred_element_type=F32)
    for h in range(N_HEADS_MLA):
        b = h * MLA_QK_PAD
        q_ref[:, b:b + LANES] = (q[:, b:b + LANES] * scale).astype(BF16)
        q_ref[:, b + LANES:b + 2 * LANES] = (_rope_tail(q[:, b + LANES:b + 2 * LANES], c, s1, s2) * scale).astype(BF16)
        k_ref[:, b:b + LANES] = kv[:, b:b + LANES].astype(BF16)
        k_ref[:, b + LANES:b + 2 * LANES] = ktail
        v_ref[:, b:b + LANES] = kv[:, b + LANES:b + 2 * LANES].astype(BF16)
        v_ref[:, b + LANES:b + 2 * LANES] = jnp.ones((q.shape[0], LANES), BF16)


def _mlaproj(rest, g_q, g_kv, w_uq, w_ukv, rc, rs1, rs2):
    s, nr = rest.shape
    tm = min(TM_MLAPROJ, s)
    nq = N_HEADS_MLA * MLA_QK_PAD
    row = lambda i: (i, 0)
    tab = pl.BlockSpec((tm, LANES), row)
    scale = float(QK_NOPE_DIM + QK_ROPE_DIM) ** -0.5 * LOG2E
    return pl.pallas_call(
        functools.partial(_mlaproj_body, scale=scale),
        grid=(s // tm,),
        in_specs=[pl.BlockSpec((tm, nr), row),
                  pl.BlockSpec((1, Q_LORA_RANK), lambda i: (0, 0)),
                  pl.BlockSpec((1, KV_LORA_RANK), lambda i: (0, 0)),
                  _resident(w_uq.shape), _resident(w_ukv.shape), tab, tab, tab],
        out_specs=[pl.BlockSpec((tm, nq), row)] * 3,
        out_shape=[jax.ShapeDtypeStruct((s, nq), BF16)] * 3,
        compiler_params=_params("parallel"),
        name="mlaproj",
    )(rest, g_q, g_kv, w_uq, w_ukv, rc, rs1, rs2)


def _mla_body(q_ref, k_ref, v_ref, o_ref, m_ref, acc_ref, sa_ref, sb_ref, *, t):
    qi = pl.program_id(1)
    q = q_ref[...]
    m_ref[...] = jnp.full(m_ref.shape, NEG_INF, F32)
    acc_ref[...] = jnp.zeros(acc_ref.shape, F32)

    def scores(j, dst):
        k = k_ref[pl.ds(pl.multiple_of(j * t, t), t), :]
        dst[...] = lax.dot_general(q, k, (((1,), (1,)), ((), ())), preferred_element_type=F32)

    def absorb(j, src, masked):
        s = src[...]
        if masked:
            r = lax.broadcasted_iota(I32, (t, t), 0)
            cidx = lax.broadcasted_iota(I32, (t, t), 1)
            s = jnp.where(cidx <= r, s, NEG_INF)
        v = v_ref[pl.ds(pl.multiple_of(j * t, t), t), :]
        m_old = m_ref[...]
        m_new = jnp.maximum(m_old, jnp.max(s, axis=-1, keepdims=True))
        p = jnp.exp2(s - m_new).astype(BF16)
        acc_ref[...] = jnp.exp2(m_old - m_new) * acc_ref[...] + jnp.dot(p, v, preferred_element_type=F32)
        m_ref[...] = m_new

    scores(0, sa_ref)

    def pair(i, carry):
        j = 2 * i
        scores(j + 1, sb_ref)
        absorb(j, sa_ref, False)
        scores(j + 2, sa_ref)
        absorb(j + 1, sb_ref, False)
        return carry

    lax.fori_loop(0, qi // 2, pair, 0)

    @pl.when(qi % 2 == 0)
    def _():
        absorb(qi, sa_ref, True)

    @pl.when(qi % 2 == 1)
    def _():
        scores(qi, sb_ref)
        absorb(qi - 1, sa_ref, False)
        absorb(qi, sb_ref, True)

    acc = acc_ref[...]
    o_ref[...] = acc[:, :V_HEAD_DIM] / acc[:, V_HEAD_DIM:]


def _mla(q, k, v):
    s = q.shape[0]
    t = min(T_MLA, s)
    head_cols = lambda h, i: (0, h)
    return pl.pallas_call(
        functools.partial(_mla_body, t=t),
        grid=(N_HEADS_MLA, s // t),
        in_specs=[pl.BlockSpec((t, MLA_QK_PAD), lambda h, i: (i, h)),
                  pl.BlockSpec((s, MLA_QK_PAD), head_cols, pipeline_mode=pl.Buffered(1)),
                  pl.BlockSpec((s, MLA_QK_PAD), head_cols, pipeline_mode=pl.Buffered(1))],
        out_specs=pl.BlockSpec((t, V_HEAD_DIM), lambda h, i: (i, h)),
        out_shape=jax.ShapeDtypeStruct((s, D_MLA), F32),
        scratch_shapes=[pltpu.VMEM((t, 1), F32), pltpu.VMEM((t, MLA_QK_PAD), F32),
                        pltpu.VMEM((t, t), F32), pltpu.VMEM((t, t), F32)],
        compiler_params=_params("parallel", "arbitrary"),
        name="mla",
    )(q, k, v)


def _dilated_body(q_ref, kc_ref, kp_ref, vc_ref, vp_ref, pq_ref, pkc_ref, pkp_ref, o_ref, lse_ref):
    n = pl.program_id(1)
    blk = SWA_BLOCK
    i = lax.broadcasted_iota(I32, (blk, blk), 0)
    j = lax.broadcasted_iota(I32, (blk, blk), 1)
    ok_cur = j <= i
    ok_prev = (j >= i) & (n > 0)
    pq = pq_ref[...]
    dist_cur = jnp.abs(pq - pkc_ref[0])
    dist_prev = jnp.abs(pq - pkp_ref[0])
    scale = float(HEAD_DIM_SWA) ** -0.5
    nt = (((1,), (1,)), ((), ()))
    heads = range(N_HEADS_SWA)
    hs = [slice(h * HEAD_DIM_SWA, (h + 1) * HEAD_DIM_SWA) for h in heads]
    slope = [2.0 ** (-8.0 * (h + 1) / N_HEADS_SWA) for h in heads]
    ones = jnp.ones((blk, HEAD_DIM_SWA), BF16)
    sc = [jnp.where(ok_cur, lax.dot_general(q_ref[:, hs[h]], kc_ref[:, hs[h]], nt, preferred_element_type=F32)
                    * scale - slope[h] * dist_cur, NEG_INF) for h in heads]
    sp = [jnp.where(ok_prev, lax.dot_general(q_ref[:, hs[h]], kp_ref[:, hs[h]], nt, preferred_element_type=F32)
                    * scale - slope[h] * dist_prev, NEG_INF) for h in heads]
    m = [jnp.max(jnp.maximum(sc[h], sp[h]), axis=-1, keepdims=True) for h in heads]
    pc = [jnp.exp(sc[h] - m[h]).astype(BF16) for h in heads]
    pp = [jnp.exp(sp[h] - m[h]).astype(BF16) for h in heads]
    acc = [jnp.dot(pc[h], jnp.concatenate([vc_ref[:, hs[h]], ones], axis=1), preferred_element_type=F32)
           + jnp.dot(pp[h], jnp.concatenate([vp_ref[:, hs[h]], ones], axis=1), preferred_element_type=F32)
           for h in heads]
    for h in heads:
        den = acc[h][:, HEAD_DIM_SWA:]
        o_ref[:, hs[h]] = acc[h][:, :HEAD_DIM_SWA] / den
        lse_ref[:, hs[h]] = m[h] + jnp.log(den)


def _dilated(qkv_v, posf, dil):
    sd = qkv_v.shape[0]
    nb = sd // SWA_BLOCK
    blk = SWA_BLOCK
    pos_v = posf.reshape(sd, dil)
    pq_v = jnp.repeat(pos_v, LANES, axis=1)
    pos_rows = pos_v.T.reshape(dil, 1, sd)
    prev = lambda n: jnp.maximum(n - 1, 0)
    wide = (blk, D_SWA)
    return pl.pallas_call(
        _dilated_body,
        grid=(dil, nb),
        in_specs=[pl.BlockSpec(wide, lambda r, n: (n, 3 * r)),
                  pl.BlockSpec(wide, lambda r, n: (n, 3 * r + 1)),
                  pl.BlockSpec(wide, lambda r, n: (prev(n), 3 * r + 1)),
                  pl.BlockSpec(wide, lambda r, n: (n, 3 * r + 2)),
                  pl.BlockSpec(wide, lambda r, n: (prev(n), 3 * r + 2)),
                  pl.BlockSpec((blk, LANES), lambda r, n: (n, r)),
                  pl.BlockSpec((1, 1, blk), lambda r, n: (r, 0, n)),
                  pl.BlockSpec((1, 1, blk), lambda r, n: (r, 0, prev(n)))],
        out_specs=[pl.BlockSpec(wide, lambda r, n: (n, r)), pl.BlockSpec(wide, lambda r, n: (n, r))],
        out_shape=[jax.ShapeDtypeStruct((sd, dil * D_SWA), F32), jax.ShapeDtypeStruct((sd, dil * D_SWA), F32)],
        compiler_params=_params("parallel", "parallel"),
        name=f"dil{dil}",
    )(qkv_v, qkv_v, qkv_v, qkv_v, qkv_v, pq_v, pos_rows, pos_rows)


def _outproj_body(*refs, tm, dils):
    npat = len(dils)
    o_views, l_views = refs[:npat], refs[npat:2 * npat]
    (ob_ref, x_ref, gsw_ref, gml_ref, wo_ref, ga_ref, nfg_ref, scf_ref, shf_ref, wrt_ref,
     x1_ref, h2p_ref, lgt_ref) = refs[2 * npat:2 * npat + 13]
    scratch = list(refs[2 * npat + 13:])

    chunks = range(D_SWA // LANES)

    def token_order(view_ref, dil):
        if dil == 1:
            return lambda rs: view_ref[rs, :]
        nat_ref = scratch.pop(0)
        for r in range(dil):
            for c in chunks:
                b = r * D_SWA + c * LANES
                nat_ref[c, pl.ds(r, tm // dil, stride=dil), :] = view_ref[:, b:b + LANES]
        return lambda rs: jnp.concatenate([nat_ref[c, rs, :] for c in chunks], axis=1)

    o1, o2, o3 = [token_order(v, dil) for v, dil in zip(o_views, dils)]
    l1f, l2f, l3f = [token_order(v, dil) for v, dil in zip(l_views, dils)]

    nsub = OUTPROJ_SUBTILES
    r = tm // nsub
    subs = range(nsub)
    rows = [slice(i * r, (i + 1) * r) for i in subs]

    def merged(rs):
        l1, l2, l3 = l1f(rs), l2f(rs), l3f(rs)
        m = jnp.maximum(jnp.maximum(l1, l2), l3)
        e1, e2, e3 = jnp.exp(l1 - m), jnp.exp(l2 - m), jnp.exp(l3 - m)
        return (e1 * o1(rs) + e2 * o2(rs) + e3 * o3(rs)) / (e1 + e2 + e3)

    mix = [jnp.concatenate([_rms(merged(rs), gsw_ref[...]), _rms(ob_ref[rs, :], gml_ref[...])],
                           axis=-1).astype(BF16) for rs in rows]
    proj = [jnp.dot(mix[i], wo_ref[...], preferred_element_type=F32) for i in subs]
    x1 = [x_ref[rows[i], :] + ga_ref[...] * proj[i] for i in subs]
    h2 = [_rms(x1[i], nfg_ref[...]) * (1.0 + scf_ref[...]) + shf_ref[...] for i in subs]
    for i in subs:
        x1_ref[rows[i], :] = x1[i]
        lgt_ref[:, rows[i]] = lax.dot_general(wrt_ref[...], h2[i], (((1,), (1,)), ((), ())),
                                              precision=lax.Precision.HIGHEST, preferred_element_type=F32)
    for i in subs:
        for s in range(PACK_ROWS):
            b = 2 * LANES * s
            h2p_ref[pl.ds(i * r * PACK_ROWS + s, r, stride=PACK_ROWS), :] = _pack_bf16_pairs(
                h2[i][:, b:b + LANES], h2[i][:, b + LANES:b + 2 * LANES])


def _outproj(o_pats, lse_pats, dils, o_b, x, g_sw, g_ml, w_o, gate_a, nfg, scale_f, shift_f, w_router_t):
    s, d = x.shape
    tm = min(TM_OUTPROJ, s)
    row = lambda i: (i, 0)
    views = [pl.BlockSpec((tm // dil, dil * D_SWA), row) for dil in dils]
    vec = lambda n: pl.BlockSpec((1, n), lambda i: (0, 0))
    n_reordered = 2 * sum(1 for dil in dils if dil > 1)
    return pl.pallas_call(
        functools.partial(_outproj_body, tm=tm, dils=dils),
        grid=(s // tm,),
        in_specs=views + views + [pl.BlockSpec((tm, D_MLA), row), pl.BlockSpec((tm, d), row), vec(D_SWA),
                                  vec(D_MLA), _resident(w_o.shape), vec(d), vec(d), vec(d), vec(d),
                                  _resident(w_router_t.shape)],
        out_specs=[pl.BlockSpec((tm, d), row), pl.BlockSpec((tm * PACK_ROWS, LANES), row),
                   pl.BlockSpec((N_EXPERTS, tm), lambda i: (0, i))],
        out_shape=[jax.ShapeDtypeStruct((s, d), F32), jax.ShapeDtypeStruct((s * PACK_ROWS, LANES), I32),
                   jax.ShapeDtypeStruct((N_EXPERTS, s), F32)],
        scratch_shapes=[pltpu.VMEM((D_SWA // LANES, tm, LANES), F32)] * n_reordered,
        compiler_params=_params("parallel"),
        name="outproj",
    )(*o_pats, *lse_pats, o_b, x, g_sw, g_ml, w_o, gate_a, nfg, scale_f, shift_f, w_router_t)


def _first_index(hit_value, x, iota, size, axis):
    return jnp.min(jnp.where(x == hit_value, iota, size), axis=axis, keepdims=True)


def _route_body(lgt_ref, bias_ref, tri_ref, eidx_ref, wts_ref, rank_ref, cnt_ref, carry_ref, *, tn):
    @pl.when(pl.program_id(0) == 0)
    def _():
        carry_ref[...] = jnp.zeros(carry_ref.shape, F32)

    gsz = N_EXPERTS // N_GROUPS
    scores = jax.nn.sigmoid(lgt_ref[...])
    choice = scores + bias_ref[...]
    neg = jnp.float32(-jnp.inf)

    g3 = choice.reshape(N_GROUPS, gsz, tn)
    i3 = lax.broadcasted_iota(I32, g3.shape, 1)
    m1 = jnp.max(g3, axis=1, keepdims=True)
    f1 = _first_index(m1, g3, i3, gsz, 1)
    m2 = jnp.max(jnp.where(i3 == f1, neg, g3), axis=1, keepdims=True)
    gs = (m1 + m2).reshape(N_GROUPS, tn)

    ig = lax.broadcasted_iota(I32, gs.shape, 0)
    gsel = jnp.zeros(gs.shape, F32)
    for _ in range(TOPK_GROUPS):
        hit = ig == _first_index(jnp.max(gs, axis=0, keepdims=True), gs, ig, N_GROUPS, 0)
        gsel = jnp.where(hit, 1.0, gsel)
        gs = jnp.where(hit, neg, gs)
    emask = jnp.broadcast_to(gsel.reshape(N_GROUPS, 1, tn), (N_GROUPS, gsz, tn)).reshape(N_EXPERTS, tn)
    cand = jnp.where(emask > 0.0, choice, NEG_INF)

    ie = lax.broadcasted_iota(I32, cand.shape, 0)
    picks, wsel = [], []
    onehot = jnp.zeros(cand.shape, F32)
    for _ in range(TOP_K):
        f = _first_index(jnp.max(cand, axis=0, keepdims=True), cand, ie, N_EXPERTS, 0)
        hit = ie == f
        picks.append(f)
        wsel.append(jnp.sum(jnp.where(hit, scores, 0.0), axis=0, keepdims=True))
        onehot = jnp.where(hit, 1.0, onehot)
        cand = jnp.where(hit, neg, cand)

    rank = carry_ref[...] + jnp.dot(onehot.astype(BF16), tri_ref[...], preferred_element_type=F32)
    carry_ref[...] = carry_ref[...] + jnp.sum(onehot, axis=1, keepdims=True)
    cnt_ref[...] = carry_ref[...]

    w = jnp.concatenate(wsel, axis=0)
    wts_ref[...] = w / jnp.sum(w, axis=0, keepdims=True) * ROUTED_SCALE
    eidx_ref[...] = jnp.concatenate(picks, axis=0)
    rank_ref[...] = jnp.concatenate(
        [jnp.sum(jnp.where(ie == f, rank, 0.0), axis=0, keepdims=True) for f in picks], axis=0).astype(I32)


def _route(logits_t, router_bias):
    e, n = logits_t.shape
    tn = min(TN_ROUTE, n)
    tri = (lax.broadcasted_iota(I32, (tn, tn), 0) < lax.broadcasted_iota(I32, (tn, tn), 1)).astype(BF16)
    col = lambda i: (0, i)
    return pl.pallas_call(
        functools.partial(_route_body, tn=tn),
        grid=(n // tn,),
        in_specs=[pl.BlockSpec((e, tn), col), pl.BlockSpec((e, 1), lambda i: (0, 0)),
                  pl.BlockSpec((tn, tn), lambda i: (0, 0))],
        out_specs=[pl.BlockSpec((TOP_K, tn), col), pl.BlockSpec((TOP_K, tn), col), pl.BlockSpec((TOP_K, tn), col),
                   pl.BlockSpec((e, 1), lambda i: (0, 0))],
        out_shape=[jax.ShapeDtypeStruct((TOP_K, n), I32), jax.ShapeDtypeStruct((TOP_K, n), F32),
                   jax.ShapeDtypeStruct((TOP_K, n), I32), jax.ShapeDtypeStruct((e, 1), F32)],
        scratch_shapes=[pltpu.VMEM((e, 1), F32)],
        compiler_params=_params("arbitrary"),
        name="route",
    )(logits_t, router_bias.reshape(e, 1), tri)


def _dispatch_body(tail_start_ref, tail_len_ref, dest_ref, h_ref, xs_ref, zeros_ref, sem, zsem, *, tb, tr):
    sizes = [1 << b for b in reversed(range((tr - 1).bit_length()))]

    def tail_copies(e, act):
        start, length = tail_start_ref[e], tail_len_ref[e]
        done = jnp.int32(0)
        for size in sizes:
            piece = (length & size) != 0
            dst = xs_ref.at[pl.ds(pl.multiple_of((start + done) * PACK_ROWS, PACK_ROWS), size * PACK_ROWS), :]
            copy = pltpu.make_async_copy(zeros_ref.at[pl.ds(0, size * PACK_ROWS), :], dst, zsem)
            pl.when(piece)(functools.partial(act, copy))
            done = done + (length & size)

    @pl.when(pl.program_id(0) == 0)
    def _():
        zeros_ref[...] = jnp.zeros(zeros_ref.shape, I32)

        def start_tail(e, carry):
            tail_copies(e, lambda copy: copy.start())
            return carry

        def wait_tail(e, carry):
            tail_copies(e, lambda copy: copy.wait())
            return carry

        lax.fori_loop(0, N_EXPERTS, start_tail, 0)
        lax.fori_loop(0, N_EXPERTS, wait_tail, 0)

    def row_copy(t, k):
        src = h_ref.at[pl.ds(pl.multiple_of(t * PACK_ROWS, PACK_ROWS), PACK_ROWS), :]
        dst = xs_ref.at[pl.ds(pl.multiple_of(dest_ref[t * TOP_K + k] * PACK_ROWS, PACK_ROWS), PACK_ROWS), :]
        return pltpu.make_async_copy(src, dst, sem)

    def issue(t, carry):
        for k in range(TOP_K):
            row_copy(t, k).start(priority=k % 2)
        return carry

    def drain(t, carry):
        for k in range(TOP_K):
            row_copy(t, k).wait()
        return carry

    lax.fori_loop(0, tb, issue, 0)
    lax.fori_loop(0, tb, drain, 0)


def _dispatch(dest, h2p, tail_start, tail_len, n_slots, tr):
    n = dest.shape[0] // TOP_K
    tb = min(TB_DISPATCH, n)
    max_piece = 1 << ((tr - 1).bit_length() - 1)
    grid_spec = pltpu.PrefetchScalarGridSpec(
        num_scalar_prefetch=2,
        grid=(n // tb,),
        in_specs=[pl.BlockSpec((tb * TOP_K,), lambda i, ts, tl: (i,), memory_space=pltpu.SMEM),
                  pl.BlockSpec((tb * PACK_ROWS, LANES), lambda i, ts, tl: (i, 0))],
        out_specs=pl.BlockSpec(memory_space=pl.ANY),
        scratch_shapes=[pltpu.VMEM((max_piece * PACK_ROWS, LANES), I32),
                        pltpu.SemaphoreType.DMA(()), pltpu.SemaphoreType.DMA(())],
    )
    return pl.pallas_call(
        functools.partial(_dispatch_body, tb=tb, tr=tr),
        grid_spec=grid_spec,
        out_shape=jax.ShapeDtypeStruct((n_slots * PACK_ROWS, LANES), I32),
        compiler_params=_params("arbitrary"),
        name="dispatch",
    )(tail_start, tail_len, dest, h2p)


def _swiglu_packed(xp_ref, wgu_ref, wd_ref, rows, nsub=1):
    r = rows // nsub
    subs = range(nsub)
    x = [jnp.concatenate([_unpack_bf16_pairs(xp_ref[pl.ds(i * r * PACK_ROWS + s, r, stride=PACK_ROWS), :])
                          for s in range(PACK_ROWS)], axis=1) for i in subs]
    h = [jnp.dot(x[i], wgu_ref[...], preferred_element_type=F32) for i in subs]
    a = [(h[i][:, :D_EXPERT] * jax.nn.sigmoid(h[i][:, :D_EXPERT]) * h[i][:, D_EXPERT:]).astype(BF16) for i in subs]
    return [jnp.dot(a[i], wd_ref[...], preferred_element_type=F32) for i in subs]


def _experts_body(be_ref, nv_ref, new_ref, xs_ref, wg_ref, wu_ref, wd_ref, y_ref, wgu_s, wd_s, *, tr):
    del be_ref
    b = pl.program_id(0)

    @pl.when(new_ref[b] > 0)
    def _():
        wgu_s[:, :D_EXPERT] = wg_ref[0].astype(BF16)
        wgu_s[:, D_EXPERT:] = wu_ref[0].astype(BF16)
        wd_s[...] = wd_ref[0].astype(BF16)

    @pl.when(nv_ref[b] > 0)
    def _():
        nsub = EXPERT_SUBTILES
        r = tr // nsub
        ys = _swiglu_packed(xs_ref, wgu_s, wd_s, tr, nsub)
        for i, y in enumerate(ys):
            for s in range(PACK_ROWS):
                c = 2 * LANES * s
                y_ref[pl.ds(i * r * PACK_ROWS + s, r, stride=PACK_ROWS), :] = _pack_bf16_pairs(
                    y[:, c:c + LANES], y[:, c + LANES:c + 2 * LANES])

    @pl.when(nv_ref[b] == 0)
    def _():
        y_ref[...] = jnp.zeros(y_ref.shape, I32)


def _experts(blk_e, blk_nv, blk_new, xs, w_gate, w_up, w_down):
    tr = TR_EXPERT
    nb = blk_e.shape[0]
    d, f = w_gate.shape[1], w_gate.shape[2]
    grid_spec = pltpu.PrefetchScalarGridSpec(
        num_scalar_prefetch=3,
        grid=(nb,),
        in_specs=[pl.BlockSpec((tr * PACK_ROWS, LANES), lambda b, be, nv, nw: (jnp.where(nv[b] > 0, b, 0), 0)),
                  pl.BlockSpec((1, d, f), lambda b, be, nv, nw: (be[b], 0, 0)),
                  pl.BlockSpec((1, d, f), lambda b, be, nv, nw: (be[b], 0, 0)),
                  pl.BlockSpec((1, f, d), lambda b, be, nv, nw: (be[b], 0, 0))],
        out_specs=pl.BlockSpec((tr * PACK_ROWS, LANES), lambda b, be, nv, nw: (b, 0)),
        scratch_shapes=[pltpu.VMEM((d, 2 * f), BF16), pltpu.VMEM((f, d), BF16)],
    )
    return pl.pallas_call(
        functools.partial(_experts_body, tr=tr),
        grid_spec=grid_spec,
        out_shape=jax.ShapeDtypeStruct((nb * tr * PACK_ROWS, LANES), I32),
        compiler_params=_params("arbitrary"),
        name="experts",
    )(blk_e, blk_nv, blk_new, xs, w_gate, w_up, w_down)


def _combine_body(dest_ref, wts_ref, h2p_ref, x1_ref, gf_ref, fg_ref, wsgu_ref, wsd_ref, y_ref,
                  out_ref, ybuf, sem, *, tb):
    def row_copy(t, k):
        src = y_ref.at[pl.ds(pl.multiple_of(dest_ref[t * TOP_K + k] * PACK_ROWS, PACK_ROWS), PACK_ROWS), :]
        dst = ybuf.at[pl.ds(pl.multiple_of((k * tb + t) * PACK_ROWS, PACK_ROWS), PACK_ROWS), :]
        return pltpu.make_async_copy(src, dst, sem)

    def issue(t, carry):
        for k in range(TOP_K):
            row_copy(t, k).start(priority=k % 2)
        return carry

    def drain(t, carry):
        for k in range(TOP_K):
            row_copy(t, k).wait()
        return carry

    lax.fori_loop(0, tb, issue, 0)
    shared = _swiglu_packed(h2p_ref, wsgu_ref, wsd_ref, tb)[0]
    lax.fori_loop(0, tb, drain, 0)

    wts = wts_ref[...]
    wb = [jnp.broadcast_to(wts[:, k:k + 1], (tb, LANES)) for k in range(TOP_K)]
    cols = []
    for s in range(PACK_ROWS):
        c = 2 * LANES * s
        lo, hi = shared[:, c:c + LANES], shared[:, c + LANES:c + 2 * LANES]
        for k in range(TOP_K):
            w = ybuf[pl.ds(k * tb * PACK_ROWS + s, tb, stride=PACK_ROWS), :]
            lo = lo + wb[k] * lax.bitcast_convert_type(lax.shift_left(w, jnp.int32(16)), F32)
            hi = hi + wb[k] * lax.bitcast_convert_type(w & jnp.int32(-65536), F32)
        cols += [lo, hi]
    moe = jnp.concatenate(cols, axis=1)
    out_ref[...] = _rms(x1_ref[...] + gf_ref[...] * moe, fg_ref[...])


def _combine(dest, wts_t, h2p, x1, gate_f, final_g, w_sgu, w_sd, y):
    n, d = x1.shape
    tb = min(TB_COMBINE, n)
    row = lambda i: (i, 0)
    vec = pl.BlockSpec((1, d), lambda i: (0, 0))
    return pl.pallas_call(
        functools.partial(_combine_body, tb=tb),
        grid=(n // tb,),
        in_specs=[pl.BlockSpec((tb * TOP_K,), lambda i: (i,), memory_space=pltpu.SMEM),
                  pl.BlockSpec((tb, TOP_K), row),
                  pl.BlockSpec((tb * PACK_ROWS, LANES), row),
                  pl.BlockSpec((tb, d), row), vec, vec,
                  _resident(w_sgu.shape), _resident(w_sd.shape),
                  pl.BlockSpec(memory_space=pl.ANY)],
        out_specs=pl.BlockSpec((tb, d), row),
        out_shape=jax.ShapeDtypeStruct((n, d), F32),
        scratch_shapes=[pltpu.VMEM((TOP_K * tb * PACK_ROWS, LANES), I32), pltpu.SemaphoreType.DMA(())],
        compiler_params=_params("arbitrary"),
        name="combine",
    )(dest, wts_t, h2p, x1, gate_f, final_g, w_sgu, w_sd, y)


def _rope_tables(pos):
    half = QK_ROPE_DIM // 2
    inv_freq = ROPE_THETA ** (-jnp.arange(half, dtype=F32) / half)
    ang = pos.astype(F32)[:, None] * inv_freq
    cos, sin = jnp.cos(ang), jnp.sin(ang)
    z = jnp.zeros_like(cos)
    c = jnp.concatenate([cos, cos, z, z], axis=1)
    s1 = jnp.concatenate([z, sin, z, z], axis=1)
    s2 = jnp.concatenate([-sin, z, z, z], axis=1)
    return c, s1, s2


def _layer(x, c, pos, norm_attn_g, w_ada, b_ada, w_in, g_q, w_uq, g_kv, w_ukv, g_out_swa, g_out_mla, w_o,
           norm_ffn_g, w_router, router_bias, w_exp_gate, w_exp_up, w_exp_down, w_sh_gate, w_sh_up, w_sh_down,
           final_g):
    s, d = x.shape
    row = lambda a: a.reshape(1, -1)

    mod = _ada(c, w_ada, b_ada)
    shift_a, scale_a, gate_a, shift_f, scale_f, gate_f = [mod[:, i * d:(i + 1) * d] for i in range(N_ADA)]

    n_qkv = 3 * D_SWA
    w_qkv = w_in[:, :n_qkv].astype(BF16)
    w_rest = jnp.pad(w_in[:, n_qkv:], ((0, 0), (0, LANES - QK_ROPE_DIM))).astype(BF16)
    dils = tuple(dil for _, dil in SWA_PATTERNS)
    assert all(window // dil == SWA_BLOCK and s % (dil * SWA_BLOCK) == 0 for window, dil in SWA_PATTERNS)
    rest, qkv_views = _inproj(x, row(norm_attn_g), scale_a, shift_a, w_qkv, w_rest, dils)

    dq = QK_NOPE_DIM + QK_ROPE_DIM
    w_uq_p = jnp.pad(w_uq.reshape(Q_LORA_RANK, N_HEADS_MLA, dq), ((0, 0), (0, 0), (0, MLA_QK_PAD - dq)))
    w_uq_p = w_uq_p.reshape(Q_LORA_RANK, N_HEADS_MLA * MLA_QK_PAD).astype(BF16)
    rc, rs1, rs2 = _rope_tables(pos)
    q_m, k_m, v_m = _mlaproj(rest, row(g_q), row(g_kv), w_uq_p, w_ukv.astype(BF16), rc, rs1, rs2)
    o_b = _mla(q_m, k_m, v_m)

    posf = pos.astype(F32)
    o_pats, lse_pats = zip(*[_dilated(qkv_v, posf, dil) for qkv_v, dil in zip(qkv_views, dils)])

    x1, h2p, logits_t = _outproj(o_pats, lse_pats, dils, o_b, x, row(g_out_swa), row(g_out_mla),
                                 w_o.astype(BF16), gate_a, row(norm_ffn_g), scale_f, shift_f, w_router.T)

    eidx, wts, rank, cnt = _route(logits_t, router_bias)
    tr = TR_EXPERT
    counts = cnt[:, 0].astype(I32)
    padded = (counts + tr - 1) // tr * tr
    e_ids = jnp.arange(N_EXPERTS, dtype=I32)
    pad_end = jnp.sum(jnp.where(e_ids[None, :] <= e_ids[:, None], padded[None, :], 0), axis=1)
    pad_start = pad_end - padded
    lookup = lambda table, idx: jnp.sum(jnp.where(idx[..., None] == e_ids, table, 0), axis=-1)
    dest = (lookup(pad_start, eidx) + rank).T.reshape(-1)
    n_slots = s * TOP_K + N_EXPERTS * tr
    blk_start = jnp.arange(n_slots // tr, dtype=I32) * tr
    blk_e = jnp.minimum(jnp.sum((pad_end[None, :] <= blk_start[:, None]).astype(I32), axis=1), N_EXPERTS - 1)
    blk_nv = jnp.clip(lookup(counts, blk_e) - (blk_start - lookup(pad_start, blk_e)), 0, tr)
    blk_new = ((blk_nv > 0) & (blk_start == lookup(pad_start, blk_e))).astype(I32)

    xs = _dispatch(dest, h2p, pad_start + counts, padded - counts, n_slots, tr)
    y = _experts(blk_e, blk_nv, blk_new, xs, w_exp_gate, w_exp_up, w_exp_down)
    w_sgu = jnp.concatenate([w_sh_gate, w_sh_up], axis=1).astype(BF16)
    return _combine(dest, wts.T, h2p, x1, gate_f, row(final_g), w_sgu, w_sh_down.astype(BF16), y)


def kernel(x, c, positions, norm_attn_g, w_ada, b_ada, w_in, g_q, w_uq, g_kv, w_ukv, g_out_swa, g_out_mla, w_o,
           norm_ffn_g, w_router, router_bias, w_exp_gate, w_exp_up, w_exp_down, w_sh_gate, w_sh_up, w_sh_down,
           final_norm_g):
    assert x.shape[0] == 1 and w_ada.shape[0] == 1
    out = _layer(x[0], c[0], positions[0], norm_attn_g[0], w_ada[0], b_ada[0], w_in[0], g_q[0], w_uq[0], g_kv[0],
                 w_ukv[0], g_out_swa[0], g_out_mla[0], w_o[0], norm_ffn_g[0], w_router[0], router_bias[0],
                 w_exp_gate[0], w_exp_up[0], w_exp_down[0], w_sh_gate[0], w_sh_up[0], w_sh_down[0], final_norm_g)
    return out[None]
```

```python
import functools

import jax
import jax.numpy as jnp
from jax import lax
from jax.experimental import pallas as pl
from jax.experimental.pallas import tpu as pltpu

F32 = jnp.float32
BF16 = jnp.bfloat16
I32 = jnp.int32

D_MODEL = 2048
N_HEADS_SWA = 8
HEAD_DIM_SWA = 128
SWA_PATTERNS = ((128, 1), (512, 4), (2048, 16))
SWA_BLOCK = 128
N_HEADS_MLA = 8
Q_LORA_RANK = 512
KV_LORA_RANK = 256
QK_NOPE_DIM = 128
QK_ROPE_DIM = 64
V_HEAD_DIM = 128
ROPE_THETA = 10000.0
D_SWA = N_HEADS_SWA * HEAD_DIM_SWA
D_MLA = N_HEADS_MLA * V_HEAD_DIM
N_EXPERTS = 64
N_GROUPS = 8
TOPK_GROUPS = 4
TOP_K = 8
D_EXPERT = 512
ROUTED_SCALE = 2.5
N_ADA = 6
EPS = 1e-6
NEG_INF = -1e30
LOG2E = 1.4426950408889634

LANES = 128
MLA_QK_PAD = 256
PACK_ROWS = D_MODEL // (2 * LANES)
VMEM_LIMIT = 56 * 1024 * 1024

TM_INPROJ = 256
TM_MLAPROJ = 512
T_MLA = 1024
TM_OUTPROJ = 256
OUTPROJ_SUBTILES = 2
TN_ROUTE = 512
TB_DISPATCH = 256
TR_EXPERT = 512
EXPERT_SUBTILES = 2
TB_COMBINE = 256


def _params(*sem):
    return pltpu.CompilerParams(dimension_semantics=sem, vmem_limit_bytes=VMEM_LIMIT)


def _rms(x, g):
    return x * lax.rsqrt(jnp.mean(x * x, axis=-1, keepdims=True) + EPS) * g


def _resident(shape):
    nd = len(shape)
    return pl.BlockSpec(shape, lambda *_: (0,) * nd, pipeline_mode=pl.Buffered(1))


def _pack_bf16_pairs(a, b):
    ua = lax.bitcast_convert_type(a.astype(BF16).astype(F32), I32)
    ub = lax.bitcast_convert_type(b.astype(BF16).astype(F32), I32)
    return lax.shift_right_logical(ua, jnp.int32(16)) | (ub & jnp.int32(-65536))


def _unpack_bf16_pairs(w):
    lo = lax.bitcast_convert_type(lax.shift_left(w, jnp.int32(16)), F32).astype(BF16)
    hi = lax.bitcast_convert_type(w & jnp.int32(-65536), F32).astype(BF16)
    return jnp.concatenate([lo, hi], axis=1)


def _packed_chunk(ref, s, rows):
    return _unpack_bf16_pairs(ref[pl.ds(s, rows, stride=PACK_ROWS), :])


def _ada_body(c_ref, w_ref, b_ref, o_ref):
    c = c_ref[...]
    a = c * jax.nn.sigmoid(c)
    o_ref[...] = jnp.sum(w_ref[...] * a, axis=0, keepdims=True) + b_ref[...]


def _ada(c, w_ada, b_ada):
    d, n = w_ada.shape
    tn = 512
    return pl.pallas_call(
        _ada_body,
        grid=(n // tn,),
        in_specs=[pl.BlockSpec((d, 1), lambda j: (0, 0)),
                  pl.BlockSpec((d, tn), lambda j: (0, j)),
                  pl.BlockSpec((1, tn), lambda j: (0, j))],
        out_specs=pl.BlockSpec((1, tn), lambda j: (0, j)),
        out_shape=jax.ShapeDtypeStruct((1, n), F32),
        compiler_params=_params("parallel"),
        name="ada",
    )(c.reshape(d, 1), w_ada, b_ada.reshape(1, n))


def _inproj_body(x_ref, g_ref, sc_ref, sh_ref, wqkv_ref, wr_ref, rest_ref, *out_and_scratch, tm, dils):
    view_refs, res_ref = out_and_scratch[:-1], out_and_scratch[-1]
    n = wqkv_ref.shape[1]
    h = (_rms(x_ref[...], g_ref[...]) * (1.0 + sc_ref[...]) + sh_ref[...]).astype(BF16)
    rest_ref[...] = jnp.dot(h, wr_ref[...], preferred_element_type=F32)
    res = jnp.dot(h, wqkv_ref[...], preferred_element_type=F32)
    chunks = range(n // LANES)
    for c in chunks:
        res_ref[c] = res[:, c * LANES:(c + 1) * LANES]
    for dil, v_ref in zip(dils, view_refs):
        if dil == 1:
            v_ref[...] = res.astype(BF16)
            continue
        for r in range(dil):
            for c in chunks:
                b = r * n + c * LANES
                v_ref[:, b:b + LANES] = res_ref[c, pl.ds(r, tm // dil, stride=dil), :].astype(BF16)


def _inproj(x, g, scale, shift, w_qkv, w_rest, dils):
    s, d = x.shape
    tm = min(TM_INPROJ, s)
    n1, n2 = w_qkv.shape[1], w_rest.shape[1]
    row = lambda i: (i, 0)
    vec = pl.BlockSpec((1, d), lambda i: (0, 0))
    outs = pl.pallas_call(
        functools.partial(_inproj_body, tm=tm, dils=dils),
        grid=(s // tm,),
        in_specs=[pl.BlockSpec((tm, d), row), vec, vec, vec, _resident((d, n1)), _resident((d, n2))],
        out_specs=[pl.BlockSpec((tm, n2), row)] + [pl.BlockSpec((tm // dil, dil * n1), row) for dil in dils],
        out_shape=[jax.ShapeDtypeStruct((s, n2), F32)]
        + [jax.ShapeDtypeStruct((s // dil, dil * n1), BF16) for dil in dils],
        scratch_shapes=[pltpu.VMEM((n1 // LANES, tm, LANES), F32)],
        compiler_params=_params("parallel"),
        name="inproj",
    )(x, g, scale, shift, w_qkv, w_rest)
    return outs[0], outs[1:]


def _rope_tail(t, c, s1, s2):
    return t * c + pltpu.roll(t, 32, 1) * s1 + pltpu.roll(t, 96, 1) * s2


def _mlaproj_body(rest_ref, gq_ref, gkv_ref, wuq_ref, wukv_ref, c_ref, s1_ref, s2_ref,
                  q_ref, k_ref, v_ref, *, scale):
    rest = rest_ref[...]
    c, s1, s2 = c_ref[...], s1_ref[...], s2_ref[...]
    cq = _rms(rest[:, :Q_LORA_RANK], gq_ref[...]).astype(BF16)
    ckv = _rms(rest[:, Q_LORA_RANK:Q_LORA_RANK + KV_LORA_RANK], gkv_ref[...]).astype(BF16)
    ktail = _rope_tail(rest[:, Q_LORA_RANK + KV_LORA_RANK:], c, s1, s2).astype(BF16)
    q = jnp.dot(cq, wuq_ref[...], preferred_element_type=F32)
    kv = jnp.dot(ckv, wukv_ref[...], preferred_element_type=F32)
    for h in range(N_HEADS_MLA):
        b = h * MLA_QK_PAD
        q_ref[:, b:b + LANES] = (q[:, b:b + LANES] * scale).astype(BF16)
        q_ref[:, b + LANES:b + 2 * LANES] = (_rope_tail(q[:, b + LANES:b + 2 * LANES], c, s1, s2) * scale).astype(BF16)
        k_ref[:, b:b + LANES] = kv[:, b:b + LANES].astype(BF16)
        k_ref[:, b + LANES:b + 2 * LANES] = ktail
        v_ref[:, b:b + LANES] = kv[:, b + LANES:b + 2 * LANES].astype(BF16)
        v_ref[:, b + LANES:b + 2 * LANES] = jnp.ones((q.shape[0], LANES), BF16)


def _mlaproj(rest, g_q, g_kv, w_uq, w_ukv, rc, rs1, rs2):
    s, nr = rest.shape
    tm = min(TM_MLAPROJ, s)
    nq = N_HEADS_MLA * MLA_QK_PAD
    row = lambda i: (i, 0)
    tab = pl.BlockSpec((tm, LANES), row)
    scale = float(QK_NOPE_DIM + QK_ROPE_DIM) ** -0.5 * LOG2E
    return pl.pallas_call(
        functools.partial(_mlaproj_body, scale=scale),
        grid=(s // tm,),
        in_specs=[pl.BlockSpec((tm, nr), row),
                  pl.BlockSpec((1, Q_LORA_RANK), lambda i: (0, 0)),
                  pl.BlockSpec((1, KV_LORA_RANK), lambda i: (0, 0)),
                  _resident(w_uq.shape), _resident(w_ukv.shape), tab, tab, tab],
        out_specs=[pl.BlockSpec((tm, nq), row)] * 3,
        out_shape=[jax.ShapeDtypeStruct((s, nq), BF16)] * 3,
        compiler_params=_params("parallel"),
        name="mlaproj",
    )(rest, g_q, g_kv, w_uq, w_ukv, rc, rs1, rs2)


def _mla_body(q_ref, k_ref, v_ref, o_ref, m_ref, acc_ref, sa_ref, sb_ref, *, t):
    qi = pl.program_id(1)
    q = q_ref[...]
    m_ref[...] = jnp.full(m_ref.shape, NEG_INF, F32)
    acc_ref[...] = jnp.zeros(acc_ref.shape, F32)

    def scores(j, dst):
        k = k_ref[pl.ds(pl.multiple_of(j * t, t), t), :]
        dst[...] = lax.dot_general(q, k, (((1,), (1,)), ((), ())), preferred_element_type=F32)

    def absorb(j, src, masked):
        s = src[...]
        if masked:
            r = lax.broadcasted_iota(I32, (t, t), 0)
            cidx = lax.broadcasted_iota(I32, (t, t), 1)
            s = jnp.where(cidx <= r, s, NEG_INF)
        v = v_ref[pl.ds(pl.multiple_of(j * t, t), t), :]
        m_old = m_ref[...]
        m_new = jnp.maximum(m_old, jnp.max(s, axis=-1, keepdims=True))
        p = jnp.exp2(s - m_new).astype(BF16)
        acc_ref[...] = jnp.exp2(m_old - m_new) * acc_ref[...] + jnp.dot(p, v, preferred_element_type=F32)
        m_ref[...] = m_new

    scores(0, sa_ref)

    def pair(i, carry):
        j = 2 * i
        scores(j + 1, sb_ref)
        absorb(j, sa_ref, False)
        scores(j + 2, sa_ref)
        absorb(j + 1, sb_ref, False)
        return carry

    lax.fori_loop(0, qi // 2, pair, 0)

    @pl.when(qi % 2 == 0)
    def _():
        absorb(qi, sa_ref, True)

    @pl.when(qi % 2 == 1)
    def _():
        scores(qi, sb_ref)
        absorb(qi - 1, sa_ref, False)
        absorb(qi, sb_ref, True)

    acc = acc_ref[...]
    o_ref[...] = acc[:, :V_HEAD_DIM] / acc[:, V_HEAD_DIM:]


def _mla(q, k, v):
    s = q.shape[0]
    t = min(T_MLA, s)
    head_cols = lambda h, i: (0, h)
    return pl.pallas_call(
        functools.partial(_mla_body, t=t),
        grid=(N_HEADS_MLA, s // t),
        in_specs=[pl.BlockSpec((t, MLA_QK_PAD), lambda h, i: (i, h)),
                  pl.BlockSpec((s, MLA_QK_PAD), head_cols, pipeline_mode=pl.Buffered(1)),
                  pl.BlockSpec((s, MLA_QK_PAD), head_cols, pipeline_mode=pl.Buffered(1))],
        out_specs=pl.BlockSpec((t, V_HEAD_DIM), lambda h, i: (i, h)),
        out_shape=jax.ShapeDtypeStruct((s, D_MLA), F32),
        scratch_shapes=[pltpu.VMEM((t, 1), F32), pltpu.VMEM((t, MLA_QK_PAD), F32),
                        pltpu.VMEM((t, t), F32), pltpu.VMEM((t, t), F32)],
        compiler_params=_params("parallel", "arbitrary"),
        name="mla",
    )(q, k, v)


def _dilated_body(q_ref, kc_ref, kp_ref, vc_ref, vp_ref, pq_ref, pkc_ref, pkp_ref, o_ref, lse_ref):
    n = pl.program_id(1)
    blk = SWA_BLOCK
    i = lax.broadcasted_iota(I32, (blk, blk), 0)
    j = lax.broadcasted_iota(I32, (blk, blk), 1)
    ok_cur = j <= i
    ok_prev = (j >= i) & (n > 0)
    pq = pq_ref[...]
    dist_cur = jnp.abs(pq - pkc_ref[0])
    dist_prev = jnp.abs(pq - pkp_ref[0])
    scale = float(HEAD_DIM_SWA) ** -0.5
    nt = (((1,), (1,)), ((), ()))
    heads = range(N_HEADS_SWA)
    hs = [slice(h * HEAD_DIM_SWA, (h + 1) * HEAD_DIM_SWA) for h in heads]
    slope = [2.0 ** (-8.0 * (h + 1) / N_HEADS_SWA) for h in heads]
    ones = jnp.ones((blk, HEAD_DIM_SWA), BF16)
    sc = [jnp.where(ok_cur, lax.dot_general(q_ref[:, hs[h]], kc_ref[:, hs[h]], nt, preferred_element_type=F32)
                    * scale - slope[h] * dist_cur, NEG_INF) for h in heads]
    sp = [jnp.where(ok_prev, lax.dot_general(q_ref[:, hs[h]], kp_ref[:, hs[h]], nt, preferred_element_type=F32)
                    * scale - slope[h] * dist_prev, NEG_INF) for h in heads]
    m = [jnp.max(jnp.maximum(sc[h], sp[h]), axis=-1, keepdims=True) for h in heads]
    pc = [jnp.exp(sc[h] - m[h]).astype(BF16) for h in heads]
    pp = [jnp.exp(sp[h] - m[h]).astype(BF16) for h in heads]
    acc = [jnp.dot(pc[h], jnp.concatenate([vc_ref[:, hs[h]], ones], axis=1), preferred_element_type=F32)
           + jnp.dot(pp[h], jnp.concatenate([vp_ref[:, hs[h]], ones], axis=1), preferred_element_type=F32)
           for h in heads]
    for h in heads:
        den = acc[h][:, HEAD_DIM_SWA:]
        o_ref[:, hs[h]] = acc[h][:, :HEAD_DIM_SWA] / den
        lse_ref[:, hs[h]] = m[h] + jnp.log(den)


def _dilated(qkv_v, posf, dil):
    sd = qkv_v.shape[0]
    nb = sd // SWA_BLOCK
    blk = SWA_BLOCK
    pos_v = posf.reshape(sd, dil)
    pq_v = jnp.repeat(pos_v, LANES, axis=1)
    pos_rows = pos_v.T.reshape(dil, 1, sd)
    prev = lambda n: jnp.maximum(n - 1, 0)
    wide = (blk, D_SWA)
    return pl.pallas_call(
        _dilated_body,
        grid=(dil, nb),
        in_specs=[pl.BlockSpec(wide, lambda r, n: (n, 3 * r)),
                  pl.BlockSpec(wide, lambda r, n: (n, 3 * r + 1)),
                  pl.BlockSpec(wide, lambda r, n: (prev(n), 3 * r + 1)),
                  pl.BlockSpec(wide, lambda r, n: (n, 3 * r + 2)),
                  pl.BlockSpec(wide, lambda r, n: (prev(n), 3 * r + 2)),
                  pl.BlockSpec((blk, LANES), lambda r, n: (n, r)),
                  pl.BlockSpec((1, 1, blk), lambda r, n: (r, 0, n)),
                  pl.BlockSpec((1, 1, blk), lambda r, n: (r, 0, prev(n)))],
        out_specs=[pl.BlockSpec(wide, lambda r, n: (n, r)), pl.BlockSpec(wide, lambda r, n: (n, r))],
        out_shape=[jax.ShapeDtypeStruct((sd, dil * D_SWA), F32), jax.ShapeDtypeStruct((sd, dil * D_SWA), F32)],
        compiler_params=_params("parallel", "parallel"),
        name=f"dil{dil}",
    )(qkv_v, qkv_v, qkv_v, qkv_v, qkv_v, pq_v, pos_rows, pos_rows)


def _outproj_body(*refs, tm, dils):
    npat = len(dils)
    o_views, l_views = refs[:npat], refs[npat:2 * npat]
    (ob_ref, x_ref, gsw_ref, gml_ref, wo_ref, ga_ref, nfg_ref, scf_ref, shf_ref, wrt_ref,
     x1_ref, h2p_ref, lgt_ref) = refs[2 * npat:2 * npat + 13]
    scratch = list(refs[2 * npat + 13:])

    chunks = range(D_SWA // LANES)

    def token_order(view_ref, dil):
        if dil == 1:
            return lambda rs: view_ref[rs, :]
        nat_ref = scratch.pop(0)
        for r in range(dil):
            for c in chunks:
                b = r * D_SWA + c * LANES
                nat_ref[c, pl.ds(r, tm // dil, stride=dil), :] = view_ref[:, b:b + LANES]
        return lambda rs: jnp.concatenate([nat_ref[c, rs, :] for c in chunks], axis=1)

    o1, o2, o3 = [token_order(v, dil) for v, dil in zip(o_views, dils)]
    l1f, l2f, l3f = [token_order(v, dil) for v, dil in zip(l_views, dils)]

    nsub = OUTPROJ_SUBTILES
    r = tm // nsub
    subs = range(nsub)
    rows = [slice(i * r, (i + 1) * r) for i in subs]

    def merged(rs):
        l1, l2, l3 = l1f(rs), l2f(rs), l3f(rs)
        m = jnp.maximum(jnp.maximum(l1, l2), l3)
        e1, e2, e3 = jnp.exp(l1 - m), jnp.exp(l2 - m), jnp.exp(l3 - m)
        return (e1 * o1(rs) + e2 * o2(rs) + e3 * o3(rs)) / (e1 + e2 + e3)

    mix = [jnp.concatenate([_rms(merged(rs), gsw_ref[...]), _rms(ob_ref[rs, :], gml_ref[...])],
                           axis=-1).astype(BF16) for rs in rows]
    proj = [jnp.dot(mix[i], wo_ref[...], preferred_element_type=F32) for i in subs]
    x1 = [x_ref[rows[i], :] + ga_ref[...] * proj[i] for i in subs]
    h2 = [_rms(x1[i], nfg_ref[...]) * (1.0 + scf_ref[...]) + shf_ref[...] for i in subs]
    for i in subs:
        x1_ref[rows[i], :] = x1[i]
        lgt_ref[:, rows[i]] = lax.dot_general(wrt_ref[...], h2[i], (((1,), (1,)), ((), ())),
                                              precision=lax.Precision.HIGHEST, preferred_element_type=F32)
    for i in subs:
        for s in range(PACK_ROWS):
            b = 2 * LANES * s
            h2p_ref[pl.ds(i * r * PACK_ROWS + s, r, stride=PACK_ROWS), :] = _pack_bf16_pairs(
                h2[i][:, b:b + LANES], h2[i][:, b + LANES:b + 2 * LANES])


def _outproj(o_pats, lse_pats, dils, o_b, x, g_sw, g_ml, w_o, gate_a, nfg, scale_f, shift_f, w_router_t):
    s, d = x.shape
    tm = min(TM_OUTPROJ, s)
    row = lambda i: (i, 0)
    views = [pl.BlockSpec((tm // dil, dil * D_SWA), row) for dil in dils]
    vec = lambda n: pl.BlockSpec((1, n), lambda i: (0, 0))
    n_reordered = 2 * sum(1 for dil in dils if dil > 1)
    return pl.pallas_call(
        functools.partial(_outproj_body, tm=tm, dils=dils),
        grid=(s // tm,),
        in_specs=views + views + [pl.BlockSpec((tm, D_MLA), row), pl.BlockSpec((tm, d), row), vec(D_SWA),
                                  vec(D_MLA), _resident(w_o.shape), vec(d), vec(d), vec(d), vec(d),
                                  _resident(w_router_t.shape)],
        out_specs=[pl.BlockSpec((tm, d), row), pl.BlockSpec((tm * PACK_ROWS, LANES), row),
                   pl.BlockSpec((N_EXPERTS, tm), lambda i: (0, i))],
        out_shape=[jax.ShapeDtypeStruct((s, d), F32), jax.ShapeDtypeStruct((s * PACK_ROWS, LANES), I32),
                   jax.ShapeDtypeStruct((N_EXPERTS, s), F32)],
        scratch_shapes=[pltpu.VMEM((D_SWA // LANES, tm, LANES), F32)] * n_reordered,
        compiler_params=_params("parallel"),
        name="outproj",
    )(*o_pats, *lse_pats, o_b, x, g_sw, g_ml, w_o, gate_a, nfg, scale_f, shift_f, w_router_t)


def _first_index(hit_value, x, iota, size, axis):
    return jnp.min(jnp.where(x == hit_value, iota, size), axis=axis, keepdims=True)


def _route_body(lgt_ref, bias_ref, tri_ref, eidx_ref, wts_ref, rank_ref, cnt_ref, carry_ref, *, tn):
    @pl.when(pl.program_id(0) == 0)
    def _():
        carry_ref[...] = jnp.zeros(carry_ref.shape, F32)

    gsz = N_EXPERTS // N_GROUPS
    scores = jax.nn.sigmoid(lgt_ref[...])
    choice = scores + bias_ref[...]
    neg = jnp.float32(-jnp.inf)

    g3 = choice.reshape(N_GROUPS, gsz, tn)
    i3 = lax.broadcasted_iota(I32, g3.shape, 1)
    m1 = jnp.max(g3, axis=1, keepdims=True)
    f1 = _first_index(m1, g3, i3, gsz, 1)
    m2 = jnp.max(jnp.where(i3 == f1, neg, g3), axis=1, keepdims=True)
    gs = (m1 + m2).reshape(N_GROUPS, tn)

    ig = lax.broadcasted_iota(I32, gs.shape, 0)
    gsel = jnp.zeros(gs.shape, F32)
    for _ in range(TOPK_GROUPS):
        hit = ig == _first_index(jnp.max(gs, axis=0, keepdims=True), gs, ig, N_GROUPS, 0)
        gsel = jnp.where(hit, 1.0, gsel)
        gs = jnp.where(hit, neg, gs)
    emask = jnp.broadcast_to(gsel.reshape(N_GROUPS, 1, tn), (N_GROUPS, gsz, tn)).reshape(N_EXPERTS, tn)
    cand = jnp.where(emask > 0.0, choice, NEG_INF)

    ie = lax.broadcasted_iota(I32, cand.shape, 0)
    picks, wsel = [], []
    onehot = jnp.zeros(cand.shape, F32)
    for _ in range(TOP_K):
        f = _first_index(jnp.max(cand, axis=0, keepdims=True), cand, ie, N_EXPERTS, 0)
        hit = ie == f
        picks.append(f)
        wsel.append(jnp.sum(jnp.where(hit, scores, 0.0), axis=0, keepdims=True))
        onehot = jnp.where(hit, 1.0, onehot)
        cand = jnp.where(hit, neg, cand)

    rank = carry_ref[...] + jnp.dot(onehot.astype(BF16), tri_ref[...], preferred_element_type=F32)
    carry_ref[...] = carry_ref[...] + jnp.sum(onehot, axis=1, keepdims=True)
    cnt_ref[...] = carry_ref[...]

    w = jnp.concatenate(wsel, axis=0)
    wts_ref[...] = w / jnp.sum(w, axis=0, keepdims=True) * ROUTED_SCALE
    eidx_ref[...] = jnp.concatenate(picks, axis=0)
    rank_ref[...] = jnp.concatenate(
        [jnp.sum(jnp.where(ie == f, rank, 0.0), axis=0, keepdims=True) for f in picks], axis=0).astype(I32)


def _route(logits_t, router_bias):
    e, n = logits_t.shape
    tn = min(TN_ROUTE, n)
    tri = (lax.broadcasted_iota(I32, (tn, tn), 0) < lax.broadcasted_iota(I32, (tn, tn), 1)).astype(BF16)
    col = lambda i: (0, i)
    return pl.pallas_call(
        functools.partial(_route_body, tn=tn),
        grid=(n // tn,),
        in_specs=[pl.BlockSpec((e, tn), col), pl.BlockSpec((e, 1), lambda i: (0, 0)),
                  pl.BlockSpec((tn, tn), lambda i: (0, 0))],
        out_specs=[pl.BlockSpec((TOP_K, tn), col), pl.BlockSpec((TOP_K, tn), col), pl.BlockSpec((TOP_K, tn), col),
                   pl.BlockSpec((e, 1), lambda i: (0, 0))],
        out_shape=[jax.ShapeDtypeStruct((TOP_K, n), I32), jax.ShapeDtypeStruct((TOP_K, n), F32),
                   jax.ShapeDtypeStruct((TOP_K, n), I32), jax.ShapeDtypeStruct((e, 1), F32)],
        scratch_shapes=[pltpu.VMEM((e, 1), F32)],
        compiler_params=_params("arbitrary"),
        name="route",
    )(logits_t, router_bias.reshape(e, 1), tri)


def _dispatch_body(tail_start_ref, tail_len_ref, dest_ref, h_ref, xs_ref, zeros_ref, sem, zsem, *, tb, tr):
    sizes = [1 << b for b in reversed(range((tr - 1).bit_length()))]

    def tail_copies(e, act):
        start, length = tail_start_ref[e], tail_len_ref[e]
        done = jnp.int32(0)
        for size in sizes:
            piece = (length & size) != 0
            dst = xs_ref.at[pl.ds(pl.multiple_of((start + done) * PACK_ROWS, PACK_ROWS), size * PACK_ROWS), :]
            copy = pltpu.make_async_copy(zeros_ref.at[pl.ds(0, size * PACK_ROWS), :], dst, zsem)
            pl.when(piece)(functools.partial(act, copy))
            done = done + (length & size)

    @pl.when(pl.program_id(0) == 0)
    def _():
        zeros_ref[...] = jnp.zeros(zeros_ref.shape, I32)

        def start_tail(e, carry):
            tail_copies(e, lambda copy: copy.start())
            return carry

        def wait_tail(e, carry):
            tail_copies(e, lambda copy: copy.wait())
            return carry

        lax.fori_loop(0, N_EXPERTS, start_tail, 0)
        lax.fori_loop(0, N_EXPERTS, wait_tail, 0)

    def row_copy(t, k):
        src = h_ref.at[pl.ds(pl.multiple_of(t * PACK_ROWS, PACK_ROWS), PACK_ROWS), :]
        dst = xs_ref.at[pl.ds(pl.multiple_of(dest_ref[t * TOP_K + k] * PACK_ROWS, PACK_ROWS), PACK_ROWS), :]
        return pltpu.make_async_copy(src, dst, sem)

    def issue(t, carry):
        for k in range(TOP_K):
            row_copy(t, k).start(priority=k % 2)
        return carry

    def drain(t, carry):
        for k in range(TOP_K):
            row_copy(t, k).wait()
        return carry

    lax.fori_loop(0, tb, issue, 0)
    lax.fori_loop(0, tb, drain, 0)


def _dispatch(dest, h2p, tail_start, tail_len, n_slots, tr):
    n = dest.shape[0] // TOP_K
    tb = min(TB_DISPATCH, n)
    max_piece = 1 << ((tr - 1).bit_length() - 1)
    grid_spec = pltpu.PrefetchScalarGridSpec(
        num_scalar_prefetch=2,
        grid=(n // tb,),
        in_specs=[pl.BlockSpec((tb * TOP_K,), lambda i, ts, tl: (i,), memory_space=pltpu.SMEM),
                  pl.BlockSpec((tb * PACK_ROWS, LANES), lambda i, ts, tl: (i, 0))],
        out_specs=pl.BlockSpec(memory_space=pl.ANY),
        scratch_shapes=[pltpu.VMEM((max_piece * PACK_ROWS, LANES), I32),
                        pltpu.SemaphoreType.DMA(()), pltpu.SemaphoreType.DMA(())],
    )
    return pl.pallas_call(
        functools.partial(_dispatch_body, tb=tb, tr=tr),
        grid_spec=grid_spec,
        out_shape=jax.ShapeDtypeStruct((n_slots * PACK_ROWS, LANES), I32),
        compiler_params=_params("arbitrary"),
        name="dispatch",
    )(tail_start, tail_len, dest, h2p)


def _swiglu_packed(xp_ref, wgu_ref, wd_ref, rows, nsub=1):
    r = rows // nsub
    subs = range(nsub)
    x = [jnp.concatenate([_unpack_bf16_pairs(xp_ref[pl.ds(i * r * PACK_ROWS + s, r, stride=PACK_ROWS), :])
                          for s in range(PACK_ROWS)], axis=1) for i in subs]
    h = [jnp.dot(x[i], wgu_ref[...], preferred_element_type=F32) for i in subs]
    a = [(h[i][:, :D_EXPERT] * jax.nn.sigmoid(h[i][:, :D_EXPERT]) * h[i][:, D_EXPERT:]).astype(BF16) for i in subs]
    return [jnp.dot(a[i], wd_ref[...], preferred_element_type=F32) for i in subs]


def _experts_body(be_ref, nv_ref, new_ref, xs_ref, wg_ref, wu_ref, wd_ref, y_ref, wgu_s, wd_s, *, tr):
    del be_ref
    b = pl.program_id(0)

    @pl.when(new_ref[b] > 0)
    def _():
        wgu_s[:, :D_EXPERT] = wg_ref[0].astype(BF16)
        wgu_s[:, D_EXPERT:] = wu_ref[0].astype(BF16)
        wd_s[...] = wd_ref[0].astype(BF16)

    @pl.when(nv_ref[b] > 0)
    def _():
        nsub = EXPERT_SUBTILES
        r = tr // nsub
        ys = _swiglu_packed(xs_ref, wgu_s, wd_s, tr, nsub)
        for i, y in enumerate(ys):
            for s in range(PACK_ROWS):
                c = 2 * LANES * s
                y_ref[pl.ds(i * r * PACK_ROWS + s, r, stride=PACK_ROWS), :] = _pack_bf16_pairs(
                    y[:, c:c + LANES], y[:, c + LANES:c + 2 * LANES])

    @pl.when(nv_ref[b] == 0)
    def _():
        y_ref[...] = jnp.zeros(y_ref.shape, I32)


def _experts(blk_e, blk_nv, blk_new, xs, w_gate, w_up, w_down):
    tr = TR_EXPERT
    nb = blk_e.shape[0]
    d, f = w_gate.shape[1], w_gate.shape[2]
    grid_spec = pltpu.PrefetchScalarGridSpec(
        num_scalar_prefetch=3,
        grid=(nb,),
        in_specs=[pl.BlockSpec((tr * PACK_ROWS, LANES), lambda b, be, nv, nw: (jnp.where(nv[b] > 0, b, 0), 0)),
                  pl.BlockSpec((1, d, f), lambda b, be, nv, nw: (be[b], 0, 0)),
                  pl.BlockSpec((1, d, f), lambda b, be, nv, nw: (be[b], 0, 0)),
                  pl.BlockSpec((1, f, d), lambda b, be, nv, nw: (be[b], 0, 0))],
        out_specs=pl.BlockSpec((tr * PACK_ROWS, LANES), lambda b, be, nv, nw: (b, 0)),
        scratch_shapes=[pltpu.VMEM((d, 2 * f), BF16), pltpu.VMEM((f, d), BF16)],
    )
    return pl.pallas_call(
        functools.partial(_experts_body, tr=tr),
        grid_spec=grid_spec,
        out_shape=jax.ShapeDtypeStruct((nb * tr * PACK_ROWS, LANES), I32),
        compiler_params=_params("arbitrary"),
        name="experts",
    )(blk_e, blk_nv, blk_new, xs, w_gate, w_up, w_down)


def _combine_body(dest_ref, dest_next_ref, wts_ref, h2p_ref, x1_ref, gf_ref, fg_ref, wsgu_ref, wsd_ref, y_ref,
                  out_ref, buf_a, buf_b, sem_a, sem_b, *, tb, nsteps):
    i = pl.program_id(0)
    buf_rows = TOP_K * tb * PACK_ROWS

    def row_copy(slots_ref, t, k, buf, sem):
        src = y_ref.at[pl.ds(pl.multiple_of(slots_ref[t * TOP_K + k] * PACK_ROWS, PACK_ROWS), PACK_ROWS), :]
        dst = buf.at[pl.ds(pl.multiple_of((k * tb + t) * PACK_ROWS, PACK_ROWS), PACK_ROWS), :]
        return pltpu.make_async_copy(src, dst, sem)

    def wait_block(buf, sem):
        pltpu.make_async_copy(y_ref.at[pl.ds(0, buf_rows), :], buf, sem).wait()

    @pl.when(i == 0)
    def _():
        def issue(t, carry):
            for k in range(TOP_K):
                row_copy(dest_ref, t, k, buf_a, sem_a).start(priority=k % 2)
            return carry

        lax.fori_loop(0, tb, issue, 0)

    def step(buf, sem, next_buf, next_sem):
        wait_block(buf, sem)
        for t in range(tb):
            for k in range(TOP_K):
                row_copy(dest_next_ref, t, k, next_buf, next_sem).start(priority=k % 2)
        shared = _swiglu_packed(h2p_ref, wsgu_ref, wsd_ref, tb)[0]
        wts = wts_ref[...]
        wb = [jnp.broadcast_to(wts[:, k:k + 1], (tb, LANES)) for k in range(TOP_K)]
        cols = []
        for s in range(PACK_ROWS):
            c = 2 * LANES * s
            lo, hi = shared[:, c:c + LANES], shared[:, c + LANES:c + 2 * LANES]
            for k in range(TOP_K):
                w = buf[pl.ds(k * tb * PACK_ROWS + s, tb, stride=PACK_ROWS), :]
                lo = lo + wb[k] * lax.bitcast_convert_type(lax.shift_left(w, jnp.int32(16)), F32)
                hi = hi + wb[k] * lax.bitcast_convert_type(w & jnp.int32(-65536), F32)
            cols += [lo, hi]
        moe = jnp.concatenate(cols, axis=1)
        out_ref[...] = _rms(x1_ref[...] + gf_ref[...] * moe, fg_ref[...])

        @pl.when(i == nsteps - 1)
        def _():
            wait_block(next_buf, next_sem)

    pl.when(i % 2 == 0)(functools.partial(step, buf_a, sem_a, buf_b, sem_b))
    pl.when(i % 2 == 1)(functools.partial(step, buf_b, sem_b, buf_a, sem_a))


def _combine(dest, wts_t, h2p, x1, gate_f, final_g, w_sgu, w_sd, y):
    n, d = x1.shape
    tb = min(TB_COMBINE, n)
    row = lambda i: (i, 0)
    vec = pl.BlockSpec((1, d), lambda i: (0, 0))
    nsteps = n // tb
    buf = pltpu.VMEM((TOP_K * tb * PACK_ROWS, LANES), I32)
    return pl.pallas_call(
        functools.partial(_combine_body, tb=tb, nsteps=nsteps),
        grid=(nsteps,),
        in_specs=[pl.BlockSpec((tb * TOP_K,), lambda i: (i,), memory_space=pltpu.SMEM),
                  pl.BlockSpec((tb * TOP_K,), lambda i: (jnp.minimum(i + 1, nsteps - 1),), memory_space=pltpu.SMEM),
                  pl.BlockSpec((tb, TOP_K), row),
                  pl.BlockSpec((tb * PACK_ROWS, LANES), row),
                  pl.BlockSpec((tb, d), row), vec, vec,
                  _resident(w_sgu.shape), _resident(w_sd.shape),
                  pl.BlockSpec(memory_space=pl.ANY)],
        out_specs=pl.BlockSpec((tb, d), row),
        out_shape=jax.ShapeDtypeStruct((n, d), F32),
        scratch_shapes=[buf, buf, pltpu.SemaphoreType.DMA(()), pltpu.SemaphoreType.DMA(())],
        compiler_params=_params("arbitrary"),
        name="combine",
    )(dest, dest, wts_t, h2p, x1, gate_f, final_g, w_sgu, w_sd, y)


def _rope_tables(pos):
    half = QK_ROPE_DIM // 2
    inv_freq = ROPE_THETA ** (-jnp.arange(half, dtype=F32) / half)
    ang = pos.astype(F32)[:, None] * inv_freq
    cos, sin = jnp.cos(ang), jnp.sin(ang)
    z = jnp.zeros_like(cos)
    c = jnp.concatenate([cos, cos, z, z], axis=1)
    s1 = jnp.concatenate([z, sin, z, z], axis=1)
    s2 = jnp.concatenate([-sin, z, z, z], axis=1)
    return c, s1, s2


def _layer(x, c, pos, norm_attn_g, w_ada, b_ada, w_in, g_q, w_uq, g_kv, w_ukv, g_out_swa, g_out_mla, w_o,
           norm_ffn_g, w_router, router_bias, w_exp_gate, w_exp_up, w_exp_down, w_sh_gate, w_sh_up, w_sh_down,
           final_g):
    s, d = x.shape
    row = lambda a: a.reshape(1, -1)

    mod = _ada(c, w_ada, b_ada)
    shift_a, scale_a, gate_a, shift_f, scale_f, gate_f = [mod[:, i * d:(i + 1) * d] for i in range(N_ADA)]

    n_qkv = 3 * D_SWA
    w_qkv = w_in[:, :n_qkv].astype(BF16)
    w_rest = jnp.pad(w_in[:, n_qkv:], ((0, 0), (0, LANES - QK_ROPE_DIM))).astype(BF16)
    dils = tuple(dil for _, dil in SWA_PATTERNS)
    assert all(window // dil == SWA_BLOCK and s % (dil * SWA_BLOCK) == 0 for window, dil in SWA_PATTERNS)
    rest, qkv_views = _inproj(x, row(norm_attn_g), scale_a, shift_a, w_qkv, w_rest, dils)

    dq = QK_NOPE_DIM + QK_ROPE_DIM
    w_uq_p = jnp.pad(w_uq.reshape(Q_LORA_RANK, N_HEADS_MLA, dq), ((0, 0), (0, 0), (0, MLA_QK_PAD - dq)))
    w_uq_p = w_uq_p.reshape(Q_LORA_RANK, N_HEADS_MLA * MLA_QK_PAD).astype(BF16)
    rc, rs1, rs2 = _rope_tables(pos)
    q_m, k_m, v_m = _mlaproj(rest, row(g_q), row(g_kv), w_uq_p, w_ukv.astype(BF16), rc, rs1, rs2)
    o_b = _mla(q_m, k_m, v_m)

    posf = pos.astype(F32)
    o_pats, lse_pats = zip(*[_dilated(qkv_v, posf, dil) for qkv_v, dil in zip(qkv_views, dils)])

    x1, h2p, logits_t = _outproj(o_pats, lse_pats, dils, o_b, x, row(g_out_swa), row(g_out_mla),
                                 w_o.astype(BF16), gate_a, row(norm_ffn_g), scale_f, shift_f, w_router.T)

    eidx, wts, rank, cnt = _route(logits_t, router_bias)
    tr = TR_EXPERT
    counts = cnt[:, 0].astype(I32)
    padded = (counts + tr - 1) // tr * tr
    e_ids = jnp.arange(N_EXPERTS, dtype=I32)
    pad_end = jnp.sum(jnp.where(e_ids[None, :] <= e_ids[:, None], padded[None, :], 0), axis=1)
    pad_start = pad_end - padded
    lookup = lambda table, idx: jnp.sum(jnp.where(idx[..., None] == e_ids, table, 0), axis=-1)
    dest = (lookup(pad_start, eidx) + rank).T.reshape(-1)
    n_slots = s * TOP_K + N_EXPERTS * tr
    blk_start = jnp.arange(n_slots // tr, dtype=I32) * tr
    blk_e = jnp.minimum(jnp.sum((pad_end[None, :] <= blk_start[:, None]).astype(I32), axis=1), N_EXPERTS - 1)
    blk_nv = jnp.clip(lookup(counts, blk_e) - (blk_start - lookup(pad_start, blk_e)), 0, tr)
    blk_new = ((blk_nv > 0) & (blk_start == lookup(pad_start, blk_e))).astype(I32)

    xs = _dispatch(dest, h2p, pad_start + counts, padded - counts, n_slots, tr)
    y = _experts(blk_e, blk_nv, blk_new, xs, w_exp_gate, w_exp_up, w_exp_down)
    w_sgu = jnp.concatenate([w_sh_gate, w_sh_up], axis=1).astype(BF16)
    return _combine(dest, wts.T, h2p, x1, gate_f, row(final_g), w_sgu, w_sh_down.astype(BF16), y)


def kernel(x, c, positions, norm_attn_g, w_ada, b_ada, w_in, g_q, w_uq, g_kv, w_ukv, g_out_swa, g_out_mla, w_o,
           norm_ffn_g, w_router, router_bias, w_exp_gate, w_exp_up, w_exp_down, w_sh_gate, w_sh_up, w_sh_down,
           final_norm_g):
    assert x.shape[0] == 1 and w_ada.shape[0] == 1
    out = _layer(x[0], c[0], positions[0], norm_attn_g[0], w_ada[0], b_ada[0], w_in[0], g_q[0], w_uq[0], g_kv[0],
                 w_ukv[0], g_out_swa[0], g_out_mla[0], w_o[0], norm_ffn_g[0], w_router[0], router_bias[0],
                 w_exp_gate[0], w_exp_up[0], w_exp_down[0], w_sh_gate[0], w_sh_up[0], w_sh_down[0], final_norm_g)
    return out[None]
```

```python
import functools

import jax
import jax.numpy as jnp
from jax import lax
from jax.experimental import pallas as pl
from jax.experimental.pallas import tpu as pltpu

F32 = jnp.float32
BF16 = jnp.bfloat16
I32 = jnp.int32

D_MODEL = 2048
N_HEADS_SWA = 8
HEAD_DIM_SWA = 128
SWA_PATTERNS = ((128, 1), (512, 4), (2048, 16))
SWA_BLOCK = 128
N_HEADS_MLA = 8
Q_LORA_RANK = 512
KV_LORA_RANK = 256
QK_NOPE_DIM = 128
QK_ROPE_DIM = 64
V_HEAD_DIM = 128
ROPE_THETA = 10000.0
D_SWA = N_HEADS_SWA * HEAD_DIM_SWA
D_MLA = N_HEADS_MLA * V_HEAD_DIM
N_EXPERTS = 64
N_GROUPS = 8
TOPK_GROUPS = 4
TOP_K = 8
D_EXPERT = 512
ROUTED_SCALE = 2.5
N_ADA = 6
EPS = 1e-6
NEG_INF = -1e30
LOG2E = 1.4426950408889634

LANES = 128
MLA_QK_PAD = 256
PACK_ROWS = D_MODEL // (2 * LANES)
VMEM_LIMIT = 56 * 1024 * 1024

TM_INPROJ = 256
TM_MLAPROJ = 512
T_MLA = 1024
DIL_GROUP = 2
TM_OUTPROJ = 256
OUTPROJ_SUBTILES = 2
TN_ROUTE = 512
TB_DISPATCH = 256
TR_EXPERT = 512
EXPERT_SUBTILES = 2
TB_COMBINE = 256


def _params(*sem):
    return pltpu.CompilerParams(dimension_semantics=sem, vmem_limit_bytes=VMEM_LIMIT)


def _rms(x, g):
    return x * lax.rsqrt(jnp.mean(x * x, axis=-1, keepdims=True) + EPS) * g


def _resident(shape):
    nd = len(shape)
    return pl.BlockSpec(shape, lambda *_: (0,) * nd, pipeline_mode=pl.Buffered(1))


def _pack_bf16_pairs(a, b):
    ua = lax.bitcast_convert_type(a.astype(BF16).astype(F32), I32)
    ub = lax.bitcast_convert_type(b.astype(BF16).astype(F32), I32)
    return lax.shift_right_logical(ua, jnp.int32(16)) | (ub & jnp.int32(-65536))


def _unpack_bf16_pairs(w):
    lo = lax.bitcast_convert_type(lax.shift_left(w, jnp.int32(16)), F32).astype(BF16)
    hi = lax.bitcast_convert_type(w & jnp.int32(-65536), F32).astype(BF16)
    return jnp.concatenate([lo, hi], axis=1)


def _packed_chunk(ref, s, rows):
    return _unpack_bf16_pairs(ref[pl.ds(s, rows, stride=PACK_ROWS), :])


def _ada_body(c_ref, w_ref, b_ref, o_ref):
    c = c_ref[...]
    a = c * jax.nn.sigmoid(c)
    o_ref[...] = jnp.sum(w_ref[...] * a, axis=0, keepdims=True) + b_ref[...]


def _ada(c, w_ada, b_ada):
    d, n = w_ada.shape
    tn = 512
    return pl.pallas_call(
        _ada_body,
        grid=(n // tn,),
        in_specs=[pl.BlockSpec((d, 1), lambda j: (0, 0)),
                  pl.BlockSpec((d, tn), lambda j: (0, j)),
                  pl.BlockSpec((1, tn), lambda j: (0, j))],
        out_specs=pl.BlockSpec((1, tn), lambda j: (0, j)),
        out_shape=jax.ShapeDtypeStruct((1, n), F32),
        compiler_params=_params("parallel"),
        name="ada",
    )(c.reshape(d, 1), w_ada, b_ada.reshape(1, n))


def _inproj_body(x_ref, g_ref, sc_ref, sh_ref, wqkv_ref, wr_ref, rest_ref, *out_and_scratch, tm, dils):
    view_refs, res_ref = out_and_scratch[:-1], out_and_scratch[-1]
    n = wqkv_ref.shape[1]
    h = (_rms(x_ref[...], g_ref[...]) * (1.0 + sc_ref[...]) + sh_ref[...]).astype(BF16)
    rest_ref[...] = jnp.dot(h, wr_ref[...], preferred_element_type=F32)
    res = jnp.dot(h, wqkv_ref[...], preferred_element_type=F32)
    chunks = range(n // LANES)
    for c in chunks:
        res_ref[c] = res[:, c * LANES:(c + 1) * LANES]
    for dil, v_ref in zip(dils, view_refs):
        if dil == 1:
            v_ref[...] = res.astype(BF16)
            continue
        for r in range(dil):
            for c in chunks:
                b = r * n + c * LANES
                v_ref[:, b:b + LANES] = res_ref[c, pl.ds(r, tm // dil, stride=dil), :].astype(BF16)


def _inproj(x, g, scale, shift, w_qkv, w_rest, dils):
    s, d = x.shape
    tm = min(TM_INPROJ, s)
    n1, n2 = w_qkv.shape[1], w_rest.shape[1]
    row = lambda i: (i, 0)
    vec = pl.BlockSpec((1, d), lambda i: (0, 0))
    outs = pl.pallas_call(
        functools.partial(_inproj_body, tm=tm, dils=dils),
        grid=(s // tm,),
        in_specs=[pl.BlockSpec((tm, d), row), vec, vec, vec, _resident((d, n1)), _resident((d, n2))],
        out_specs=[pl.BlockSpec((tm, n2), row)] + [pl.BlockSpec((tm // dil, dil * n1), row) for dil in dils],
        out_shape=[jax.ShapeDtypeStruct((s, n2), F32)]
        + [jax.ShapeDtypeStruct((s // dil, dil * n1), BF16) for dil in dils],
        scratch_shapes=[pltpu.VMEM((n1 // LANES, tm, LANES), F32)],
        compiler_params=_params("parallel"),
        name="inproj",
    )(x, g, scale, shift, w_qkv, w_rest)
    return outs[0], outs[1:]


def _rope_tail(t, c, s1, s2):
    return t * c + pltpu.roll(t, 32, 1) * s1 + pltpu.roll(t, 96, 1) * s2


def _mlaproj_body(rest_ref, gq_ref, gkv_ref, wuq_ref, wukv_ref, c_ref, s1_ref, s2_ref,
                  q_ref, k_ref, v_ref, *, scale):
    rest = rest_ref[...]
    c, s1, s2 = c_ref[...], s1_ref[...], s2_ref[...]
    cq = _rms(rest[:, :Q_LORA_RANK], gq_ref[...]).astype(BF16)
    ckv = _rms(rest[:, Q_LORA_RANK:Q_LORA_RANK + KV_LORA_RANK], gkv_ref[...]).astype(BF16)
    ktail = _rope_tail(rest[:, Q_LORA_RANK + KV_LORA_RANK:], c, s1, s2).astype(BF16)
    q = jnp.dot(cq, wuq_ref[...], preferred_element_type=F32)
    kv = jnp.dot(ckv, wukv_ref[...], preferred_element_type=F32)
    for h in range(N_HEADS_MLA):
        b = h * MLA_QK_PAD
        q_ref[:, b:b + LANES] = (q[:, b:b + LANES] * scale).astype(BF16)
        q_ref[:, b + LANES:b + 2 * LANES] = (_rope_tail(q[:, b + LANES:b + 2 * LANES], c, s1, s2) * scale).astype(BF16)
        k_ref[:, b:b + LANES] = kv[:, b:b + LANES].astype(BF16)
        k_ref[:, b + LANES:b + 2 * LANES] = ktail
        v_ref[:, b:b + LANES] = kv[:, b + LANES:b + 2 * LANES].astype(BF16)
        v_ref[:, b + LANES:b + 2 * LANES] = jnp.ones((q.shape[0], LANES), BF16)


def _mlaproj(rest, g_q, g_kv, w_uq, w_ukv, rc, rs1, rs2):
    s, nr = rest.shape
    tm = min(TM_MLAPROJ, s)
    nq = N_HEADS_MLA * MLA_QK_PAD
    row = lambda i: (i, 0)
    tab = pl.BlockSpec((tm, LANES), row)
    scale = float(QK_NOPE_DIM + QK_ROPE_DIM) ** -0.5 * LOG2E
    return pl.pallas_call(
        functools.partial(_mlaproj_body, scale=scale),
        grid=(s // tm,),
        in_specs=[pl.BlockSpec((tm, nr), row),
                  pl.BlockSpec((1, Q_LORA_RANK), lambda i: (0, 0)),
                  pl.BlockSpec((1, KV_LORA_RANK), lambda i: (0, 0)),
                  _resident(w_uq.shape), _resident(w_ukv.shape), tab, tab, tab],
        out_specs=[pl.BlockSpec((tm, nq), row)] * 3,
        out_shape=[jax.ShapeDtypeStruct((s, nq), BF16)] * 3,
        compiler_params=_params("parallel"),
        name="mlaproj",
    )(rest, g_q, g_kv, w_uq, w_ukv, rc, rs1, rs2)


def _mla_body(q_ref, k_ref, v_ref, o_ref, m_ref, acc_ref, sa_ref, sb_ref, *, t):
    qi = pl.program_id(1)
    q = q_ref[...]
    m_ref[...] = jnp.full(m_ref.shape, NEG_INF, F32)
    acc_ref[...] = jnp.zeros(acc_ref.shape, F32)

    def scores(j, dst):
        k = k_ref[pl.ds(pl.multiple_of(j * t, t), t), :]
        dst[...] = lax.dot_general(q, k, (((1,), (1,)), ((), ())), preferred_element_type=F32)

    def absorb(j, src, masked):
        s = src[...]
        if masked:
            r = lax.broadcasted_iota(I32, (t, t), 0)
            cidx = lax.broadcasted_iota(I32, (t, t), 1)
            s = jnp.where(cidx <= r, s, NEG_INF)
        v = v_ref[pl.ds(pl.multiple_of(j * t, t), t), :]
        m_old = m_ref[...]
        m_new = jnp.maximum(m_old, jnp.max(s, axis=-1, keepdims=True))
        p = jnp.exp2(s - m_new).astype(BF16)
        acc_ref[...] = jnp.exp2(m_old - m_new) * acc_ref[...] + jnp.dot(p, v, preferred_element_type=F32)
        m_ref[...] = m_new

    scores(0, sa_ref)

    def pair(i, carry):
        j = 2 * i
        scores(j + 1, sb_ref)
        absorb(j, sa_ref, False)
        scores(j + 2, sa_ref)
        absorb(j + 1, sb_ref, False)
        return carry

    lax.fori_loop(0, qi // 2, pair, 0)

    @pl.when(qi % 2 == 0)
    def _():
        absorb(qi, sa_ref, True)

    @pl.when(qi % 2 == 1)
    def _():
        scores(qi, sb_ref)
        absorb(qi - 1, sa_ref, False)
        absorb(qi, sb_ref, True)

    acc = acc_ref[...]
    o_ref[...] = acc[:, :V_HEAD_DIM] / acc[:, V_HEAD_DIM:]


def _mla(q, k, v):
    s = q.shape[0]
    t = min(T_MLA, s)
    head_cols = lambda h, i: (0, h)
    return pl.pallas_call(
        functools.partial(_mla_body, t=t),
        grid=(N_HEADS_MLA, s // t),
        in_specs=[pl.BlockSpec((t, MLA_QK_PAD), lambda h, i: (i, h)),
                  pl.BlockSpec((s, MLA_QK_PAD), head_cols, pipeline_mode=pl.Buffered(1)),
                  pl.BlockSpec((s, MLA_QK_PAD), head_cols, pipeline_mode=pl.Buffered(1))],
        out_specs=pl.BlockSpec((t, V_HEAD_DIM), lambda h, i: (i, h)),
        out_shape=jax.ShapeDtypeStruct((s, D_MLA), F32),
        scratch_shapes=[pltpu.VMEM((t, 1), F32), pltpu.VMEM((t, MLA_QK_PAD), F32),
                        pltpu.VMEM((t, t), F32), pltpu.VMEM((t, t), F32)],
        compiler_params=_params("parallel", "arbitrary"),
        name="mla",
    )(q, k, v)


def _dilated_body(q_ref, kc_ref, kp_ref, vc_ref, vp_ref, pq_ref, pkc_ref, pkp_ref, o_ref, lse_ref):
    n = pl.program_id(1)
    blk = SWA_BLOCK
    i = lax.broadcasted_iota(I32, (blk, blk), 0)
    j = lax.broadcasted_iota(I32, (blk, blk), 1)
    ok_cur = j <= i
    scale = float(HEAD_DIM_SWA) ** -0.5
    nt = (((1,), (1,)), ((), ()))
    ones = jnp.ones((blk, HEAD_DIM_SWA), BF16)
    rows = [slice(g * blk, (g + 1) * blk) for g in range(DIL_GROUP)]
    dist_cur, dist_prev, ok_prev, prev_of = [], [], [], []
    for g in range(DIL_GROUP):
        pq = pq_ref[rows[g], :]
        dist_cur.append(jnp.abs(pq - pkc_ref[0][:, rows[g]]))
        if g == 0:
            dist_prev.append(jnp.abs(pq - pkp_ref[0]))
            ok_prev.append((j >= i) & (n > 0))
            prev_of.append((kp_ref, vp_ref, slice(0, blk)))
        else:
            dist_prev.append(jnp.abs(pq - pkc_ref[0][:, rows[g - 1]]))
            ok_prev.append(j >= i)
            prev_of.append((kc_ref, vc_ref, rows[g - 1]))
    units = [(g, h) for g in range(DIL_GROUP) for h in range(N_HEADS_SWA)]
    hs = lambda h: slice(h * HEAD_DIM_SWA, (h + 1) * HEAD_DIM_SWA)
    slope = lambda h: 2.0 ** (-8.0 * (h + 1) / N_HEADS_SWA)
    sc = [jnp.where(ok_cur, lax.dot_general(q_ref[rows[g], hs(h)], kc_ref[rows[g], hs(h)], nt,
                                            preferred_element_type=F32) * scale - slope(h) * dist_cur[g], NEG_INF)
          for g, h in units]
    sp = [jnp.where(ok_prev[g], lax.dot_general(q_ref[rows[g], hs(h)], prev_of[g][0][prev_of[g][2], hs(h)], nt,
                                                preferred_element_type=F32) * scale - slope(h) * dist_prev[g], NEG_INF)
          for g, h in units]
    m = [jnp.max(jnp.maximum(a, b), axis=-1, keepdims=True) for a, b in zip(sc, sp)]
    pc = [jnp.exp(a - mm).astype(BF16) for a, mm in zip(sc, m)]
    pp = [jnp.exp(b - mm).astype(BF16) for b, mm in zip(sp, m)]
    acc = [jnp.dot(pc[u], jnp.concatenate([vc_ref[rows[g], hs(h)], ones], axis=1), preferred_element_type=F32)
           + jnp.dot(pp[u], jnp.concatenate([prev_of[g][1][prev_of[g][2], hs(h)], ones], axis=1),
                     preferred_element_type=F32)
           for u, (g, h) in enumerate(units)]
    for u, (g, h) in enumerate(units):
        den = acc[u][:, HEAD_DIM_SWA:]
        o_ref[rows[g], hs(h)] = acc[u][:, :HEAD_DIM_SWA] / den
        lse_ref[rows[g], hs(h)] = m[u] + jnp.log(den)


def _dilated(qkv_v, posf, dil):
    sd = qkv_v.shape[0]
    blk = SWA_BLOCK
    grp = DIL_GROUP * blk
    assert sd % grp == 0
    pos_v = posf.reshape(sd, dil)
    pq_v = jnp.repeat(pos_v, LANES, axis=1)
    pos_rows = pos_v.T.reshape(dil, 1, sd)
    prev = lambda n: jnp.maximum(DIL_GROUP * n - 1, 0)
    wide, one = (grp, D_SWA), (blk, D_SWA)
    return pl.pallas_call(
        _dilated_body,
        grid=(dil, sd // grp),
        in_specs=[pl.BlockSpec(wide, lambda r, n: (n, 3 * r)),
                  pl.BlockSpec(wide, lambda r, n: (n, 3 * r + 1)),
                  pl.BlockSpec(one, lambda r, n: (prev(n), 3 * r + 1)),
                  pl.BlockSpec(wide, lambda r, n: (n, 3 * r + 2)),
                  pl.BlockSpec(one, lambda r, n: (prev(n), 3 * r + 2)),
                  pl.BlockSpec((grp, LANES), lambda r, n: (n, r)),
                  pl.BlockSpec((1, 1, grp), lambda r, n: (r, 0, n)),
                  pl.BlockSpec((1, 1, blk), lambda r, n: (r, 0, prev(n)))],
        out_specs=[pl.BlockSpec(wide, lambda r, n: (n, r)), pl.BlockSpec(wide, lambda r, n: (n, r))],
        out_shape=[jax.ShapeDtypeStruct((sd, dil * D_SWA), F32), jax.ShapeDtypeStruct((sd, dil * D_SWA), F32)],
        compiler_params=_params("parallel", "parallel"),
        name=f"dil{dil}",
    )(qkv_v, qkv_v, qkv_v, qkv_v, qkv_v, pq_v, pos_rows, pos_rows)


def _outproj_body(*refs, tm, dils):
    npat = len(dils)
    o_views, l_views = refs[:npat], refs[npat:2 * npat]
    (ob_ref, x_ref, gsw_ref, gml_ref, wo_ref, ga_ref, nfg_ref, scf_ref, shf_ref, wrt_ref,
     x1_ref, h2p_ref, lgt_ref) = refs[2 * npat:2 * npat + 13]
    scratch = list(refs[2 * npat + 13:])

    chunks = range(D_SWA // LANES)

    def token_order(view_ref, dil):
        if dil == 1:
            return lambda rs: view_ref[rs, :]
        nat_ref = scratch.pop(0)
        for r in range(dil):
            for c in chunks:
                b = r * D_SWA + c * LANES
                nat_ref[c, pl.ds(r, tm // dil, stride=dil), :] = view_ref[:, b:b + LANES]
        return lambda rs: jnp.concatenate([nat_ref[c, rs, :] for c in chunks], axis=1)

    o1, o2, o3 = [token_order(v, dil) for v, dil in zip(o_views, dils)]
    l1f, l2f, l3f = [token_order(v, dil) for v, dil in zip(l_views, dils)]

    nsub = OUTPROJ_SUBTILES
    r = tm // nsub
    subs = range(nsub)
    rows = [slice(i * r, (i + 1) * r) for i in subs]

    def merged(rs):
        l1, l2, l3 = l1f(rs), l2f(rs), l3f(rs)
        m = jnp.maximum(jnp.maximum(l1, l2), l3)
        e1, e2, e3 = jnp.exp(l1 - m), jnp.exp(l2 - m), jnp.exp(l3 - m)
        return (e1 * o1(rs) + e2 * o2(rs) + e3 * o3(rs)) / (e1 + e2 + e3)

    mix = [jnp.concatenate([_rms(merged(rs), gsw_ref[...]), _rms(ob_ref[rs, :], gml_ref[...])],
                           axis=-1).astype(BF16) for rs in rows]
    proj = [jnp.dot(mix[i], wo_ref[...], preferred_element_type=F32) for i in subs]
    x1 = [x_ref[rows[i], :] + ga_ref[...] * proj[i] for i in subs]
    h2 = [_rms(x1[i], nfg_ref[...]) * (1.0 + scf_ref[...]) + shf_ref[...] for i in subs]
    for i in subs:
        x1_ref[rows[i], :] = x1[i]
        lgt_ref[:, rows[i]] = lax.dot_general(wrt_ref[...], h2[i], (((1,), (1,)), ((), ())),
                                              precision=lax.Precision.HIGHEST, preferred_element_type=F32)
    for i in subs:
        for s in range(PACK_ROWS):
            b = 2 * LANES * s
            h2p_ref[pl.ds(i * r * PACK_ROWS + s, r, stride=PACK_ROWS), :] = _pack_bf16_pairs(
                h2[i][:, b:b + LANES], h2[i][:, b + LANES:b + 2 * LANES])


def _outproj(o_pats, lse_pats, dils, o_b, x, g_sw, g_ml, w_o, gate_a, nfg, scale_f, shift_f, w_router_t):
    s, d = x.shape
    tm = min(TM_OUTPROJ, s)
    row = lambda i: (i, 0)
    views = [pl.BlockSpec((tm // dil, dil * D_SWA), row) for dil in dils]
    vec = lambda n: pl.BlockSpec((1, n), lambda i: (0, 0))
    n_reordered = 2 * sum(1 for dil in dils if dil > 1)
    return pl.pallas_call(
        functools.partial(_outproj_body, tm=tm, dils=dils),
        grid=(s // tm,),
        in_specs=views + views + [pl.BlockSpec((tm, D_MLA), row), pl.BlockSpec((tm, d), row), vec(D_SWA),
                                  vec(D_MLA), _resident(w_o.shape), vec(d), vec(d), vec(d), vec(d),
                                  _resident(w_router_t.shape)],
        out_specs=[pl.BlockSpec((tm, d), row), pl.BlockSpec((tm * PACK_ROWS, LANES), row),
                   pl.BlockSpec((N_EXPERTS, tm), lambda i: (0, i))],
        out_shape=[jax.ShapeDtypeStruct((s, d), F32), jax.ShapeDtypeStruct((s * PACK_ROWS, LANES), I32),
                   jax.ShapeDtypeStruct((N_EXPERTS, s), F32)],
        scratch_shapes=[pltpu.VMEM((D_SWA // LANES, tm, LANES), F32)] * n_reordered,
        compiler_params=_params("parallel"),
        name="outproj",
    )(*o_pats, *lse_pats, o_b, x, g_sw, g_ml, w_o, gate_a, nfg, scale_f, shift_f, w_router_t)


def _first_index(hit_value, x, iota, size, axis):
    return jnp.min(jnp.where(x == hit_value, iota, size), axis=axis, keepdims=True)


def _route_body(lgt_ref, bias_ref, tri_ref, eidx_ref, wts_ref, rank_ref, cnt_ref, carry_ref, *, tn):
    @pl.when(pl.program_id(0) == 0)
    def _():
        carry_ref[...] = jnp.zeros(carry_ref.shape, F32)

    gsz = N_EXPERTS // N_GROUPS
    scores = jax.nn.sigmoid(lgt_ref[...])
    choice = scores + bias_ref[...]
    neg = jnp.float32(-jnp.inf)

    g3 = choice.reshape(N_GROUPS, gsz, tn)
    i3 = lax.broadcasted_iota(I32, g3.shape, 1)
    m1 = jnp.max(g3, axis=1, keepdims=True)
    f1 = _first_index(m1, g3, i3, gsz, 1)
    m2 = jnp.max(jnp.where(i3 == f1, neg, g3), axis=1, keepdims=True)
    gs = (m1 + m2).reshape(N_GROUPS, tn)

    ig = lax.broadcasted_iota(I32, gs.shape, 0)
    gsel = jnp.zeros(gs.shape, F32)
    for _ in range(TOPK_GROUPS):
        hit = ig == _first_index(jnp.max(gs, axis=0, keepdims=True), gs, ig, N_GROUPS, 0)
        gsel = jnp.where(hit, 1.0, gsel)
        gs = jnp.where(hit, neg, gs)
    emask = jnp.broadcast_to(gsel.reshape(N_GROUPS, 1, tn), (N_GROUPS, gsz, tn)).reshape(N_EXPERTS, tn)
    cand = jnp.where(emask > 0.0, choice, NEG_INF)

    ie = lax.broadcasted_iota(I32, cand.shape, 0)
    picks, wsel = [], []
    onehot = jnp.zeros(cand.shape, F32)
    for _ in range(TOP_K):
        f = _first_index(jnp.max(cand, axis=0, keepdims=True), cand, ie, N_EXPERTS, 0)
        hit = ie == f
        picks.append(f)
        wsel.append(jnp.sum(jnp.where(hit, scores, 0.0), axis=0, keepdims=True))
        onehot = jnp.where(hit, 1.0, onehot)
        cand = jnp.where(hit, neg, cand)

    rank = carry_ref[...] + jnp.dot(onehot.astype(BF16), tri_ref[...], preferred_element_type=F32)
    carry_ref[...] = carry_ref[...] + jnp.sum(onehot, axis=1, keepdims=True)
    cnt_ref[...] = carry_ref[...]

    w = jnp.concatenate(wsel, axis=0)
    wts_ref[...] = w / jnp.sum(w, axis=0, keepdims=True) * ROUTED_SCALE
    eidx_ref[...] = jnp.concatenate(picks, axis=0)
    rank_ref[...] = jnp.concatenate(
        [jnp.sum(jnp.where(ie == f, rank, 0.0), axis=0, keepdims=True) for f in picks], axis=0).astype(I32)


def _route(logits_t, router_bias):
    e, n = logits_t.shape
    tn = min(TN_ROUTE, n)
    tri = (lax.broadcasted_iota(I32, (tn, tn), 0) < lax.broadcasted_iota(I32, (tn, tn), 1)).astype(BF16)
    col = lambda i: (0, i)
    return pl.pallas_call(
        functools.partial(_route_body, tn=tn),
        grid=(n // tn,),
        in_specs=[pl.BlockSpec((e, tn), col), pl.BlockSpec((e, 1), lambda i: (0, 0)),
                  pl.BlockSpec((tn, tn), lambda i: (0, 0))],
        out_specs=[pl.BlockSpec((TOP_K, tn), col), pl.BlockSpec((TOP_K, tn), col), pl.BlockSpec((TOP_K, tn), col),
                   pl.BlockSpec((e, 1), lambda i: (0, 0))],
        out_shape=[jax.ShapeDtypeStruct((TOP_K, n), I32), jax.ShapeDtypeStruct((TOP_K, n), F32),
                   jax.ShapeDtypeStruct((TOP_K, n), I32), jax.ShapeDtypeStruct((e, 1), F32)],
        scratch_shapes=[pltpu.VMEM((e, 1), F32)],
        compiler_params=_params("arbitrary"),
        name="route",
    )(logits_t, router_bias.reshape(e, 1), tri)


def _dispatch_body(tail_start_ref, tail_len_ref, dest_ref, h_ref, xs_ref, zeros_ref, sem, zsem, *, tb, tr):
    sizes = [1 << b for b in reversed(range((tr - 1).bit_length()))]

    def tail_copies(e, act):
        start, length = tail_start_ref[e], tail_len_ref[e]
        done = jnp.int32(0)
        for size in sizes:
            piece = (length & size) != 0
            dst = xs_ref.at[pl.ds(pl.multiple_of((start + done) * PACK_ROWS, PACK_ROWS), size * PACK_ROWS), :]
            copy = pltpu.make_async_copy(zeros_ref.at[pl.ds(0, size * PACK_ROWS), :], dst, zsem)
            pl.when(piece)(functools.partial(act, copy))
            done = done + (length & size)

    @pl.when(pl.program_id(0) == 0)
    def _():
        zeros_ref[...] = jnp.zeros(zeros_ref.shape, I32)

        def start_tail(e, carry):
            tail_copies(e, lambda copy: copy.start())
            return carry

        def wait_tail(e, carry):
            tail_copies(e, lambda copy: copy.wait())
            return carry

        lax.fori_loop(0, N_EXPERTS, start_tail, 0)
        lax.fori_loop(0, N_EXPERTS, wait_tail, 0)

    def row_copy(t, k):
        src = h_ref.at[pl.ds(pl.multiple_of(t * PACK_ROWS, PACK_ROWS), PACK_ROWS), :]
        dst = xs_ref.at[pl.ds(pl.multiple_of(dest_ref[t * TOP_K + k] * PACK_ROWS, PACK_ROWS), PACK_ROWS), :]
        return pltpu.make_async_copy(src, dst, sem)

    def issue(t, carry):
        for k in range(TOP_K):
            row_copy(t, k).start(priority=k % 2)
        return carry

    def drain(t, carry):
        for k in range(TOP_K):
            row_copy(t, k).wait()
        return carry

    lax.fori_loop(0, tb, issue, 0)
    lax.fori_loop(0, tb, drain, 0)


def _dispatch(dest, h2p, tail_start, tail_len, n_slots, tr):
    n = dest.shape[0] // TOP_K
    tb = min(TB_DISPATCH, n)
    max_piece = 1 << ((tr - 1).bit_length() - 1)
    grid_spec = pltpu.PrefetchScalarGridSpec(
        num_scalar_prefetch=2,
        grid=(n // tb,),
        in_specs=[pl.BlockSpec((tb * TOP_K,), lambda i, ts, tl: (i,), memory_space=pltpu.SMEM),
                  pl.BlockSpec((tb * PACK_ROWS, LANES), lambda i, ts, tl: (i, 0))],
        out_specs=pl.BlockSpec(memory_space=pl.ANY),
        scratch_shapes=[pltpu.VMEM((max_piece * PACK_ROWS, LANES), I32),
                        pltpu.SemaphoreType.DMA(()), pltpu.SemaphoreType.DMA(())],
    )
    return pl.pallas_call(
        functools.partial(_dispatch_body, tb=tb, tr=tr),
        grid_spec=grid_spec,
        out_shape=jax.ShapeDtypeStruct((n_slots * PACK_ROWS, LANES), I32),
        compiler_params=_params("arbitrary"),
        name="dispatch",
    )(tail_start, tail_len, dest, h2p)


def _swiglu_packed(xp_ref, wgu_ref, wd_ref, rows, nsub=1):
    r = rows // nsub
    subs = range(nsub)
    x = [jnp.concatenate([_unpack_bf16_pairs(xp_ref[pl.ds(i * r * PACK_ROWS + s, r, stride=PACK_ROWS), :])
                          for s in range(PACK_ROWS)], axis=1) for i in subs]
    h = [jnp.dot(x[i], wgu_ref[...], preferred_element_type=F32) for i in subs]
    a = [(h[i][:, :D_EXPERT] * jax.nn.sigmoid(h[i][:, :D_EXPERT]) * h[i][:, D_EXPERT:]).astype(BF16) for i in subs]
    return [jnp.dot(a[i], wd_ref[...], preferred_element_type=F32) for i in subs]


def _experts_body(be_ref, nv_ref, new_ref, xs_ref, wg_ref, wu_ref, wd_ref, y_ref, wgu_s, wd_s, *, tr):
    del be_ref
    b = pl.program_id(0)

    @pl.when(new_ref[b] > 0)
    def _():
        wgu_s[:, :D_EXPERT] = wg_ref[0].astype(BF16)
        wgu_s[:, D_EXPERT:] = wu_ref[0].astype(BF16)
        wd_s[...] = wd_ref[0].astype(BF16)

    nsub = EXPERT_SUBTILES
    r = tr // nsub
    nv = nv_ref[b]

    def run(live):
        ys = _swiglu_packed(xs_ref, wgu_s, wd_s, live * r, live)
        for i, y in enumerate(ys):
            for s in range(PACK_ROWS):
                c = 2 * LANES * s
                y_ref[pl.ds(i * r * PACK_ROWS + s, r, stride=PACK_ROWS), :] = _pack_bf16_pairs(
                    y[:, c:c + LANES], y[:, c + LANES:c + 2 * LANES])
        if live < nsub:
            y_ref[live * r * PACK_ROWS:, :] = jnp.zeros(((nsub - live) * r * PACK_ROWS, LANES), I32)

    for live in range(1, nsub + 1):
        pl.when((nv > (live - 1) * r) & (nv <= live * r))(functools.partial(run, live))

    @pl.when(nv == 0)
    def _():
        y_ref[...] = jnp.zeros(y_ref.shape, I32)


def _experts(blk_e, blk_nv, blk_new, xs, w_gate, w_up, w_down):
    tr = TR_EXPERT
    nb = blk_e.shape[0]
    d, f = w_gate.shape[1], w_gate.shape[2]
    grid_spec = pltpu.PrefetchScalarGridSpec(
        num_scalar_prefetch=3,
        grid=(nb,),
        in_specs=[pl.BlockSpec((tr * PACK_ROWS, LANES), lambda b, be, nv, nw: (jnp.where(nv[b] > 0, b, 0), 0)),
                  pl.BlockSpec((1, d, f), lambda b, be, nv, nw: (be[b], 0, 0)),
                  pl.BlockSpec((1, d, f), lambda b, be, nv, nw: (be[b], 0, 0)),
                  pl.BlockSpec((1, f, d), lambda b, be, nv, nw: (be[b], 0, 0))],
        out_specs=pl.BlockSpec((tr * PACK_ROWS, LANES), lambda b, be, nv, nw: (b, 0)),
        scratch_shapes=[pltpu.VMEM((d, 2 * f), BF16), pltpu.VMEM((f, d), BF16)],
    )
    return pl.pallas_call(
        functools.partial(_experts_body, tr=tr),
        grid_spec=grid_spec,
        out_shape=jax.ShapeDtypeStruct((nb * tr * PACK_ROWS, LANES), I32),
        compiler_params=_params("arbitrary"),
        name="experts",
    )(blk_e, blk_nv, blk_new, xs, w_gate, w_up, w_down)


def _combine_body(dest_ref, dest_next_ref, wts_ref, h2p_ref, x1_ref, gf_ref, fg_ref, wsgu_ref, wsd_ref, y_ref,
                  out_ref, buf_a, buf_b, sem_a, sem_b, *, tb, nsteps):
    i = pl.program_id(0)
    buf_rows = TOP_K * tb * PACK_ROWS

    def row_copy(slots_ref, t, k, buf, sem):
        src = y_ref.at[pl.ds(pl.multiple_of(slots_ref[t * TOP_K + k] * PACK_ROWS, PACK_ROWS), PACK_ROWS), :]
        dst = buf.at[pl.ds(pl.multiple_of((k * tb + t) * PACK_ROWS, PACK_ROWS), PACK_ROWS), :]
        return pltpu.make_async_copy(src, dst, sem)

    def wait_block(buf, sem):
        pltpu.make_async_copy(y_ref.at[pl.ds(0, buf_rows), :], buf, sem).wait()

    @pl.when(i == 0)
    def _():
        def issue(t, carry):
            for k in range(TOP_K):
                row_copy(dest_ref, t, k, buf_a, sem_a).start(priority=k % 2)
            return carry

        lax.fori_loop(0, tb, issue, 0)

    def step(buf, sem, next_buf, next_sem):
        wait_block(buf, sem)
        for t in range(tb):
            for k in range(TOP_K):
                row_copy(dest_next_ref, t, k, next_buf, next_sem).start(priority=k % 2)
        shared = _swiglu_packed(h2p_ref, wsgu_ref, wsd_ref, tb)[0]
        wts = wts_ref[...]
        wb = [jnp.broadcast_to(wts[:, k:k + 1], (tb, LANES)) for k in range(TOP_K)]
        cols = []
        for s in range(PACK_ROWS):
            c = 2 * LANES * s
            lo, hi = shared[:, c:c + LANES], shared[:, c + LANES:c + 2 * LANES]
            for k in range(TOP_K):
                w = buf[pl.ds(k * tb * PACK_ROWS + s, tb, stride=PACK_ROWS), :]
                lo = lo + wb[k] * lax.bitcast_convert_type(lax.shift_left(w, jnp.int32(16)), F32)
                hi = hi + wb[k] * lax.bitcast_convert_type(w & jnp.int32(-65536), F32)
            cols += [lo, hi]
        moe = jnp.concatenate(cols, axis=1)
        out_ref[...] = _rms(x1_ref[...] + gf_ref[...] * moe, fg_ref[...])

        @pl.when(i == nsteps - 1)
        def _():
            wait_block(next_buf, next_sem)

    pl.when(i % 2 == 0)(functools.partial(step, buf_a, sem_a, buf_b, sem_b))
    pl.when(i % 2 == 1)(functools.partial(step, buf_b, sem_b, buf_a, sem_a))


def _combine(dest, wts_t, h2p, x1, gate_f, final_g, w_sgu, w_sd, y):
    n, d = x1.shape
    tb = min(TB_COMBINE, n)
    row = lambda i: (i, 0)
    vec = pl.BlockSpec((1, d), lambda i: (0, 0))
    nsteps = n // tb
    buf = pltpu.VMEM((TOP_K * tb * PACK_ROWS, LANES), I32)
    return pl.pallas_call(
        functools.partial(_combine_body, tb=tb, nsteps=nsteps),
        grid=(nsteps,),
        in_specs=[pl.BlockSpec((tb * TOP_K,), lambda i: (i,), memory_space=pltpu.SMEM),
                  pl.BlockSpec((tb * TOP_K,), lambda i: (jnp.minimum(i + 1, nsteps - 1),), memory_space=pltpu.SMEM),
                  pl.BlockSpec((tb, TOP_K), row),
                  pl.BlockSpec((tb * PACK_ROWS, LANES), row),
                  pl.BlockSpec((tb, d), row), vec, vec,
                  _resident(w_sgu.shape), _resident(w_sd.shape),
                  pl.BlockSpec(memory_space=pl.ANY)],
        out_specs=pl.BlockSpec((tb, d), row),
        out_shape=jax.ShapeDtypeStruct((n, d), F32),
        scratch_shapes=[buf, buf, pltpu.SemaphoreType.DMA(()), pltpu.SemaphoreType.DMA(())],
        compiler_params=_params("arbitrary"),
        name="combine",
    )(dest, dest, wts_t, h2p, x1, gate_f, final_g, w_sgu, w_sd, y)


def _rope_tables(pos):
    half = QK_ROPE_DIM // 2
    inv_freq = ROPE_THETA ** (-jnp.arange(half, dtype=F32) / half)
    ang = pos.astype(F32)[:, None] * inv_freq
    cos, sin = jnp.cos(ang), jnp.sin(ang)
    z = jnp.zeros_like(cos)
    c = jnp.concatenate([cos, cos, z, z], axis=1)
    s1 = jnp.concatenate([z, sin, z, z], axis=1)
    s2 = jnp.concatenate([-sin, z, z, z], axis=1)
    return c, s1, s2


def _layer(x, c, pos, norm_attn_g, w_ada, b_ada, w_in, g_q, w_uq, g_kv, w_ukv, g_out_swa, g_out_mla, w_o,
           norm_ffn_g, w_router, router_bias, w_exp_gate, w_exp_up, w_exp_down, w_sh_gate, w_sh_up, w_sh_down,
           final_g):
    s, d = x.shape
    row = lambda a: a.reshape(1, -1)

    mod = _ada(c, w_ada, b_ada)
    shift_a, scale_a, gate_a, shift_f, scale_f, gate_f = [mod[:, i * d:(i + 1) * d] for i in range(N_ADA)]

    n_qkv = 3 * D_SWA
    w_qkv = w_in[:, :n_qkv].astype(BF16)
    w_rest = jnp.pad(w_in[:, n_qkv:], ((0, 0), (0, LANES - QK_ROPE_DIM))).astype(BF16)
    dils = tuple(dil for _, dil in SWA_PATTERNS)
    assert all(window // dil == SWA_BLOCK and s % (dil * SWA_BLOCK) == 0 for window, dil in SWA_PATTERNS)
    rest, qkv_views = _inproj(x, row(norm_attn_g), scale_a, shift_a, w_qkv, w_rest, dils)

    dq = QK_NOPE_DIM + QK_ROPE_DIM
    w_uq_p = jnp.pad(w_uq.reshape(Q_LORA_RANK, N_HEADS_MLA, dq), ((0, 0), (0, 0), (0, MLA_QK_PAD - dq)))
    w_uq_p = w_uq_p.reshape(Q_LORA_RANK, N_HEADS_MLA * MLA_QK_PAD).astype(BF16)
    rc, rs1, rs2 = _rope_tables(pos)
    q_m, k_m, v_m = _mlaproj(rest, row(g_q), row(g_kv), w_uq_p, w_ukv.astype(BF16), rc, rs1, rs2)
    o_b = _mla(q_m, k_m, v_m)

    posf = pos.astype(F32)
    o_pats, lse_pats = zip(*[_dilated(qkv_v, posf, dil) for qkv_v, dil in zip(qkv_views, dils)])

    x1, h2p, logits_t = _outproj(o_pats, lse_pats, dils, o_b, x, row(g_out_swa), row(g_out_mla),
                                 w_o.astype(BF16), gate_a, row(norm_ffn_g), scale_f, shift_f, w_router.T)

    eidx, wts, rank, cnt = _route(logits_t, router_bias)
    tr = TR_EXPERT
    counts = cnt[:, 0].astype(I32)
    padded = (counts + tr - 1) // tr * tr
    e_ids = jnp.arange(N_EXPERTS, dtype=I32)
    pad_end = jnp.sum(jnp.where(e_ids[None, :] <= e_ids[:, None], padded[None, :], 0), axis=1)
    pad_start = pad_end - padded
    lookup = lambda table, idx: jnp.sum(jnp.where(idx[..., None] == e_ids, table, 0), axis=-1)
    dest = (lookup(pad_start, eidx) + rank).T.reshape(-1)
    n_slots = s * TOP_K + N_EXPERTS * tr
    blk_start = jnp.arange(n_slots // tr, dtype=I32) * tr
    blk_e = jnp.minimum(jnp.sum((pad_end[None, :] <= blk_start[:, None]).astype(I32), axis=1), N_EXPERTS - 1)
    blk_nv = jnp.clip(lookup(counts, blk_e) - (blk_start - lookup(pad_start, blk_e)), 0, tr)
    blk_new = ((blk_nv > 0) & (blk_start == lookup(pad_start, blk_e))).astype(I32)

    xs = _dispatch(dest, h2p, pad_start + counts, padded - counts, n_slots, tr)
    y = _experts(blk_e, blk_nv, blk_new, xs, w_exp_gate, w_exp_up, w_exp_down)
    w_sgu = jnp.concatenate([w_sh_gate, w_sh_up], axis=1).astype(BF16)
    return _combine(dest, wts.T, h2p, x1, gate_f, row(final_g), w_sgu, w_sh_down.astype(BF16), y)


def kernel(x, c, positions, norm_attn_g, w_ada, b_ada, w_in, g_q, w_uq, g_kv, w_ukv, g_out_swa, g_out_mla, w_o,
           norm_ffn_g, w_router, router_bias, w_exp_gate, w_exp_up, w_exp_down, w_sh_gate, w_sh_up, w_sh_down,
           final_norm_g):
    assert x.shape[0] == 1 and w_ada.shape[0] == 1
    out = _layer(x[0], c[0], positions[0], norm_attn_g[0], w_ada[0], b_ada[0], w_in[0], g_q[0], w_uq[0], g_kv[0],
                 w_ukv[0], g_out_swa[0], g_out_mla[0], w_o[0], norm_ffn_g[0], w_router[0], router_bias[0],
                 w_exp_gate[0], w_exp_up[0], w_exp_down[0], w_sh_gate[0], w_sh_up[0], w_sh_down[0], final_norm_g)
    return out[None]
```

```python
import functools

import jax
import jax.numpy as jnp
from jax import lax
from jax.experimental import pallas as pl
from jax.experimental.pallas import tpu as pltpu

F32 = jnp.float32
BF16 = jnp.bfloat16
I32 = jnp.int32

D_MODEL = 2048
N_HEADS_SWA = 8
HEAD_DIM_SWA = 128
SWA_PATTERNS = ((128, 1), (512, 4), (2048, 16))
SWA_BLOCK = 128
N_HEADS_MLA = 8
Q_LORA_RANK = 512
KV_LORA_RANK = 256
QK_NOPE_DIM = 128
QK_ROPE_DIM = 64
V_HEAD_DIM = 128
ROPE_THETA = 10000.0
D_SWA = N_HEADS_SWA * HEAD_DIM_SWA
D_MLA = N_HEADS_MLA * V_HEAD_DIM
N_EXPERTS = 64
N_GROUPS = 8
TOPK_GROUPS = 4
TOP_K = 8
D_EXPERT = 512
ROUTED_SCALE = 2.5
N_ADA = 6
EPS = 1e-6
NEG_INF = -1e30
LOG2E = 1.4426950408889634

LANES = 128
MLA_QK_PAD = 256
PACK_ROWS = D_MODEL // (2 * LANES)
VMEM_LIMIT = 56 * 1024 * 1024

TM_INPROJ = 256
TM_MLAPROJ = 512
T_MLA = 1024
DIL_GROUP = 2
TM_OUTPROJ = 256
OUTPROJ_SUBTILES = 2
TN_ROUTE = 512
TB_DISPATCH = 256
TR_EXPERT = 512
EXPERT_SUBTILES = 2
TB_COMBINE = 256


def _params(*sem):
    return pltpu.CompilerParams(dimension_semantics=sem, vmem_limit_bytes=VMEM_LIMIT)


def _rms(x, g):
    return x * lax.rsqrt(jnp.mean(x * x, axis=-1, keepdims=True) + EPS) * g


def _resident(shape):
    nd = len(shape)
    return pl.BlockSpec(shape, lambda *_: (0,) * nd, pipeline_mode=pl.Buffered(1))


def _pack_bf16_pairs(a, b):
    ua = lax.bitcast_convert_type(a.astype(BF16).astype(F32), I32)
    ub = lax.bitcast_convert_type(b.astype(BF16).astype(F32), I32)
    return lax.shift_right_logical(ua, jnp.int32(16)) | (ub & jnp.int32(-65536))


def _unpack_bf16_pairs(w):
    lo = lax.bitcast_convert_type(lax.shift_left(w, jnp.int32(16)), F32).astype(BF16)
    hi = lax.bitcast_convert_type(w & jnp.int32(-65536), F32).astype(BF16)
    return jnp.concatenate([lo, hi], axis=1)


def _packed_chunk(ref, s, rows):
    return _unpack_bf16_pairs(ref[pl.ds(s, rows, stride=PACK_ROWS), :])


def _ada_body(c_ref, w_ref, b_ref, o_ref):
    c = c_ref[...]
    a = c * jax.nn.sigmoid(c)
    o_ref[...] = jnp.sum(w_ref[...] * a, axis=0, keepdims=True) + b_ref[...]


def _ada(c, w_ada, b_ada):
    d, n = w_ada.shape
    tn = 512
    return pl.pallas_call(
        _ada_body,
        grid=(n // tn,),
        in_specs=[pl.BlockSpec((d, 1), lambda j: (0, 0)),
                  pl.BlockSpec((d, tn), lambda j: (0, j)),
                  pl.BlockSpec((1, tn), lambda j: (0, j))],
        out_specs=pl.BlockSpec((1, tn), lambda j: (0, j)),
        out_shape=jax.ShapeDtypeStruct((1, n), F32),
        compiler_params=_params("parallel"),
        name="ada",
    )(c.reshape(d, 1), w_ada, b_ada.reshape(1, n))


def _inproj_body(x_ref, g_ref, sc_ref, sh_ref, wqkv_ref, wr_ref, rest_ref, *out_and_scratch, tm, dils):
    view_refs, res_ref = out_and_scratch[:-1], out_and_scratch[-1]
    n = wqkv_ref.shape[1]
    h = (_rms(x_ref[...], g_ref[...]) * (1.0 + sc_ref[...]) + sh_ref[...]).astype(BF16)
    rest_ref[...] = jnp.dot(h, wr_ref[...], preferred_element_type=F32)
    res = jnp.dot(h, wqkv_ref[...], preferred_element_type=F32)
    chunks = range(n // LANES)
    for c in chunks:
        res_ref[c] = res[:, c * LANES:(c + 1) * LANES]
    for dil, v_ref in zip(dils, view_refs):
        if dil == 1:
            v_ref[...] = res.astype(BF16)
            continue
        for r in range(dil):
            for c in chunks:
                b = r * n + c * LANES
                v_ref[:, b:b + LANES] = res_ref[c, pl.ds(r, tm // dil, stride=dil), :].astype(BF16)


def _inproj(x, g, scale, shift, w_qkv, w_rest, dils):
    s, d = x.shape
    tm = min(TM_INPROJ, s)
    n1, n2 = w_qkv.shape[1], w_rest.shape[1]
    row = lambda i: (i, 0)
    vec = pl.BlockSpec((1, d), lambda i: (0, 0))
    outs = pl.pallas_call(
        functools.partial(_inproj_body, tm=tm, dils=dils),
        grid=(s // tm,),
        in_specs=[pl.BlockSpec((tm, d), row), vec, vec, vec, _resident((d, n1)), _resident((d, n2))],
        out_specs=[pl.BlockSpec((tm, n2), row)] + [pl.BlockSpec((tm // dil, dil * n1), row) for dil in dils],
        out_shape=[jax.ShapeDtypeStruct((s, n2), F32)]
        + [jax.ShapeDtypeStruct((s // dil, dil * n1), BF16) for dil in dils],
        scratch_shapes=[pltpu.VMEM((n1 // LANES, tm, LANES), F32)],
        compiler_params=_params("parallel"),
        name="inproj",
    )(x, g, scale, shift, w_qkv, w_rest)
    return outs[0], outs[1:]


def _rope_tail(t, c, s1, s2):
    return t * c + pltpu.roll(t, 32, 1) * s1 + pltpu.roll(t, 96, 1) * s2


def _mlaproj_body(rest_ref, gq_ref, gkv_ref, wuq_ref, wukv_ref, c_ref, s1_ref, s2_ref,
                  q_ref, k_ref, v_ref, *, scale):
    rest = rest_ref[...]
    c, s1, s2 = c_ref[...], s1_ref[...], s2_ref[...]
    cq = _rms(rest[:, :Q_LORA_RANK], gq_ref[...]).astype(BF16)
    ckv = _rms(rest[:, Q_LORA_RANK:Q_LORA_RANK + KV_LORA_RANK], gkv_ref[...]).astype(BF16)
    ktail = _rope_tail(rest[:, Q_LORA_RANK + KV_LORA_RANK:], c, s1, s2).astype(BF16)
    q = jnp.dot(cq, wuq_ref[...], preferred_element_type=F32)
    kv = jnp.dot(ckv, wukv_ref[...], preferred_element_type=F32)
    for h in range(N_HEADS_MLA):
        b = h * MLA_QK_PAD
        q_ref[:, b:b + LANES] = (q[:, b:b + LANES] * scale).astype(BF16)
        q_ref[:, b + LANES:b + 2 * LANES] = (_rope_tail(q[:, b + LANES:b + 2 * LANES], c, s1, s2) * scale).astype(BF16)
        k_ref[:, b:b + LANES] = kv[:, b:b + LANES].astype(BF16)
        k_ref[:, b + LANES:b + 2 * LANES] = ktail
        v_ref[:, b:b + LANES] = kv[:, b + LANES:b + 2 * LANES].astype(BF16)
        v_ref[:, b + LANES:b + 2 * LANES] = jnp.ones((q.shape[0], LANES), BF16)


def _mlaproj(rest, g_q, g_kv, w_uq, w_ukv, rc, rs1, rs2):
    s, nr = rest.shape
    tm = min(TM_MLAPROJ, s)
    nq = N_HEADS_MLA * MLA_QK_PAD
    row = lambda i: (i, 0)
    tab = pl.BlockSpec((tm, LANES), row)
    scale = float(QK_NOPE_DIM + QK_ROPE_DIM) ** -0.5 * LOG2E
    return pl.pallas_call(
        functools.partial(_mlaproj_body, scale=scale),
        grid=(s // tm,),
        in_specs=[pl.BlockSpec((tm, nr), row),
                  pl.BlockSpec((1, Q_LORA_RANK), lambda i: (0, 0)),
                  pl.BlockSpec((1, KV_LORA_RANK), lambda i: (0, 0)),
                  _resident(w_uq.shape), _resident(w_ukv.shape), tab, tab, tab],
        out_specs=[pl.BlockSpec((tm, nq), row)] * 3,
        out_shape=[jax.ShapeDtypeStruct((s, nq), BF16)] * 3,
        compiler_params=_params("parallel"),
        name="mlaproj",
    )(rest, g_q, g_kv, w_uq, w_ukv, rc, rs1, rs2)


def _mla_body(q_ref, k_ref, v_ref, o_ref, m_ref, acc_ref, sa_ref, sb_ref, *, t):
    qi = pl.program_id(1)
    q = q_ref[...]
    m_ref[...] = jnp.full(m_ref.shape, NEG_INF, F32)
    acc_ref[...] = jnp.zeros(acc_ref.shape, F32)

    def scores(j, dst):
        k = k_ref[pl.ds(pl.multiple_of(j * t, t), t), :]
        dst[...] = lax.dot_general(q, k, (((1,), (1,)), ((), ())), preferred_element_type=F32)

    def absorb(j, src, masked):
        s = src[...]
        if masked:
            r = lax.broadcasted_iota(I32, (t, t), 0)
            cidx = lax.broadcasted_iota(I32, (t, t), 1)
            s = jnp.where(cidx <= r, s, NEG_INF)
        v = v_ref[pl.ds(pl.multiple_of(j * t, t), t), :]
        m_old = m_ref[...]
        m_new = jnp.maximum(m_old, jnp.max(s, axis=-1, keepdims=True))
        p = jnp.exp2(s - m_new).astype(BF16)
        acc_ref[...] = jnp.exp2(m_old - m_new) * acc_ref[...] + jnp.dot(p, v, preferred_element_type=F32)
        m_ref[...] = m_new

    scores(0, sa_ref)

    def pair(i, carry):
        j = 2 * i
        scores(j + 1, sb_ref)
        absorb(j, sa_ref, False)
        scores(j + 2, sa_ref)
        absorb(j + 1, sb_ref, False)
        return carry

    lax.fori_loop(0, qi // 2, pair, 0)

    @pl.when(qi % 2 == 0)
    def _():
        absorb(qi, sa_ref, True)

    @pl.when(qi % 2 == 1)
    def _():
        scores(qi, sb_ref)
        absorb(qi - 1, sa_ref, False)
        absorb(qi, sb_ref, True)

    acc = acc_ref[...]
    o_ref[...] = acc[:, :V_HEAD_DIM] / acc[:, V_HEAD_DIM:]


def _mla(q, k, v):
    s = q.shape[0]
    t = min(T_MLA, s)
    head_cols = lambda h, i: (0, h)
    return pl.pallas_call(
        functools.partial(_mla_body, t=t),
        grid=(N_HEADS_MLA, s // t),
        in_specs=[pl.BlockSpec((t, MLA_QK_PAD), lambda h, i: (i, h)),
                  pl.BlockSpec((s, MLA_QK_PAD), head_cols),
                  pl.BlockSpec((s, MLA_QK_PAD), head_cols)],
        out_specs=pl.BlockSpec((t, V_HEAD_DIM), lambda h, i: (i, h)),
        out_shape=jax.ShapeDtypeStruct((s, D_MLA), F32),
        scratch_shapes=[pltpu.VMEM((t, 1), F32), pltpu.VMEM((t, MLA_QK_PAD), F32),
                        pltpu.VMEM((t, t), F32), pltpu.VMEM((t, t), F32)],
        compiler_params=_params("parallel", "arbitrary"),
        name="mla",
    )(q, k, v)


def _dilated_body(q_ref, kc_ref, kp_ref, vc_ref, vp_ref, pq_ref, pkc_ref, pkp_ref, o_ref, lse_ref):
    n = pl.program_id(1)
    blk = SWA_BLOCK
    i = lax.broadcasted_iota(I32, (blk, blk), 0)
    j = lax.broadcasted_iota(I32, (blk, blk), 1)
    ok_cur = j <= i
    scale = float(HEAD_DIM_SWA) ** -0.5
    nt = (((1,), (1,)), ((), ()))
    ones = jnp.ones((blk, HEAD_DIM_SWA), BF16)
    rows = [slice(g * blk, (g + 1) * blk) for g in range(DIL_GROUP)]
    dist_cur, dist_prev, ok_prev, prev_of = [], [], [], []
    for g in range(DIL_GROUP):
        pq = pq_ref[rows[g], :]
        dist_cur.append(jnp.abs(pq - pkc_ref[0][:, rows[g]]))
        if g == 0:
            dist_prev.append(jnp.abs(pq - pkp_ref[0]))
            ok_prev.append((j >= i) & (n > 0))
            prev_of.append((kp_ref, vp_ref, slice(0, blk)))
        else:
            dist_prev.append(jnp.abs(pq - pkc_ref[0][:, rows[g - 1]]))
            ok_prev.append(j >= i)
            prev_of.append((kc_ref, vc_ref, rows[g - 1]))
    units = [(g, h) for g in range(DIL_GROUP) for h in range(N_HEADS_SWA)]
    hs = lambda h: slice(h * HEAD_DIM_SWA, (h + 1) * HEAD_DIM_SWA)
    slope = lambda h: 2.0 ** (-8.0 * (h + 1) / N_HEADS_SWA)
    sc = [jnp.where(ok_cur, lax.dot_general(q_ref[rows[g], hs(h)], kc_ref[rows[g], hs(h)], nt,
                                            preferred_element_type=F32) * scale - slope(h) * dist_cur[g], NEG_INF)
          for g, h in units]
    sp = [jnp.where(ok_prev[g], lax.dot_general(q_ref[rows[g], hs(h)], prev_of[g][0][prev_of[g][2], hs(h)], nt,
                                                preferred_element_type=F32) * scale - slope(h) * dist_prev[g], NEG_INF)
          for g, h in units]
    m = [jnp.max(jnp.maximum(a, b), axis=-1, keepdims=True) for a, b in zip(sc, sp)]
    pc = [jnp.exp(a - mm).astype(BF16) for a, mm in zip(sc, m)]
    pp = [jnp.exp(b - mm).astype(BF16) for b, mm in zip(sp, m)]
    acc = [jnp.dot(pc[u], jnp.concatenate([vc_ref[rows[g], hs(h)], ones], axis=1), preferred_element_type=F32)
           + jnp.dot(pp[u], jnp.concatenate([prev_of[g][1][prev_of[g][2], hs(h)], ones], axis=1),
                     preferred_element_type=F32)
           for u, (g, h) in enumerate(units)]
    for u, (g, h) in enumerate(units):
        den = acc[u][:, HEAD_DIM_SWA:]
        o_ref[rows[g], hs(h)] = acc[u][:, :HEAD_DIM_SWA] / den
        lse_ref[rows[g], hs(h)] = m[u] + jnp.log(den)


def _dilated(qkv_v, posf, dil):
    sd = qkv_v.shape[0]
    blk = SWA_BLOCK
    grp = DIL_GROUP * blk
    assert sd % grp == 0
    pos_v = posf.reshape(sd, dil)
    pq_v = jnp.repeat(pos_v, LANES, axis=1)
    pos_rows = pos_v.T.reshape(dil, 1, sd)
    prev = lambda n: jnp.maximum(DIL_GROUP * n - 1, 0)
    wide, one = (grp, D_SWA), (blk, D_SWA)
    return pl.pallas_call(
        _dilated_body,
        grid=(dil, sd // grp),
        in_specs=[pl.BlockSpec(wide, lambda r, n: (n, 3 * r)),
                  pl.BlockSpec(wide, lambda r, n: (n, 3 * r + 1)),
                  pl.BlockSpec(one, lambda r, n: (prev(n), 3 * r + 1)),
                  pl.BlockSpec(wide, lambda r, n: (n, 3 * r + 2)),
                  pl.BlockSpec(one, lambda r, n: (prev(n), 3 * r + 2)),
                  pl.BlockSpec((grp, LANES), lambda r, n: (n, r)),
                  pl.BlockSpec((1, 1, grp), lambda r, n: (r, 0, n)),
                  pl.BlockSpec((1, 1, blk), lambda r, n: (r, 0, prev(n)))],
        out_specs=[pl.BlockSpec(wide, lambda r, n: (n, r)), pl.BlockSpec(wide, lambda r, n: (n, r))],
        out_shape=[jax.ShapeDtypeStruct((sd, dil * D_SWA), F32), jax.ShapeDtypeStruct((sd, dil * D_SWA), F32)],
        compiler_params=_params("parallel", "parallel"),
        name=f"dil{dil}",
    )(qkv_v, qkv_v, qkv_v, qkv_v, qkv_v, pq_v, pos_rows, pos_rows)


def _outproj_body(*refs, tm, dils):
    npat = len(dils)
    o_views, l_views = refs[:npat], refs[npat:2 * npat]
    (ob_ref, x_ref, gsw_ref, gml_ref, wo_ref, ga_ref, nfg_ref, scf_ref, shf_ref, wrt_ref,
     x1_ref, h2p_ref, lgt_ref) = refs[2 * npat:2 * npat + 13]
    scratch = list(refs[2 * npat + 13:])

    chunks = range(D_SWA // LANES)

    def token_order(view_ref, dil):
        if dil == 1:
            return lambda rs: view_ref[rs, :]
        nat_ref = scratch.pop(0)
        for r in range(dil):
            for c in chunks:
                b = r * D_SWA + c * LANES
                nat_ref[c, pl.ds(r, tm // dil, stride=dil), :] = view_ref[:, b:b + LANES]
        return lambda rs: jnp.concatenate([nat_ref[c, rs, :] for c in chunks], axis=1)

    o1, o2, o3 = [token_order(v, dil) for v, dil in zip(o_views, dils)]
    l1f, l2f, l3f = [token_order(v, dil) for v, dil in zip(l_views, dils)]

    nsub = OUTPROJ_SUBTILES
    r = tm // nsub
    subs = range(nsub)
    rows = [slice(i * r, (i + 1) * r) for i in subs]

    def merged(rs):
        l1, l2, l3 = l1f(rs), l2f(rs), l3f(rs)
        m = jnp.maximum(jnp.maximum(l1, l2), l3)
        e1, e2, e3 = jnp.exp(l1 - m), jnp.exp(l2 - m), jnp.exp(l3 - m)
        return (e1 * o1(rs) + e2 * o2(rs) + e3 * o3(rs)) / (e1 + e2 + e3)

    mix = [jnp.concatenate([_rms(merged(rs), gsw_ref[...]), _rms(ob_ref[rs, :], gml_ref[...])],
                           axis=-1).astype(BF16) for rs in rows]
    proj = [jnp.dot(mix[i], wo_ref[...], preferred_element_type=F32) for i in subs]
    x1 = [x_ref[rows[i], :] + ga_ref[...] * proj[i] for i in subs]
    h2 = [_rms(x1[i], nfg_ref[...]) * (1.0 + scf_ref[...]) + shf_ref[...] for i in subs]
    for i in subs:
        x1_ref[rows[i], :] = x1[i]
        lgt_ref[:, rows[i]] = lax.dot_general(wrt_ref[...], h2[i], (((1,), (1,)), ((), ())),
                                              precision=lax.Precision.HIGHEST, preferred_element_type=F32)
    for i in subs:
        for s in range(PACK_ROWS):
            b = 2 * LANES * s
            h2p_ref[pl.ds(i * r * PACK_ROWS + s, r, stride=PACK_ROWS), :] = _pack_bf16_pairs(
                h2[i][:, b:b + LANES], h2[i][:, b + LANES:b + 2 * LANES])


def _outproj(o_pats, lse_pats, dils, o_b, x, g_sw, g_ml, w_o, gate_a, nfg, scale_f, shift_f, w_router_t):
    s, d = x.shape
    tm = min(TM_OUTPROJ, s)
    row = lambda i: (i, 0)
    views = [pl.BlockSpec((tm // dil, dil * D_SWA), row) for dil in dils]
    vec = lambda n: pl.BlockSpec((1, n), lambda i: (0, 0))
    n_reordered = 2 * sum(1 for dil in dils if dil > 1)
    return pl.pallas_call(
        functools.partial(_outproj_body, tm=tm, dils=dils),
        grid=(s // tm,),
        in_specs=views + views + [pl.BlockSpec((tm, D_MLA), row), pl.BlockSpec((tm, d), row), vec(D_SWA),
                                  vec(D_MLA), _resident(w_o.shape), vec(d), vec(d), vec(d), vec(d),
                                  _resident(w_router_t.shape)],
        out_specs=[pl.BlockSpec((tm, d), row), pl.BlockSpec((tm * PACK_ROWS, LANES), row),
                   pl.BlockSpec((N_EXPERTS, tm), lambda i: (0, i))],
        out_shape=[jax.ShapeDtypeStruct((s, d), F32), jax.ShapeDtypeStruct((s * PACK_ROWS, LANES), I32),
                   jax.ShapeDtypeStruct((N_EXPERTS, s), F32)],
        scratch_shapes=[pltpu.VMEM((D_SWA // LANES, tm, LANES), F32)] * n_reordered,
        compiler_params=_params("parallel"),
        name="outproj",
    )(*o_pats, *lse_pats, o_b, x, g_sw, g_ml, w_o, gate_a, nfg, scale_f, shift_f, w_router_t)


def _first_index(hit_value, x, iota, size, axis):
    return jnp.min(jnp.where(x == hit_value, iota, size), axis=axis, keepdims=True)


def _route_body(lgt_ref, bias_ref, tri_ref, eidx_ref, wts_ref, rank_ref, cnt_ref, carry_ref, *, tn):
    @pl.when(pl.program_id(0) == 0)
    def _():
        carry_ref[...] = jnp.zeros(carry_ref.shape, F32)

    gsz = N_EXPERTS // N_GROUPS
    scores = jax.nn.sigmoid(lgt_ref[...])
    choice = scores + bias_ref[...]
    neg = jnp.float32(-jnp.inf)

    g3 = choice.reshape(N_GROUPS, gsz, tn)
    i3 = lax.broadcasted_iota(I32, g3.shape, 1)
    m1 = jnp.max(g3, axis=1, keepdims=True)
    f1 = _first_index(m1, g3, i3, gsz, 1)
    m2 = jnp.max(jnp.where(i3 == f1, neg, g3), axis=1, keepdims=True)
    gs = (m1 + m2).reshape(N_GROUPS, tn)

    ig = lax.broadcasted_iota(I32, gs.shape, 0)
    gsel = jnp.zeros(gs.shape, F32)
    for _ in range(TOPK_GROUPS):
        hit = ig == _first_index(jnp.max(gs, axis=0, keepdims=True), gs, ig, N_GROUPS, 0)
        gsel = jnp.where(hit, 1.0, gsel)
        gs = jnp.where(hit, neg, gs)
    emask = jnp.broadcast_to(gsel.reshape(N_GROUPS, 1, tn), (N_GROUPS, gsz, tn)).reshape(N_EXPERTS, tn)
    cand = jnp.where(emask > 0.0, choice, NEG_INF)

    ie = lax.broadcasted_iota(I32, cand.shape, 0)
    picks, wsel = [], []
    onehot = jnp.zeros(cand.shape, F32)
    for _ in range(TOP_K):
        f = _first_index(jnp.max(cand, axis=0, keepdims=True), cand, ie, N_EXPERTS, 0)
        hit = ie == f
        picks.append(f)
        wsel.append(jnp.sum(jnp.where(hit, scores, 0.0), axis=0, keepdims=True))
        onehot = jnp.where(hit, 1.0, onehot)
        cand = jnp.where(hit, neg, cand)

    rank = carry_ref[...] + jnp.dot(onehot.astype(BF16), tri_ref[...], preferred_element_type=F32)
    carry_ref[...] = carry_ref[...] + jnp.sum(onehot, axis=1, keepdims=True)
    cnt_ref[...] = carry_ref[...]

    w = jnp.concatenate(wsel, axis=0)
    wts_ref[...] = w / jnp.sum(w, axis=0, keepdims=True) * ROUTED_SCALE
    eidx_ref[...] = jnp.concatenate(picks, axis=0)
    rank_ref[...] = jnp.concatenate(
        [jnp.sum(jnp.where(ie == f, rank, 0.0), axis=0, keepdims=True) for f in picks], axis=0).astype(I32)


def _route(logits_t, router_bias):
    e, n = logits_t.shape
    tn = min(TN_ROUTE, n)
    tri = (lax.broadcasted_iota(I32, (tn, tn), 0) < lax.broadcasted_iota(I32, (tn, tn), 1)).astype(BF16)
    col = lambda i: (0, i)
    return pl.pallas_call(
        functools.partial(_route_body, tn=tn),
        grid=(n // tn,),
        in_specs=[pl.BlockSpec((e, tn), col), pl.BlockSpec((e, 1), lambda i: (0, 0)),
                  pl.BlockSpec((tn, tn), lambda i: (0, 0))],
        out_specs=[pl.BlockSpec((TOP_K, tn), col), pl.BlockSpec((TOP_K, tn), col), pl.BlockSpec((TOP_K, tn), col),
                   pl.BlockSpec((e, 1), lambda i: (0, 0))],
        out_shape=[jax.ShapeDtypeStruct((TOP_K, n), I32), jax.ShapeDtypeStruct((TOP_K, n), F32),
                   jax.ShapeDtypeStruct((TOP_K, n), I32), jax.ShapeDtypeStruct((e, 1), F32)],
        scratch_shapes=[pltpu.VMEM((e, 1), F32)],
        compiler_params=_params("arbitrary"),
        name="route",
    )(logits_t, router_bias.reshape(e, 1), tri)


def _dispatch_body(tail_start_ref, tail_len_ref, dest_ref, h_ref, xs_ref, zeros_ref, sem, zsem, *, tb, tr):
    sizes = [1 << b for b in reversed(range((tr - 1).bit_length()))]

    def tail_copies(e, act):
        start, length = tail_start_ref[e], tail_len_ref[e]
        done = jnp.int32(0)
        for size in sizes:
            piece = (length & size) != 0
            dst = xs_ref.at[pl.ds(pl.multiple_of((start + done) * PACK_ROWS, PACK_ROWS), size * PACK_ROWS), :]
            copy = pltpu.make_async_copy(zeros_ref.at[pl.ds(0, size * PACK_ROWS), :], dst, zsem)
            pl.when(piece)(functools.partial(act, copy))
            done = done + (length & size)

    @pl.when(pl.program_id(0) == 0)
    def _():
        zeros_ref[...] = jnp.zeros(zeros_ref.shape, I32)

        def start_tail(e, carry):
            tail_copies(e, lambda copy: copy.start())
            return carry

        def wait_tail(e, carry):
            tail_copies(e, lambda copy: copy.wait())
            return carry

        lax.fori_loop(0, N_EXPERTS, start_tail, 0)
        lax.fori_loop(0, N_EXPERTS, wait_tail, 0)

    def row_copy(t, k):
        src = h_ref.at[pl.ds(pl.multiple_of(t * PACK_ROWS, PACK_ROWS), PACK_ROWS), :]
        dst = xs_ref.at[pl.ds(pl.multiple_of(dest_ref[t * TOP_K + k] * PACK_ROWS, PACK_ROWS), PACK_ROWS), :]
        return pltpu.make_async_copy(src, dst, sem)

    def issue(t, carry):
        for k in range(TOP_K):
            row_copy(t, k).start(priority=k % 2)
        return carry

    def drain(t, carry):
        for k in range(TOP_K):
            row_copy(t, k).wait()
        return carry

    lax.fori_loop(0, tb, issue, 0)
    lax.fori_loop(0, tb, drain, 0)


def _dispatch(dest, h2p, tail_start, tail_len, n_slots, tr):
    n = dest.shape[0] // TOP_K
    tb = min(TB_DISPATCH, n)
    max_piece = 1 << ((tr - 1).bit_length() - 1)
    grid_spec = pltpu.PrefetchScalarGridSpec(
        num_scalar_prefetch=2,
        grid=(n // tb,),
        in_specs=[pl.BlockSpec((tb * TOP_K,), lambda i, ts, tl: (i,), memory_space=pltpu.SMEM),
                  pl.BlockSpec((tb * PACK_ROWS, LANES), lambda i, ts, tl: (i, 0))],
        out_specs=pl.BlockSpec(memory_space=pl.ANY),
        scratch_shapes=[pltpu.VMEM((max_piece * PACK_ROWS, LANES), I32),
                        pltpu.SemaphoreType.DMA(()), pltpu.SemaphoreType.DMA(())],
    )
    return pl.pallas_call(
        functools.partial(_dispatch_body, tb=tb, tr=tr),
        grid_spec=grid_spec,
        out_shape=jax.ShapeDtypeStruct((n_slots * PACK_ROWS, LANES), I32),
        compiler_params=_params("arbitrary"),
        name="dispatch",
    )(tail_start, tail_len, dest, h2p)


def _swiglu_packed(xp_ref, wgu_ref, wd_ref, rows, nsub=1):
    r = rows // nsub
    subs = range(nsub)
    x = [jnp.concatenate([_unpack_bf16_pairs(xp_ref[pl.ds(i * r * PACK_ROWS + s, r, stride=PACK_ROWS), :])
                          for s in range(PACK_ROWS)], axis=1) for i in subs]
    h = [jnp.dot(x[i], wgu_ref[...], preferred_element_type=F32) for i in subs]
    a = [(h[i][:, :D_EXPERT] * jax.nn.sigmoid(h[i][:, :D_EXPERT]) * h[i][:, D_EXPERT:]).astype(BF16) for i in subs]
    return [jnp.dot(a[i], wd_ref[...], preferred_element_type=F32) for i in subs]


def _experts_body(be_ref, nv_ref, new_ref, par_ref, nxt_ref, xs_ref, wg_ref, wu_ref, wd_ref, y_ref,
                  wg_st, wu_st, wd_st, wgu_s, wd_s, sems, *, tr):
    b = pl.program_id(0)

    def weight_copies(e, slot):
        return [pltpu.make_async_copy(src.at[e], dst.at[slot], sems.at[slot])
                for src, dst in ((wg_ref, wg_st), (wu_ref, wu_st), (wd_ref, wd_st))]

    @pl.when(b == 0)
    def _():
        for copy in weight_copies(be_ref[0], par_ref[0]):
            copy.start()

    @pl.when(new_ref[b] > 0)
    def _():
        slot = par_ref[b]
        for copy in weight_copies(be_ref[b], slot):
            copy.wait()
        wgu_s[:, :D_EXPERT] = wg_st[slot].astype(BF16)
        wgu_s[:, D_EXPERT:] = wu_st[slot].astype(BF16)
        wd_s[...] = wd_st[slot].astype(BF16)

        @pl.when(nxt_ref[b] >= 0)
        def _():
            for copy in weight_copies(nxt_ref[b], 1 - slot):
                copy.start()

    nsub = EXPERT_SUBTILES
    r = tr // nsub
    nv = nv_ref[b]

    def run(live):
        ys = _swiglu_packed(xs_ref, wgu_s, wd_s, live * r, live)
        for i, y in enumerate(ys):
            for s in range(PACK_ROWS):
                c = 2 * LANES * s
                y_ref[pl.ds(i * r * PACK_ROWS + s, r, stride=PACK_ROWS), :] = _pack_bf16_pairs(
                    y[:, c:c + LANES], y[:, c + LANES:c + 2 * LANES])
        if live < nsub:
            y_ref[live * r * PACK_ROWS:, :] = jnp.zeros(((nsub - live) * r * PACK_ROWS, LANES), I32)

    for live in range(1, nsub + 1):
        pl.when((nv > (live - 1) * r) & (nv <= live * r))(functools.partial(run, live))

    @pl.when(nv == 0)
    def _():
        y_ref[...] = jnp.zeros(y_ref.shape, I32)


def _experts(blk_e, blk_nv, blk_new, blk_par, blk_nxt, xs, w_gate, w_up, w_down):
    tr = TR_EXPERT
    nb = blk_e.shape[0]
    d, f = w_gate.shape[1], w_gate.shape[2]
    any_space = pl.BlockSpec(memory_space=pl.ANY)
    grid_spec = pltpu.PrefetchScalarGridSpec(
        num_scalar_prefetch=5,
        grid=(nb,),
        in_specs=[pl.BlockSpec((tr * PACK_ROWS, LANES), lambda b, be, nv, *_: (jnp.where(nv[b] > 0, b, 0), 0)),
                  any_space, any_space, any_space],
        out_specs=pl.BlockSpec((tr * PACK_ROWS, LANES), lambda b, *_: (b, 0)),
        scratch_shapes=[pltpu.VMEM((2, d, f), F32), pltpu.VMEM((2, d, f), F32), pltpu.VMEM((2, f, d), F32),
                        pltpu.VMEM((d, 2 * f), BF16), pltpu.VMEM((f, d), BF16), pltpu.SemaphoreType.DMA((2,))],
    )
    return pl.pallas_call(
        functools.partial(_experts_body, tr=tr),
        grid_spec=grid_spec,
        out_shape=jax.ShapeDtypeStruct((nb * tr * PACK_ROWS, LANES), I32),
        compiler_params=_params("arbitrary"),
        name="experts",
    )(blk_e, blk_nv, blk_new, blk_par, blk_nxt, xs, w_gate, w_up, w_down)


def _combine_body(dest_ref, dest_next_ref, wts_ref, h2p_ref, x1_ref, gf_ref, fg_ref, wsgu_ref, wsd_ref, y_ref,
                  out_ref, buf_a, buf_b, sem_a, sem_b, *, tb, nsteps):
    i = pl.program_id(0)
    buf_rows = TOP_K * tb * PACK_ROWS

    def row_copy(slots_ref, t, k, buf, sem):
        src = y_ref.at[pl.ds(pl.multiple_of(slots_ref[t * TOP_K + k] * PACK_ROWS, PACK_ROWS), PACK_ROWS), :]
        dst = buf.at[pl.ds(pl.multiple_of((k * tb + t) * PACK_ROWS, PACK_ROWS), PACK_ROWS), :]
        return pltpu.make_async_copy(src, dst, sem)

    def wait_block(buf, sem):
        pltpu.make_async_copy(y_ref.at[pl.ds(0, buf_rows), :], buf, sem).wait()

    @pl.when(i == 0)
    def _():
        def issue(t, carry):
            for k in range(TOP_K):
                row_copy(dest_ref, t, k, buf_a, sem_a).start(priority=k % 2)
            return carry

        lax.fori_loop(0, tb, issue, 0)

    def step(buf, sem, next_buf, next_sem):
        wait_block(buf, sem)
        for t in range(tb):
            for k in range(TOP_K):
                row_copy(dest_next_ref, t, k, next_buf, next_sem).start(priority=k % 2)
        shared = _swiglu_packed(h2p_ref, wsgu_ref, wsd_ref, tb)[0]
        wts = wts_ref[...]
        wb = [jnp.broadcast_to(wts[:, k:k + 1], (tb, LANES)) for k in range(TOP_K)]
        cols = []
        for s in range(PACK_ROWS):
            c = 2 * LANES * s
            lo, hi = shared[:, c:c + LANES], shared[:, c + LANES:c + 2 * LANES]
            for k in range(TOP_K):
                w = buf[pl.ds(k * tb * PACK_ROWS + s, tb, stride=PACK_ROWS), :]
                lo = lo + wb[k] * lax.bitcast_convert_type(lax.shift_left(w, jnp.int32(16)), F32)
                hi = hi + wb[k] * lax.bitcast_convert_type(w & jnp.int32(-65536), F32)
            cols += [lo, hi]
        moe = jnp.concatenate(cols, axis=1)
        out_ref[...] = _rms(x1_ref[...] + gf_ref[...] * moe, fg_ref[...])

        @pl.when(i == nsteps - 1)
        def _():
            wait_block(next_buf, next_sem)

    pl.when(i % 2 == 0)(functools.partial(step, buf_a, sem_a, buf_b, sem_b))
    pl.when(i % 2 == 1)(functools.partial(step, buf_b, sem_b, buf_a, sem_a))


def _combine(dest, wts_t, h2p, x1, gate_f, final_g, w_sgu, w_sd, y):
    n, d = x1.shape
    tb = min(TB_COMBINE, n)
    row = lambda i: (i, 0)
    vec = pl.BlockSpec((1, d), lambda i: (0, 0))
    nsteps = n // tb
    buf = pltpu.VMEM((TOP_K * tb * PACK_ROWS, LANES), I32)
    return pl.pallas_call(
        functools.partial(_combine_body, tb=tb, nsteps=nsteps),
        grid=(nsteps,),
        in_specs=[pl.BlockSpec((tb * TOP_K,), lambda i: (i,), memory_space=pltpu.SMEM),
                  pl.BlockSpec((tb * TOP_K,), lambda i: (jnp.minimum(i + 1, nsteps - 1),), memory_space=pltpu.SMEM),
                  pl.BlockSpec((tb, TOP_K), row),
                  pl.BlockSpec((tb * PACK_ROWS, LANES), row),
                  pl.BlockSpec((tb, d), row), vec, vec,
                  _resident(w_sgu.shape), _resident(w_sd.shape),
                  pl.BlockSpec(memory_space=pl.ANY)],
        out_specs=pl.BlockSpec((tb, d), row),
        out_shape=jax.ShapeDtypeStruct((n, d), F32),
        scratch_shapes=[buf, buf, pltpu.SemaphoreType.DMA(()), pltpu.SemaphoreType.DMA(())],
        compiler_params=_params("arbitrary"),
        name="combine",
    )(dest, dest, wts_t, h2p, x1, gate_f, final_g, w_sgu, w_sd, y)


def _rope_tables(pos):
    half = QK_ROPE_DIM // 2
    inv_freq = ROPE_THETA ** (-jnp.arange(half, dtype=F32) / half)
    ang = pos.astype(F32)[:, None] * inv_freq
    cos, sin = jnp.cos(ang), jnp.sin(ang)
    z = jnp.zeros_like(cos)
    c = jnp.concatenate([cos, cos, z, z], axis=1)
    s1 = jnp.concatenate([z, sin, z, z], axis=1)
    s2 = jnp.concatenate([-sin, z, z, z], axis=1)
    return c, s1, s2


def _layer(x, c, pos, norm_attn_g, w_ada, b_ada, w_in, g_q, w_uq, g_kv, w_ukv, g_out_swa, g_out_mla, w_o,
           norm_ffn_g, w_router, router_bias, w_exp_gate, w_exp_up, w_exp_down, w_sh_gate, w_sh_up, w_sh_down,
           final_g):
    s, d = x.shape
    row = lambda a: a.reshape(1, -1)

    mod = _ada(c, w_ada, b_ada)
    shift_a, scale_a, gate_a, shift_f, scale_f, gate_f = [mod[:, i * d:(i + 1) * d] for i in range(N_ADA)]

    n_qkv = 3 * D_SWA
    w_qkv = w_in[:, :n_qkv].astype(BF16)
    w_rest = jnp.pad(w_in[:, n_qkv:], ((0, 0), (0, LANES - QK_ROPE_DIM))).astype(BF16)
    dils = tuple(dil for _, dil in SWA_PATTERNS)
    assert all(window // dil == SWA_BLOCK and s % (dil * SWA_BLOCK) == 0 for window, dil in SWA_PATTERNS)
    rest, qkv_views = _inproj(x, row(norm_attn_g), scale_a, shift_a, w_qkv, w_rest, dils)

    dq = QK_NOPE_DIM + QK_ROPE_DIM
    w_uq_p = jnp.pad(w_uq.reshape(Q_LORA_RANK, N_HEADS_MLA, dq), ((0, 0), (0, 0), (0, MLA_QK_PAD - dq)))
    w_uq_p = w_uq_p.reshape(Q_LORA_RANK, N_HEADS_MLA * MLA_QK_PAD).astype(BF16)
    rc, rs1, rs2 = _rope_tables(pos)
    q_m, k_m, v_m = _mlaproj(rest, row(g_q), row(g_kv), w_uq_p, w_ukv.astype(BF16), rc, rs1, rs2)
    o_b = _mla(q_m, k_m, v_m)

    posf = pos.astype(F32)
    o_pats, lse_pats = zip(*[_dilated(qkv_v, posf, dil) for qkv_v, dil in zip(qkv_views, dils)])

    x1, h2p, logits_t = _outproj(o_pats, lse_pats, dils, o_b, x, row(g_out_swa), row(g_out_mla),
                                 w_o.astype(BF16), gate_a, row(norm_ffn_g), scale_f, shift_f, w_router.T)

    eidx, wts, rank, cnt = _route(logits_t, router_bias)
    tr = TR_EXPERT
    counts = cnt[:, 0].astype(I32)
    padded = (counts + tr - 1) // tr * tr
    e_ids = jnp.arange(N_EXPERTS, dtype=I32)
    pad_end = jnp.sum(jnp.where(e_ids[None, :] <= e_ids[:, None], padded[None, :], 0), axis=1)
    pad_start = pad_end - padded
    lookup = lambda table, idx: jnp.sum(jnp.where(idx[..., None] == e_ids, table, 0), axis=-1)
    dest = (lookup(pad_start, eidx) + rank).T.reshape(-1)
    n_slots = s * TOP_K + N_EXPERTS * tr
    blk_start = jnp.arange(n_slots // tr, dtype=I32) * tr
    blk_e = jnp.minimum(jnp.sum((pad_end[None, :] <= blk_start[:, None]).astype(I32), axis=1), N_EXPERTS - 1)
    blk_nv = jnp.clip(lookup(counts, blk_e) - (blk_start - lookup(pad_start, blk_e)), 0, tr)
    blk_new = ((blk_nv > 0) & (blk_start == lookup(pad_start, blk_e))).astype(I32)
    used = counts > 0
    later_used = used[None, :] & (e_ids[None, :] > e_ids[:, None])
    next_used = jnp.min(jnp.where(later_used, e_ids[None, :], N_EXPERTS), axis=1)
    next_used = jnp.where(next_used < N_EXPERTS, next_used, -1)
    used_before = jnp.sum((used[None, :] & (e_ids[None, :] < e_ids[:, None])).astype(I32), axis=1)
    blk_par = lookup(used_before % 2, blk_e)
    blk_nxt = lookup(next_used, blk_e)

    xs = _dispatch(dest, h2p, pad_start + counts, padded - counts, n_slots, tr)
    y = _experts(blk_e, blk_nv, blk_new, blk_par, blk_nxt, xs, w_exp_gate, w_exp_up, w_exp_down)
    w_sgu = jnp.concatenate([w_sh_gate, w_sh_up], axis=1).astype(BF16)
    return _combine(dest, wts.T, h2p, x1, gate_f, row(final_g), w_sgu, w_sh_down.astype(BF16), y)


def kernel(x, c, positions, norm_attn_g, w_ada, b_ada, w_in, g_q, w_uq, g_kv, w_ukv, g_out_swa, g_out_mla, w_o,
           norm_ffn_g, w_router, router_bias, w_exp_gate, w_exp_up, w_exp_down, w_sh_gate, w_sh_up, w_sh_down,
           final_norm_g):
    assert x.shape[0] == 1 and w_ada.shape[0] == 1
    out = _layer(x[0], c[0], positions[0], norm_attn_g[0], w_ada[0], b_ada[0], w_in[0], g_q[0], w_uq[0], g_kv[0],
                 w_ukv[0], g_out_swa[0], g_out_mla[0], w_o[0], norm_ffn_g[0], w_router[0], router_bias[0],
                 w_exp_gate[0], w_exp_up[0], w_exp_down[0], w_sh_gate[0], w_sh_up[0], w_sh_down[0], final_norm_g)
    return out[None]
```

```python
import functools

import jax
import jax.numpy as jnp
from jax import lax
from jax.experimental import pallas as pl
from jax.experimental.pallas import tpu as pltpu

F32 = jnp.float32
BF16 = jnp.bfloat16
I32 = jnp.int32

D_MODEL = 2048
N_HEADS_SWA = 8
HEAD_DIM_SWA = 128
SWA_PATTERNS = ((128, 1), (512, 4), (2048, 16))
SWA_BLOCK = 128
N_HEADS_MLA = 8
Q_LORA_RANK = 512
KV_LORA_RANK = 256
QK_NOPE_DIM = 128
QK_ROPE_DIM = 64
V_HEAD_DIM = 128
ROPE_THETA = 10000.0
D_SWA = N_HEADS_SWA * HEAD_DIM_SWA
D_MLA = N_HEADS_MLA * V_HEAD_DIM
N_EXPERTS = 64
N_GROUPS = 8
TOPK_GROUPS = 4
TOP_K = 8
D_EXPERT = 512
ROUTED_SCALE = 2.5
N_ADA = 6
EPS = 1e-6
NEG_INF = -1e30
LOG2E = 1.4426950408889634

LANES = 128
MLA_QK_PAD = 256
PACK_ROWS = D_MODEL // (2 * LANES)
V7X_VMEM_BYTES = 64 * 1024 * 1024
VMEM_LIMIT = V7X_VMEM_BYTES // 8 * 7

TN_ADA = 512
TM_INPROJ = 256
TM_MLAPROJ = 512
T_MLA = 1024
DIL_GROUP = 4
TM_OUTPROJ = 256
OUTPROJ_SUBTILES = 2
TN_ROUTE = 512
TB_DISPATCH = 256
TR_EXPERT = 512
EXPERT_SUBTILES = 2
TB_COMBINE = 256


def _params(*sem):
    return pltpu.CompilerParams(dimension_semantics=sem, vmem_limit_bytes=VMEM_LIMIT)


def _rms(x, g):
    return x * lax.rsqrt(jnp.mean(x * x, axis=-1, keepdims=True) + EPS) * g


def _resident(shape):
    nd = len(shape)
    return pl.BlockSpec(shape, lambda *_: (0,) * nd, pipeline_mode=pl.Buffered(1))


BF16_BITS = 16
HIGH_HALF = -(1 << BF16_BITS)


def _pack_bf16_pairs(a, b):
    ua = lax.bitcast_convert_type(a.astype(BF16).astype(F32), I32)
    ub = lax.bitcast_convert_type(b.astype(BF16).astype(F32), I32)
    return lax.shift_right_logical(ua, jnp.int32(BF16_BITS)) | (ub & jnp.int32(HIGH_HALF))


def _unpack_f32_pairs(w):
    lo = lax.bitcast_convert_type(lax.shift_left(w, jnp.int32(BF16_BITS)), F32)
    hi = lax.bitcast_convert_type(w & jnp.int32(HIGH_HALF), F32)
    return lo, hi


def _unpack_bf16_pairs(w):
    lo, hi = _unpack_f32_pairs(w)
    return jnp.concatenate([lo.astype(BF16), hi.astype(BF16)], axis=1)


def _ada_body(c_ref, w_ref, b_ref, o_ref):
    c = c_ref[...]
    a = c * jax.nn.sigmoid(c)
    o_ref[...] = jnp.sum(w_ref[...] * a, axis=0, keepdims=True) + b_ref[...]


def _ada(c, w_ada, b_ada):
    d, n = w_ada.shape
    tn = TN_ADA
    return pl.pallas_call(
        _ada_body,
        grid=(n // tn,),
        in_specs=[pl.BlockSpec((d, 1), lambda j: (0, 0)),
                  pl.BlockSpec((d, tn), lambda j: (0, j)),
                  pl.BlockSpec((1, tn), lambda j: (0, j))],
        out_specs=pl.BlockSpec((1, tn), lambda j: (0, j)),
        out_shape=jax.ShapeDtypeStruct((1, n), F32),
        compiler_params=_params("parallel"),
        name="ada",
    )(c.reshape(d, 1), w_ada, b_ada.reshape(1, n))


def _inproj_body(x_ref, g_ref, sc_ref, sh_ref, wqkv_ref, wr_ref, rest_ref, *out_and_scratch, tm, dils):
    view_refs, res_ref = out_and_scratch[:-1], out_and_scratch[-1]
    n = wqkv_ref.shape[1]
    h = (_rms(x_ref[...], g_ref[...]) * (1.0 + sc_ref[...]) + sh_ref[...]).astype(BF16)
    rest_ref[...] = jnp.dot(h, wr_ref[...], preferred_element_type=F32)
    res = jnp.dot(h, wqkv_ref[...], preferred_element_type=F32)
    chunks = range(n // LANES)
    for c in chunks:
        res_ref[c] = res[:, c * LANES:(c + 1) * LANES]
    for dil, v_ref in zip(dils, view_refs):
        if dil == 1:
            v_ref[...] = res.astype(BF16)
            continue
        for r in range(dil):
            for c in chunks:
                b = r * n + c * LANES
                v_ref[:, b:b + LANES] = res_ref[c, pl.ds(r, tm // dil, stride=dil), :].astype(BF16)


def _inproj(x, g, scale, shift, w_qkv, w_rest, dils):
    s, d = x.shape
    tm = min(TM_INPROJ, s)
    n1, n2 = w_qkv.shape[1], w_rest.shape[1]
    row = lambda i: (i, 0)
    vec = pl.BlockSpec((1, d), lambda i: (0, 0))
    outs = pl.pallas_call(
        functools.partial(_inproj_body, tm=tm, dils=dils),
        grid=(s // tm,),
        in_specs=[pl.BlockSpec((tm, d), row), vec, vec, vec, _resident((d, n1)), _resident((d, n2))],
        out_specs=[pl.BlockSpec((tm, n2), row)] + [pl.BlockSpec((tm // dil, dil * n1), row) for dil in dils],
        out_shape=[jax.ShapeDtypeStruct((s, n2), F32)]
        + [jax.ShapeDtypeStruct((s // dil, dil * n1), BF16) for dil in dils],
        scratch_shapes=[pltpu.VMEM((n1 // LANES, tm, LANES), F32)],
        compiler_params=_params("parallel"),
        name="inproj",
    )(x, g, scale, shift, w_qkv, w_rest)
    return outs[0], outs[1:]


def _rope_tail(t, c, s1, s2):
    half = QK_ROPE_DIM // 2
    return t * c + pltpu.roll(t, half, 1) * s1 + pltpu.roll(t, LANES - half, 1) * s2


def _mlaproj_body(rest_ref, gq_ref, gkv_ref, wuq_ref, wukv_ref, c_ref, s1_ref, s2_ref,
                  q_ref, k_ref, v_ref, *, scale):
    rest = rest_ref[...]
    c, s1, s2 = c_ref[...], s1_ref[...], s2_ref[...]
    cq = _rms(rest[:, :Q_LORA_RANK], gq_ref[...]).astype(BF16)
    ckv = _rms(rest[:, Q_LORA_RANK:Q_LORA_RANK + KV_LORA_RANK], gkv_ref[...]).astype(BF16)
    ktail = _rope_tail(rest[:, Q_LORA_RANK + KV_LORA_RANK:], c, s1, s2).astype(BF16)
    q = jnp.dot(cq, wuq_ref[...], preferred_element_type=F32)
    kv = jnp.dot(ckv, wukv_ref[...], preferred_element_type=F32)
    for h in range(N_HEADS_MLA):
        b = h * MLA_QK_PAD
        q_ref[:, b:b + LANES] = (q[:, b:b + LANES] * scale).astype(BF16)
        q_ref[:, b + LANES:b + 2 * LANES] = (_rope_tail(q[:, b + LANES:b + 2 * LANES], c, s1, s2) * scale).astype(BF16)
        k_ref[:, b:b + LANES] = kv[:, b:b + LANES].astype(BF16)
        k_ref[:, b + LANES:b + 2 * LANES] = ktail
        v_ref[:, b:b + LANES] = kv[:, b + LANES:b + 2 * LANES].astype(BF16)
        v_ref[:, b + LANES:b + 2 * LANES] = jnp.ones((q.shape[0], LANES), BF16)


def _mlaproj(rest, g_q, g_kv, w_uq, w_ukv, rc, rs1, rs2):
    s, nr = rest.shape
    tm = min(TM_MLAPROJ, s)
    nq = N_HEADS_MLA * MLA_QK_PAD
    row = lambda i: (i, 0)
    tab = pl.BlockSpec((tm, LANES), row)
    scale = float(QK_NOPE_DIM + QK_ROPE_DIM) ** -0.5 * LOG2E
    return pl.pallas_call(
        functools.partial(_mlaproj_body, scale=scale),
        grid=(s // tm,),
        in_specs=[pl.BlockSpec((tm, nr), row),
                  pl.BlockSpec((1, Q_LORA_RANK), lambda i: (0, 0)),
                  pl.BlockSpec((1, KV_LORA_RANK), lambda i: (0, 0)),
                  _resident(w_uq.shape), _resident(w_ukv.shape), tab, tab, tab],
        out_specs=[pl.BlockSpec((tm, nq), row)] * 3,
        out_shape=[jax.ShapeDtypeStruct((s, nq), BF16)] * 3,
        compiler_params=_params("parallel"),
        name="mlaproj",
    )(rest, g_q, g_kv, w_uq, w_ukv, rc, rs1, rs2)


def _mla_body(q_ref, k_ref, v_ref, o_ref, m_ref, acc_ref, sa_ref, sb_ref, *, t):
    qi = pl.program_id(1)
    q = q_ref[...]
    m_ref[...] = jnp.full(m_ref.shape, NEG_INF, F32)
    acc_ref[...] = jnp.zeros(acc_ref.shape, F32)

    def scores(j, dst):
        k = k_ref[pl.ds(pl.multiple_of(j * t, t), t), :]
        dst[...] = lax.dot_general(q, k, (((1,), (1,)), ((), ())), preferred_element_type=F32)

    def absorb(j, src, masked):
        s = src[...]
        if masked:
            r = lax.broadcasted_iota(I32, (t, t), 0)
            cidx = lax.broadcasted_iota(I32, (t, t), 1)
            s = jnp.where(cidx <= r, s, NEG_INF)
        v = v_ref[pl.ds(pl.multiple_of(j * t, t), t), :]
        m_old = m_ref[...]
        m_new = jnp.maximum(m_old, jnp.max(s, axis=-1, keepdims=True))
        p = jnp.exp2(s - m_new).astype(BF16)
        acc_ref[...] = jnp.exp2(m_old - m_new) * acc_ref[...] + jnp.dot(p, v, preferred_element_type=F32)
        m_ref[...] = m_new

    scores(0, sa_ref)

    def pair(i, carry):
        j = 2 * i
        scores(j + 1, sb_ref)
        absorb(j, sa_ref, False)
        scores(j + 2, sa_ref)
        absorb(j + 1, sb_ref, False)
        return carry

    lax.fori_loop(0, qi // 2, pair, 0)

    @pl.when(qi % 2 == 0)
    def _():
        absorb(qi, sa_ref, True)

    @pl.when(qi % 2 == 1)
    def _():
        scores(qi, sb_ref)
        absorb(qi - 1, sa_ref, False)
        absorb(qi, sb_ref, True)

    acc = acc_ref[...]
    o_ref[...] = acc[:, :V_HEAD_DIM] / acc[:, V_HEAD_DIM:]


def _mla(q, k, v):
    s = q.shape[0]
    t = min(T_MLA, s)
    head_cols = lambda h, i: (0, h)
    return pl.pallas_call(
        functools.partial(_mla_body, t=t),
        grid=(N_HEADS_MLA, s // t),
        in_specs=[pl.BlockSpec((t, MLA_QK_PAD), lambda h, i: (i, h)),
                  pl.BlockSpec((s, MLA_QK_PAD), head_cols),
                  pl.BlockSpec((s, MLA_QK_PAD), head_cols)],
        out_specs=pl.BlockSpec((t, V_HEAD_DIM), lambda h, i: (i, h)),
        out_shape=jax.ShapeDtypeStruct((s, D_MLA), F32),
        scratch_shapes=[pltpu.VMEM((t, 1), F32), pltpu.VMEM((t, MLA_QK_PAD), F32),
                        pltpu.VMEM((t, t), F32), pltpu.VMEM((t, t), F32)],
        compiler_params=_params("parallel", "arbitrary"),
        name="mla",
    )(q, k, v)


def _dilated_body(q_ref, kc_ref, kp_ref, vc_ref, vp_ref, pq_ref, pkc_ref, pkp_ref, o_ref, lse_ref):
    n = pl.program_id(1)
    blk = SWA_BLOCK
    i = lax.broadcasted_iota(I32, (blk, blk), 0)
    j = lax.broadcasted_iota(I32, (blk, blk), 1)
    ok_cur = j <= i
    scale = float(HEAD_DIM_SWA) ** -0.5
    nt = (((1,), (1,)), ((), ()))
    ones = jnp.ones((blk, HEAD_DIM_SWA), BF16)
    rows = [slice(g * blk, (g + 1) * blk) for g in range(DIL_GROUP)]
    dist_cur, dist_prev, ok_prev, prev_of = [], [], [], []
    for g in range(DIL_GROUP):
        pq = pq_ref[rows[g], :]
        dist_cur.append(jnp.abs(pq - pkc_ref[0][:, rows[g]]))
        if g == 0:
            dist_prev.append(jnp.abs(pq - pkp_ref[0]))
            ok_prev.append((j >= i) & (n > 0))
            prev_of.append((kp_ref, vp_ref, slice(0, blk)))
        else:
            dist_prev.append(jnp.abs(pq - pkc_ref[0][:, rows[g - 1]]))
            ok_prev.append(j >= i)
            prev_of.append((kc_ref, vc_ref, rows[g - 1]))
    units = [(g, h) for g in range(DIL_GROUP) for h in range(N_HEADS_SWA)]
    hs = lambda h: slice(h * HEAD_DIM_SWA, (h + 1) * HEAD_DIM_SWA)
    slope = lambda h: 2.0 ** (-8.0 * (h + 1) / N_HEADS_SWA)
    sc = [jnp.where(ok_cur, lax.dot_general(q_ref[rows[g], hs(h)], kc_ref[rows[g], hs(h)], nt,
                                            preferred_element_type=F32) * scale - slope(h) * dist_cur[g], NEG_INF)
          for g, h in units]
    sp = [jnp.where(ok_prev[g], lax.dot_general(q_ref[rows[g], hs(h)], prev_of[g][0][prev_of[g][2], hs(h)], nt,
                                                preferred_element_type=F32) * scale - slope(h) * dist_prev[g], NEG_INF)
          for g, h in units]
    m = [jnp.max(jnp.maximum(a, b), axis=-1, keepdims=True) for a, b in zip(sc, sp)]
    pc = [jnp.exp(a - mm).astype(BF16) for a, mm in zip(sc, m)]
    pp = [jnp.exp(b - mm).astype(BF16) for b, mm in zip(sp, m)]
    acc = [jnp.dot(pc[u], jnp.concatenate([vc_ref[rows[g], hs(h)], ones], axis=1), preferred_element_type=F32)
           + jnp.dot(pp[u], jnp.concatenate([prev_of[g][1][prev_of[g][2], hs(h)], ones], axis=1),
                     preferred_element_type=F32)
           for u, (g, h) in enumerate(units)]
    for u, (g, h) in enumerate(units):
        den = acc[u][:, HEAD_DIM_SWA:]
        o_ref[rows[g], hs(h)] = acc[u][:, :HEAD_DIM_SWA] / den
        lse_ref[rows[g], hs(h)] = m[u] + jnp.log(den)


def _dilated(qkv_v, posf, dil):
    sd = qkv_v.shape[0]
    blk = SWA_BLOCK
    grp = DIL_GROUP * blk
    assert sd % grp == 0
    pos_v = posf.reshape(sd, dil)
    pq_v = jnp.repeat(pos_v, LANES, axis=1)
    pos_rows = pos_v.T.reshape(dil, 1, sd)
    prev = lambda n: jnp.maximum(DIL_GROUP * n - 1, 0)
    wide, one = (grp, D_SWA), (blk, D_SWA)
    return pl.pallas_call(
        _dilated_body,
        grid=(dil, sd // grp),
        in_specs=[pl.BlockSpec(wide, lambda r, n: (n, 3 * r)),
                  pl.BlockSpec(wide, lambda r, n: (n, 3 * r + 1)),
                  pl.BlockSpec(one, lambda r, n: (prev(n), 3 * r + 1)),
                  pl.BlockSpec(wide, lambda r, n: (n, 3 * r + 2)),
                  pl.BlockSpec(one, lambda r, n: (prev(n), 3 * r + 2)),
                  pl.BlockSpec((grp, LANES), lambda r, n: (n, r)),
                  pl.BlockSpec((1, 1, grp), lambda r, n: (r, 0, n)),
                  pl.BlockSpec((1, 1, blk), lambda r, n: (r, 0, prev(n)))],
        out_specs=[pl.BlockSpec(wide, lambda r, n: (n, r)), pl.BlockSpec(wide, lambda r, n: (n, r))],
        out_shape=[jax.ShapeDtypeStruct((sd, dil * D_SWA), F32), jax.ShapeDtypeStruct((sd, dil * D_SWA), F32)],
        compiler_params=_params("parallel", "parallel"),
        name=f"dil{dil}",
    )(qkv_v, qkv_v, qkv_v, qkv_v, qkv_v, pq_v, pos_rows, pos_rows)


def _outproj_body(*refs, tm, dils):
    npat = len(dils)
    o_views, l_views = refs[:npat], refs[npat:2 * npat]
    (ob_ref, x_ref, gsw_ref, gml_ref, wo_ref, ga_ref, nfg_ref, scf_ref, shf_ref, wrt_ref,
     x1_ref, h2p_ref, lgt_ref) = refs[2 * npat:2 * npat + 13]
    scratch = list(refs[2 * npat + 13:])

    chunks = range(D_SWA // LANES)

    def token_order(view_ref, dil):
        if dil == 1:
            return lambda rs: view_ref[rs, :]
        nat_ref = scratch.pop(0)
        for r in range(dil):
            for c in chunks:
                b = r * D_SWA + c * LANES
                nat_ref[c, pl.ds(r, tm // dil, stride=dil), :] = view_ref[:, b:b + LANES]
        return lambda rs: jnp.concatenate([nat_ref[c, rs, :] for c in chunks], axis=1)

    o1, o2, o3 = [token_order(v, dil) for v, dil in zip(o_views, dils)]
    l1f, l2f, l3f = [token_order(v, dil) for v, dil in zip(l_views, dils)]

    nsub = OUTPROJ_SUBTILES
    r = tm // nsub
    subs = range(nsub)
    rows = [slice(i * r, (i + 1) * r) for i in subs]

    def merged(rs):
        l1, l2, l3 = l1f(rs), l2f(rs), l3f(rs)
        m = jnp.maximum(jnp.maximum(l1, l2), l3)
        e1, e2, e3 = jnp.exp(l1 - m), jnp.exp(l2 - m), jnp.exp(l3 - m)
        return (e1 * o1(rs) + e2 * o2(rs) + e3 * o3(rs)) / (e1 + e2 + e3)

    mix = [jnp.concatenate([_rms(merged(rs), gsw_ref[...]), _rms(ob_ref[rs, :], gml_ref[...])],
                           axis=-1).astype(BF16) for rs in rows]
    proj = [jnp.dot(mix[i], wo_ref[...], preferred_element_type=F32) for i in subs]
    x1 = [x_ref[rows[i], :] + ga_ref[...] * proj[i] for i in subs]
    h2 = [_rms(x1[i], nfg_ref[...]) * (1.0 + scf_ref[...]) + shf_ref[...] for i in subs]
    for i in subs:
        x1_ref[rows[i], :] = x1[i]
        lgt_ref[:, rows[i]] = lax.dot_general(wrt_ref[...], h2[i], (((1,), (1,)), ((), ())),
                                              precision=lax.Precision.HIGHEST, preferred_element_type=F32)
    for i in subs:
        for s in range(PACK_ROWS):
            b = 2 * LANES * s
            h2p_ref[pl.ds(i * r * PACK_ROWS + s, r, stride=PACK_ROWS), :] = _pack_bf16_pairs(
                h2[i][:, b:b + LANES], h2[i][:, b + LANES:b + 2 * LANES])


def _outproj(o_pats, lse_pats, dils, o_b, x, g_sw, g_ml, w_o, gate_a, nfg, scale_f, shift_f, w_router_t):
    s, d = x.shape
    tm = min(TM_OUTPROJ, s)
    row = lambda i: (i, 0)
    views = [pl.BlockSpec((tm // dil, dil * D_SWA), row) for dil in dils]
    vec = lambda n: pl.BlockSpec((1, n), lambda i: (0, 0))
    n_reordered = 2 * sum(1 for dil in dils if dil > 1)
    return pl.pallas_call(
        functools.partial(_outproj_body, tm=tm, dils=dils),
        grid=(s // tm,),
        in_specs=views + views + [pl.BlockSpec((tm, D_MLA), row), pl.BlockSpec((tm, d), row), vec(D_SWA),
                                  vec(D_MLA), _resident(w_o.shape), vec(d), vec(d), vec(d), vec(d),
                                  _resident(w_router_t.shape)],
        out_specs=[pl.BlockSpec((tm, d), row), pl.BlockSpec((tm * PACK_ROWS, LANES), row),
                   pl.BlockSpec((N_EXPERTS, tm), lambda i: (0, i))],
        out_shape=[jax.ShapeDtypeStruct((s, d), F32), jax.ShapeDtypeStruct((s * PACK_ROWS, LANES), I32),
                   jax.ShapeDtypeStruct((N_EXPERTS, s), F32)],
        scratch_shapes=[pltpu.VMEM((D_SWA // LANES, tm, LANES), F32)] * n_reordered,
        compiler_params=_params("parallel"),
        name="outproj",
    )(*o_pats, *lse_pats, o_b, x, g_sw, g_ml, w_o, gate_a, nfg, scale_f, shift_f, w_router_t)


def _first_index(hit_value, x, iota, size, axis):
    return jnp.min(jnp.where(x == hit_value, iota, size), axis=axis, keepdims=True)


def _route_body(lgt_ref, bias_ref, tri_ref, eidx_ref, wts_ref, rank_ref, cnt_ref, carry_ref, *, tn):
    @pl.when(pl.program_id(0) == 0)
    def _():
        carry_ref[...] = jnp.zeros(carry_ref.shape, F32)

    gsz = N_EXPERTS // N_GROUPS
    scores = jax.nn.sigmoid(lgt_ref[...])
    choice = scores + bias_ref[...]
    neg = jnp.float32(-jnp.inf)

    g3 = choice.reshape(N_GROUPS, gsz, tn)
    i3 = lax.broadcasted_iota(I32, g3.shape, 1)
    m1 = jnp.max(g3, axis=1, keepdims=True)
    f1 = _first_index(m1, g3, i3, gsz, 1)
    m2 = jnp.max(jnp.where(i3 == f1, neg, g3), axis=1, keepdims=True)
    gs = (m1 + m2).reshape(N_GROUPS, tn)

    ig = lax.broadcasted_iota(I32, gs.shape, 0)
    gsel = jnp.zeros(gs.shape, F32)
    for _ in range(TOPK_GROUPS):
        hit = ig == _first_index(jnp.max(gs, axis=0, keepdims=True), gs, ig, N_GROUPS, 0)
        gsel = jnp.where(hit, 1.0, gsel)
        gs = jnp.where(hit, neg, gs)
    emask = jnp.broadcast_to(gsel.reshape(N_GROUPS, 1, tn), (N_GROUPS, gsz, tn)).reshape(N_EXPERTS, tn)
    cand = jnp.where(emask > 0.0, choice, NEG_INF)

    ie = lax.broadcasted_iota(I32, cand.shape, 0)
    picks, wsel = [], []
    onehot = jnp.zeros(cand.shape, F32)
    for _ in range(TOP_K):
        f = _first_index(jnp.max(cand, axis=0, keepdims=True), cand, ie, N_EXPERTS, 0)
        hit = ie == f
        picks.append(f)
        wsel.append(jnp.sum(jnp.where(hit, scores, 0.0), axis=0, keepdims=True))
        onehot = jnp.where(hit, 1.0, onehot)
        cand = jnp.where(hit, neg, cand)

    rank = carry_ref[...] + jnp.dot(onehot.astype(BF16), tri_ref[...], preferred_element_type=F32)
    carry_ref[...] = carry_ref[...] + jnp.sum(onehot, axis=1, keepdims=True)
    cnt_ref[...] = carry_ref[...]

    w = jnp.concatenate(wsel, axis=0)
    wts_ref[...] = w / jnp.sum(w, axis=0, keepdims=True) * ROUTED_SCALE
    eidx_ref[...] = jnp.concatenate(picks, axis=0)
    rank_ref[...] = jnp.concatenate(
        [jnp.sum(jnp.where(ie == f, rank, 0.0), axis=0, keepdims=True) for f in picks], axis=0).astype(I32)


def _route(logits_t, router_bias):
    e, n = logits_t.shape
    tn = min(TN_ROUTE, n)
    tri = (lax.broadcasted_iota(I32, (tn, tn), 0) < lax.broadcasted_iota(I32, (tn, tn), 1)).astype(BF16)
    col = lambda i: (0, i)
    return pl.pallas_call(
        functools.partial(_route_body, tn=tn),
        grid=(n // tn,),
        in_specs=[pl.BlockSpec((e, tn), col), pl.BlockSpec((e, 1), lambda i: (0, 0)),
                  pl.BlockSpec((tn, tn), lambda i: (0, 0))],
        out_specs=[pl.BlockSpec((TOP_K, tn), col), pl.BlockSpec((TOP_K, tn), col), pl.BlockSpec((TOP_K, tn), col),
                   pl.BlockSpec((e, 1), lambda i: (0, 0))],
        out_shape=[jax.ShapeDtypeStruct((TOP_K, n), I32), jax.ShapeDtypeStruct((TOP_K, n), F32),
                   jax.ShapeDtypeStruct((TOP_K, n), I32), jax.ShapeDtypeStruct((e, 1), F32)],
        scratch_shapes=[pltpu.VMEM((e, 1), F32)],
        compiler_params=_params("arbitrary"),
        name="route",
    )(logits_t, router_bias.reshape(e, 1), tri)


def _dispatch_body(tail_start_ref, tail_len_ref, dest_ref, h_ref, xs_ref, zeros_ref, sem, zsem, *, tb, tr):
    sizes = [1 << b for b in reversed(range((tr - 1).bit_length()))]

    def tail_copies(e, act):
        start, length = tail_start_ref[e], tail_len_ref[e]
        done = jnp.int32(0)
        for size in sizes:
            piece = (length & size) != 0
            dst = xs_ref.at[pl.ds(pl.multiple_of((start + done) * PACK_ROWS, PACK_ROWS), size * PACK_ROWS), :]
            copy = pltpu.make_async_copy(zeros_ref.at[pl.ds(0, size * PACK_ROWS), :], dst, zsem)
            pl.when(piece)(functools.partial(act, copy))
            done = done + (length & size)

    @pl.when(pl.program_id(0) == 0)
    def _():
        zeros_ref[...] = jnp.zeros(zeros_ref.shape, I32)

        def start_tail(e, carry):
            tail_copies(e, lambda copy: copy.start())
            return carry

        def wait_tail(e, carry):
            tail_copies(e, lambda copy: copy.wait())
            return carry

        lax.fori_loop(0, N_EXPERTS, start_tail, 0)
        lax.fori_loop(0, N_EXPERTS, wait_tail, 0)

    def row_copy(t, k):
        src = h_ref.at[pl.ds(pl.multiple_of(t * PACK_ROWS, PACK_ROWS), PACK_ROWS), :]
        dst = xs_ref.at[pl.ds(pl.multiple_of(dest_ref[t * TOP_K + k] * PACK_ROWS, PACK_ROWS), PACK_ROWS), :]
        return pltpu.make_async_copy(src, dst, sem)

    def issue(t, carry):
        for k in range(TOP_K):
            row_copy(t, k).start(priority=k % 2)
        return carry

    def drain(t, carry):
        for k in range(TOP_K):
            row_copy(t, k).wait()
        return carry

    lax.fori_loop(0, tb, issue, 0)
    lax.fori_loop(0, tb, drain, 0)


def _dispatch(dest, h2p, tail_start, tail_len, n_slots, tr):
    n = dest.shape[0] // TOP_K
    tb = min(TB_DISPATCH, n)
    max_piece = 1 << ((tr - 1).bit_length() - 1)
    grid_spec = pltpu.PrefetchScalarGridSpec(
        num_scalar_prefetch=2,
        grid=(n // tb,),
        in_specs=[pl.BlockSpec((tb * TOP_K,), lambda i, ts, tl: (i,), memory_space=pltpu.SMEM),
                  pl.BlockSpec((tb * PACK_ROWS, LANES), lambda i, ts, tl: (i, 0))],
        out_specs=pl.BlockSpec(memory_space=pl.ANY),
        scratch_shapes=[pltpu.VMEM((max_piece * PACK_ROWS, LANES), I32),
                        pltpu.SemaphoreType.DMA(()), pltpu.SemaphoreType.DMA(())],
    )
    return pl.pallas_call(
        functools.partial(_dispatch_body, tb=tb, tr=tr),
        grid_spec=grid_spec,
        out_shape=jax.ShapeDtypeStruct((n_slots * PACK_ROWS, LANES), I32),
        compiler_params=_params("arbitrary"),
        name="dispatch",
    )(tail_start, tail_len, dest, h2p)


def _swiglu_packed(xp_ref, wgu_ref, wd_ref, rows, nsub=1):
    r = rows // nsub
    subs = range(nsub)
    x = [jnp.concatenate([_unpack_bf16_pairs(xp_ref[pl.ds(i * r * PACK_ROWS + s, r, stride=PACK_ROWS), :])
                          for s in range(PACK_ROWS)], axis=1) for i in subs]
    h = [jnp.dot(x[i], wgu_ref[...], preferred_element_type=F32) for i in subs]
    a = [(h[i][:, :D_EXPERT] * jax.nn.sigmoid(h[i][:, :D_EXPERT]) * h[i][:, D_EXPERT:]).astype(BF16) for i in subs]
    return [jnp.dot(a[i], wd_ref[...], preferred_element_type=F32) for i in subs]


def _experts_body(be_ref, nv_ref, new_ref, par_ref, nxt_ref, xs_ref, wg_ref, wu_ref, wd_ref, y_ref,
                  wg_st, wu_st, wd_st, wgu_s, wd_s, sems, *, tr):
    b = pl.program_id(0)

    def weight_copies(e, slot):
        return [pltpu.make_async_copy(src.at[e], dst.at[slot], sems.at[slot])
                for src, dst in ((wg_ref, wg_st), (wu_ref, wu_st), (wd_ref, wd_st))]

    @pl.when(b == 0)
    def _():
        for copy in weight_copies(be_ref[0], par_ref[0]):
            copy.start()

    @pl.when(new_ref[b] > 0)
    def _():
        slot = par_ref[b]
        for copy in weight_copies(be_ref[b], slot):
            copy.wait()
        wgu_s[:, :D_EXPERT] = wg_st[slot].astype(BF16)
        wgu_s[:, D_EXPERT:] = wu_st[slot].astype(BF16)
        wd_s[...] = wd_st[slot].astype(BF16)

        @pl.when(nxt_ref[b] >= 0)
        def _():
            for copy in weight_copies(nxt_ref[b], 1 - slot):
                copy.start()

    nsub = EXPERT_SUBTILES
    r = tr // nsub
    nv = nv_ref[b]

    def run(live):
        ys = _swiglu_packed(xs_ref, wgu_s, wd_s, live * r, live)
        for i, y in enumerate(ys):
            for s in range(PACK_ROWS):
                c = 2 * LANES * s
                y_ref[pl.ds(i * r * PACK_ROWS + s, r, stride=PACK_ROWS), :] = _pack_bf16_pairs(
                    y[:, c:c + LANES], y[:, c + LANES:c + 2 * LANES])
        if live < nsub:
            y_ref[live * r * PACK_ROWS:, :] = jnp.zeros(((nsub - live) * r * PACK_ROWS, LANES), I32)

    for live in range(1, nsub + 1):
        pl.when((nv > (live - 1) * r) & (nv <= live * r))(functools.partial(run, live))

    @pl.when(nv == 0)
    def _():
        y_ref[...] = jnp.zeros(y_ref.shape, I32)


def _experts(blk_e, blk_nv, blk_new, blk_par, blk_nxt, xs, w_gate, w_up, w_down):
    tr = TR_EXPERT
    nb = blk_e.shape[0]
    d, f = w_gate.shape[1], w_gate.shape[2]
    any_space = pl.BlockSpec(memory_space=pl.ANY)
    grid_spec = pltpu.PrefetchScalarGridSpec(
        num_scalar_prefetch=5,
        grid=(nb,),
        in_specs=[pl.BlockSpec((tr * PACK_ROWS, LANES), lambda b, be, nv, *_: (jnp.where(nv[b] > 0, b, 0), 0)),
                  any_space, any_space, any_space],
        out_specs=pl.BlockSpec((tr * PACK_ROWS, LANES), lambda b, *_: (b, 0)),
        scratch_shapes=[pltpu.VMEM((2, d, f), F32), pltpu.VMEM((2, d, f), F32), pltpu.VMEM((2, f, d), F32),
                        pltpu.VMEM((d, 2 * f), BF16), pltpu.VMEM((f, d), BF16), pltpu.SemaphoreType.DMA((2,))],
    )
    return pl.pallas_call(
        functools.partial(_experts_body, tr=tr),
        grid_spec=grid_spec,
        out_shape=jax.ShapeDtypeStruct((nb * tr * PACK_ROWS, LANES), I32),
        compiler_params=_params("arbitrary"),
        name="experts",
    )(blk_e, blk_nv, blk_new, blk_par, blk_nxt, xs, w_gate, w_up, w_down)


def _combine_body(dest_ref, dest_next_ref, wts_ref, h2p_ref, x1_ref, gf_ref, fg_ref, wsgu_ref, wsd_ref, y_ref,
                  out_ref, buf_a, buf_b, sem_a, sem_b, *, tb, nsteps):
    i = pl.program_id(0)
    buf_rows = TOP_K * tb * PACK_ROWS

    def row_copy(slots_ref, t, k, buf, sem):
        src = y_ref.at[pl.ds(pl.multiple_of(slots_ref[t * TOP_K + k] * PACK_ROWS, PACK_ROWS), PACK_ROWS), :]
        dst = buf.at[pl.ds(pl.multiple_of((k * tb + t) * PACK_ROWS, PACK_ROWS), PACK_ROWS), :]
        return pltpu.make_async_copy(src, dst, sem)

    def wait_block(buf, sem):
        pltpu.make_async_copy(y_ref.at[pl.ds(0, buf_rows), :], buf, sem).wait()

    @pl.when(i == 0)
    def _():
        def issue(t, carry):
            for k in range(TOP_K):
                row_copy(dest_ref, t, k, buf_a, sem_a).start(priority=k % 2)
            return carry

        lax.fori_loop(0, tb, issue, 0)

    def step(buf, sem, next_buf, next_sem):
        wait_block(buf, sem)
        for t in range(tb):
            for k in range(TOP_K):
                row_copy(dest_next_ref, t, k, next_buf, next_sem).start(priority=k % 2)
        shared = _swiglu_packed(h2p_ref, wsgu_ref, wsd_ref, tb)[0]
        wts = wts_ref[...]
        wb = [jnp.broadcast_to(wts[:, k:k + 1], (tb, LANES)) for k in range(TOP_K)]
        cols = []
        for s in range(PACK_ROWS):
            c = 2 * LANES * s
            lo, hi = shared[:, c:c + LANES], shared[:, c + LANES:c + 2 * LANES]
            for k in range(TOP_K):
                w = buf[pl.ds(k * tb * PACK_ROWS + s, tb, stride=PACK_ROWS), :]
                y_lo, y_hi = _unpack_f32_pairs(w)
                lo = lo + wb[k] * y_lo
                hi = hi + wb[k] * y_hi
            cols += [lo, hi]
        moe = jnp.concatenate(cols, axis=1)
        out_ref[...] = _rms(x1_ref[...] + gf_ref[...] * moe, fg_ref[...])

        @pl.when(i == nsteps - 1)
        def _():
            wait_block(next_buf, next_sem)

    pl.when(i % 2 == 0)(functools.partial(step, buf_a, sem_a, buf_b, sem_b))
    pl.when(i % 2 == 1)(functools.partial(step, buf_b, sem_b, buf_a, sem_a))


def _combine(dest, wts_t, h2p, x1, gate_f, final_g, w_sgu, w_sd, y):
    n, d = x1.shape
    tb = min(TB_COMBINE, n)
    row = lambda i: (i, 0)
    vec = pl.BlockSpec((1, d), lambda i: (0, 0))
    nsteps = n // tb
    buf = pltpu.VMEM((TOP_K * tb * PACK_ROWS, LANES), I32)
    return pl.pallas_call(
        functools.partial(_combine_body, tb=tb, nsteps=nsteps),
        grid=(nsteps,),
        in_specs=[pl.BlockSpec((tb * TOP_K,), lambda i: (i,), memory_space=pltpu.SMEM),
                  pl.BlockSpec((tb * TOP_K,), lambda i: (jnp.minimum(i + 1, nsteps - 1),), memory_space=pltpu.SMEM),
                  pl.BlockSpec((tb, TOP_K), row),
                  pl.BlockSpec((tb * PACK_ROWS, LANES), row),
                  pl.BlockSpec((tb, d), row), vec, vec,
                  _resident(w_sgu.shape), _resident(w_sd.shape),
                  pl.BlockSpec(memory_space=pl.ANY)],
        out_specs=pl.BlockSpec((tb, d), row),
        out_shape=jax.ShapeDtypeStruct((n, d), F32),
        scratch_shapes=[buf, buf, pltpu.SemaphoreType.DMA(()), pltpu.SemaphoreType.DMA(())],
        compiler_params=_params("arbitrary"),
        name="combine",
    )(dest, dest, wts_t, h2p, x1, gate_f, final_g, w_sgu, w_sd, y)


def _rope_tables(pos):
    half = QK_ROPE_DIM // 2
    assert LANES == 4 * half
    inv_freq = ROPE_THETA ** (-jnp.arange(half, dtype=F32) / half)
    ang = pos.astype(F32)[:, None] * inv_freq
    cos, sin = jnp.cos(ang), jnp.sin(ang)
    z = jnp.zeros_like(cos)
    c = jnp.concatenate([cos, cos, z, z], axis=1)
    s1 = jnp.concatenate([z, sin, z, z], axis=1)
    s2 = jnp.concatenate([-sin, z, z, z], axis=1)
    return c, s1, s2


def _layer(x, c, pos, norm_attn_g, w_ada, b_ada, w_in, g_q, w_uq, g_kv, w_ukv, g_out_swa, g_out_mla, w_o,
           norm_ffn_g, w_router, router_bias, w_exp_gate, w_exp_up, w_exp_down, w_sh_gate, w_sh_up, w_sh_down,
           final_g):
    s, d = x.shape
    row = lambda a: a.reshape(1, -1)

    mod = _ada(c, w_ada, b_ada)
    shift_a, scale_a, gate_a, shift_f, scale_f, gate_f = [mod[:, i * d:(i + 1) * d] for i in range(N_ADA)]

    n_qkv = 3 * D_SWA
    w_qkv = w_in[:, :n_qkv].astype(BF16)
    w_rest = jnp.pad(w_in[:, n_qkv:], ((0, 0), (0, LANES - QK_ROPE_DIM))).astype(BF16)
    dils = tuple(dil for _, dil in SWA_PATTERNS)
    assert all(window // dil == SWA_BLOCK and s % (dil * SWA_BLOCK) == 0 for window, dil in SWA_PATTERNS)
    rest, qkv_views = _inproj(x, row(norm_attn_g), scale_a, shift_a, w_qkv, w_rest, dils)

    dq = QK_NOPE_DIM + QK_ROPE_DIM
    w_uq_p = jnp.pad(w_uq.reshape(Q_LORA_RANK, N_HEADS_MLA, dq), ((0, 0), (0, 0), (0, MLA_QK_PAD - dq)))
    w_uq_p = w_uq_p.reshape(Q_LORA_RANK, N_HEADS_MLA * MLA_QK_PAD).astype(BF16)
    rc, rs1, rs2 = _rope_tables(pos)
    q_m, k_m, v_m = _mlaproj(rest, row(g_q), row(g_kv), w_uq_p, w_ukv.astype(BF16), rc, rs1, rs2)
    o_b = _mla(q_m, k_m, v_m)

    posf = pos.astype(F32)
    o_pats, lse_pats = zip(*[_dilated(qkv_v, posf, dil) for qkv_v, dil in zip(qkv_views, dils)])

    x1, h2p, logits_t = _outproj(o_pats, lse_pats, dils, o_b, x, row(g_out_swa), row(g_out_mla),
                                 w_o.astype(BF16), gate_a, row(norm_ffn_g), scale_f, shift_f, w_router.T)

    eidx, wts, rank, cnt = _route(logits_t, router_bias)
    tr = TR_EXPERT
    counts = cnt[:, 0].astype(I32)
    padded = (counts + tr - 1) // tr * tr
    e_ids = jnp.arange(N_EXPERTS, dtype=I32)
    pad_end = jnp.sum(jnp.where(e_ids[None, :] <= e_ids[:, None], padded[None, :], 0), axis=1)
    pad_start = pad_end - padded
    lookup = lambda table, idx: jnp.sum(jnp.where(idx[..., None] == e_ids, table, 0), axis=-1)
    dest = (lookup(pad_start, eidx) + rank).T.reshape(-1)
    n_slots = s * TOP_K + N_EXPERTS * tr
    blk_start = jnp.arange(n_slots // tr, dtype=I32) * tr
    blk_e = jnp.minimum(jnp.sum((pad_end[None, :] <= blk_start[:, None]).astype(I32), axis=1), N_EXPERTS - 1)
    blk_nv = jnp.clip(lookup(counts, blk_e) - (blk_start - lookup(pad_start, blk_e)), 0, tr)
    blk_new = ((blk_nv > 0) & (blk_start == lookup(pad_start, blk_e))).astype(I32)
    used = counts > 0
    later_used = used[None, :] & (e_ids[None, :] > e_ids[:, None])
    next_used = jnp.min(jnp.where(later_used, e_ids[None, :], N_EXPERTS), axis=1)
    next_used = jnp.where(next_used < N_EXPERTS, next_used, -1)
    used_before = jnp.sum((used[None, :] & (e_ids[None, :] < e_ids[:, None])).astype(I32), axis=1)
    blk_par = lookup(used_before % 2, blk_e)
    blk_nxt = lookup(next_used, blk_e)

    xs = _dispatch(dest, h2p, pad_start + counts, padded - counts, n_slots, tr)
    y = _experts(blk_e, blk_nv, blk_new, blk_par, blk_nxt, xs, w_exp_gate, w_exp_up, w_exp_down)
    w_sgu = jnp.concatenate([w_sh_gate, w_sh_up], axis=1).astype(BF16)
    return _combine(dest, wts.T, h2p, x1, gate_f, row(final_g), w_sgu, w_sh_down.astype(BF16), y)


def kernel(x, c, positions, norm_attn_g, w_ada, b_ada, w_in, g_q, w_uq, g_kv, w_ukv, g_out_swa, g_out_mla, w_o,
           norm_ffn_g, w_router, router_bias, w_exp_gate, w_exp_up, w_exp_down, w_sh_gate, w_sh_up, w_sh_down,
           final_norm_g):
    assert x.shape[0] == 1 and w_ada.shape[0] == 1
    out = _layer(x[0], c[0], positions[0], norm_attn_g[0], w_ada[0], b_ada[0], w_in[0], g_q[0], w_uq[0], g_kv[0],
                 w_ukv[0], g_out_swa[0], g_out_mla[0], w_o[0], norm_ffn_g[0], w_router[0], router_bias[0],
                 w_exp_gate[0], w_exp_up[0], w_exp_down[0], w_sh_gate[0], w_sh_up[0], w_sh_down[0], final_norm_g)
    return out[None]
```

```python
import functools

import jax
import jax.numpy as jnp
from jax import lax
from jax.experimental import pallas as pl
from jax.experimental.pallas import tpu as pltpu

F32 = jnp.float32
BF16 = jnp.bfloat16
I32 = jnp.int32

D_MODEL = 2048
N_HEADS_SWA = 8
HEAD_DIM_SWA = 128
SWA_PATTERNS = ((128, 1), (512, 4), (2048, 16))
SWA_BLOCK = 128
N_HEADS_MLA = 8
Q_LORA_RANK = 512
KV_LORA_RANK = 256
QK_NOPE_DIM = 128
QK_ROPE_DIM = 64
V_HEAD_DIM = 128
ROPE_THETA = 10000.0
D_SWA = N_HEADS_SWA * HEAD_DIM_SWA
D_MLA = N_HEADS_MLA * V_HEAD_DIM
N_EXPERTS = 64
N_GROUPS = 8
TOPK_GROUPS = 4
TOP_K = 8
D_EXPERT = 512
ROUTED_SCALE = 2.5
N_ADA = 6
EPS = 1e-6
NEG_INF = -1e30
LOG2E = 1.4426950408889634

LANES = 128
MLA_QK_PAD = 256
PACK_ROWS = D_MODEL // (2 * LANES)
V7X_VMEM_BYTES = 64 * 1024 * 1024
VMEM_LIMIT = V7X_VMEM_BYTES // 8 * 7

TN_ADA = 512
TM_INPROJ = 256
TM_MLAPROJ = 512
T_MLA = 1024
DIL_GROUP = 8
TM_OUTPROJ = 256
OUTPROJ_SUBTILES = 2
TN_ROUTE = 512
TB_DISPATCH = 256
TR_EXPERT = 512
EXPERT_SUBTILES = 2
TB_COMBINE = 256


def _params(*sem):
    return pltpu.CompilerParams(dimension_semantics=sem, vmem_limit_bytes=VMEM_LIMIT)


def _rms(x, g):
    return x * lax.rsqrt(jnp.mean(x * x, axis=-1, keepdims=True) + EPS) * g


def _resident(shape):
    nd = len(shape)
    return pl.BlockSpec(shape, lambda *_: (0,) * nd, pipeline_mode=pl.Buffered(1))


BF16_BITS = 16
HIGH_HALF = -(1 << BF16_BITS)


def _pack_bf16_pairs(a, b):
    ua = lax.bitcast_convert_type(a.astype(BF16).astype(F32), I32)
    ub = lax.bitcast_convert_type(b.astype(BF16).astype(F32), I32)
    return lax.shift_right_logical(ua, jnp.int32(BF16_BITS)) | (ub & jnp.int32(HIGH_HALF))


def _unpack_f32_pairs(w):
    lo = lax.bitcast_convert_type(lax.shift_left(w, jnp.int32(BF16_BITS)), F32)
    hi = lax.bitcast_convert_type(w & jnp.int32(HIGH_HALF), F32)
    return lo, hi


def _unpack_bf16_pairs(w):
    lo, hi = _unpack_f32_pairs(w)
    return jnp.concatenate([lo.astype(BF16), hi.astype(BF16)], axis=1)


def _ada_body(c_ref, w_ref, b_ref, o_ref):
    c = c_ref[...]
    a = c * jax.nn.sigmoid(c)
    o_ref[...] = jnp.sum(w_ref[...] * a, axis=0, keepdims=True) + b_ref[...]


def _ada(c, w_ada, b_ada):
    d, n = w_ada.shape
    tn = TN_ADA
    return pl.pallas_call(
        _ada_body,
        grid=(n // tn,),
        in_specs=[pl.BlockSpec((d, 1), lambda j: (0, 0)),
                  pl.BlockSpec((d, tn), lambda j: (0, j)),
                  pl.BlockSpec((1, tn), lambda j: (0, j))],
        out_specs=pl.BlockSpec((1, tn), lambda j: (0, j)),
        out_shape=jax.ShapeDtypeStruct((1, n), F32),
        compiler_params=_params("parallel"),
        name="ada",
    )(c.reshape(d, 1), w_ada, b_ada.reshape(1, n))


def _inproj_body(x_ref, g_ref, sc_ref, sh_ref, wqkv_ref, wr_ref, rest_ref, *out_and_scratch, tm, dils):
    view_refs, res_ref = out_and_scratch[:-1], out_and_scratch[-1]
    n = wqkv_ref.shape[1]
    h = (_rms(x_ref[...], g_ref[...]) * (1.0 + sc_ref[...]) + sh_ref[...]).astype(BF16)
    rest_ref[...] = jnp.dot(h, wr_ref[...], preferred_element_type=F32)
    res = jnp.dot(h, wqkv_ref[...], preferred_element_type=F32)
    chunks = range(n // LANES)
    for c in chunks:
        res_ref[c] = res[:, c * LANES:(c + 1) * LANES]
    for dil, v_ref in zip(dils, view_refs):
        if dil == 1:
            v_ref[...] = res.astype(BF16)
            continue
        for r in range(dil):
            for c in chunks:
                b = r * n + c * LANES
                v_ref[:, b:b + LANES] = res_ref[c, pl.ds(r, tm // dil, stride=dil), :].astype(BF16)


def _inproj(x, g, scale, shift, w_qkv, w_rest, dils):
    s, d = x.shape
    tm = min(TM_INPROJ, s)
    n1, n2 = w_qkv.shape[1], w_rest.shape[1]
    row = lambda i: (i, 0)
    vec = pl.BlockSpec((1, d), lambda i: (0, 0))
    outs = pl.pallas_call(
        functools.partial(_inproj_body, tm=tm, dils=dils),
        grid=(s // tm,),
        in_specs=[pl.BlockSpec((tm, d), row), vec, vec, vec, _resident((d, n1)), _resident((d, n2))],
        out_specs=[pl.BlockSpec((tm, n2), row)] + [pl.BlockSpec((tm // dil, dil * n1), row) for dil in dils],
        out_shape=[jax.ShapeDtypeStruct((s, n2), F32)]
        + [jax.ShapeDtypeStruct((s // dil, dil * n1), BF16) for dil in dils],
        scratch_shapes=[pltpu.VMEM((n1 // LANES, tm, LANES), F32)],
        compiler_params=_params("parallel"),
        name="inproj",
    )(x, g, scale, shift, w_qkv, w_rest)
    return outs[0], outs[1:]


def _rope_tail(t, c, s1, s2):
    half = QK_ROPE_DIM // 2
    return t * c + pltpu.roll(t, half, 1) * s1 + pltpu.roll(t, LANES - half, 1) * s2


def _mlaproj_body(rest_ref, gq_ref, gkv_ref, wuq_ref, wukv_ref, c_ref, s1_ref, s2_ref,
                  q_ref, k_ref, v_ref, *, scale):
    rest = rest_ref[...]
    c, s1, s2 = c_ref[...], s1_ref[...], s2_ref[...]
    cq = _rms(rest[:, :Q_LORA_RANK], gq_ref[...]).astype(BF16)
    ckv = _rms(rest[:, Q_LORA_RANK:Q_LORA_RANK + KV_LORA_RANK], gkv_ref[...]).astype(BF16)
    ktail = _rope_tail(rest[:, Q_LORA_RANK + KV_LORA_RANK:], c, s1, s2).astype(BF16)
    q = jnp.dot(cq, wuq_ref[...], preferred_element_type=F32)
    kv = jnp.dot(ckv, wukv_ref[...], preferred_element_type=F32)
    for h in range(N_HEADS_MLA):
        b = h * MLA_QK_PAD
        q_ref[:, b:b + LANES] = (q[:, b:b + LANES] * scale).astype(BF16)
        q_ref[:, b + LANES:b + 2 * LANES] = (_rope_tail(q[:, b + LANES:b + 2 * LANES], c, s1, s2) * scale).astype(BF16)
        k_ref[:, b:b + LANES] = kv[:, b:b + LANES].astype(BF16)
        k_ref[:, b + LANES:b + 2 * LANES] = ktail
        v_ref[:, b:b + LANES] = kv[:, b + LANES:b + 2 * LANES].astype(BF16)
        v_ref[:, b + LANES:b + 2 * LANES] = jnp.ones((q.shape[0], LANES), BF16)


def _mlaproj(rest, g_q, g_kv, w_uq, w_ukv, rc, rs1, rs2):
    s, nr = rest.shape
    tm = min(TM_MLAPROJ, s)
    nq = N_HEADS_MLA * MLA_QK_PAD
    row = lambda i: (i, 0)
    tab = pl.BlockSpec((tm, LANES), row)
    scale = float(QK_NOPE_DIM + QK_ROPE_DIM) ** -0.5 * LOG2E
    return pl.pallas_call(
        functools.partial(_mlaproj_body, scale=scale),
        grid=(s // tm,),
        in_specs=[pl.BlockSpec((tm, nr), row),
                  pl.BlockSpec((1, Q_LORA_RANK), lambda i: (0, 0)),
                  pl.BlockSpec((1, KV_LORA_RANK), lambda i: (0, 0)),
                  _resident(w_uq.shape), _resident(w_ukv.shape), tab, tab, tab],
        out_specs=[pl.BlockSpec((tm, nq), row)] * 3,
        out_shape=[jax.ShapeDtypeStruct((s, nq), BF16)] * 3,
        compiler_params=_params("parallel"),
        name="mlaproj",
    )(rest, g_q, g_kv, w_uq, w_ukv, rc, rs1, rs2)


def _mla_body(q_ref, k_ref, v_ref, o_ref, m_ref, acc_ref, sa_ref, sb_ref, *, t):
    qi = pl.program_id(1)
    q = q_ref[...]
    m_ref[...] = jnp.full(m_ref.shape, NEG_INF, F32)
    acc_ref[...] = jnp.zeros(acc_ref.shape, F32)

    def scores(j, dst):
        k = k_ref[pl.ds(pl.multiple_of(j * t, t), t), :]
        dst[...] = lax.dot_general(q, k, (((1,), (1,)), ((), ())), preferred_element_type=F32)

    def absorb(j, src, masked):
        s = src[...]
        if masked:
            r = lax.broadcasted_iota(I32, (t, t), 0)
            cidx = lax.broadcasted_iota(I32, (t, t), 1)
            s = jnp.where(cidx <= r, s, NEG_INF)
        v = v_ref[pl.ds(pl.multiple_of(j * t, t), t), :]
        m_old = m_ref[...]
        m_new = jnp.maximum(m_old, jnp.max(s, axis=-1, keepdims=True))
        p = jnp.exp2(s - m_new).astype(BF16)
        acc_ref[...] = jnp.exp2(m_old - m_new) * acc_ref[...] + jnp.dot(p, v, preferred_element_type=F32)
        m_ref[...] = m_new

    scores(0, sa_ref)

    def pair(i, carry):
        j = 2 * i
        scores(j + 1, sb_ref)
        absorb(j, sa_ref, False)
        scores(j + 2, sa_ref)
        absorb(j + 1, sb_ref, False)
        return carry

    lax.fori_loop(0, qi // 2, pair, 0)

    @pl.when(qi % 2 == 0)
    def _():
        absorb(qi, sa_ref, True)

    @pl.when(qi % 2 == 1)
    def _():
        scores(qi, sb_ref)
        absorb(qi - 1, sa_ref, False)
        absorb(qi, sb_ref, True)

    acc = acc_ref[...]
    o_ref[...] = acc[:, :V_HEAD_DIM] / acc[:, V_HEAD_DIM:]


def _mla(q, k, v):
    s = q.shape[0]
    t = min(T_MLA, s)
    head_cols = lambda h, i: (0, h)
    return pl.pallas_call(
        functools.partial(_mla_body, t=t),
        grid=(N_HEADS_MLA, s // t),
        in_specs=[pl.BlockSpec((t, MLA_QK_PAD), lambda h, i: (i, h)),
                  pl.BlockSpec((s, MLA_QK_PAD), head_cols),
                  pl.BlockSpec((s, MLA_QK_PAD), head_cols)],
        out_specs=pl.BlockSpec((t, V_HEAD_DIM), lambda h, i: (i, h)),
        out_shape=jax.ShapeDtypeStruct((s, D_MLA), F32),
        scratch_shapes=[pltpu.VMEM((t, 1), F32), pltpu.VMEM((t, MLA_QK_PAD), F32),
                        pltpu.VMEM((t, t), F32), pltpu.VMEM((t, t), F32)],
        compiler_params=_params("parallel", "arbitrary"),
        name="mla",
    )(q, k, v)


def _dilated_body(q_ref, kc_ref, kp_ref, vc_ref, vp_ref, pq_ref, pkc_ref, pkp_ref, o_ref, lse_ref):
    n = pl.program_id(1)
    blk = SWA_BLOCK
    i = lax.broadcasted_iota(I32, (blk, blk), 0)
    j = lax.broadcasted_iota(I32, (blk, blk), 1)
    ok_cur = j <= i
    scale = float(HEAD_DIM_SWA) ** -0.5
    nt = (((1,), (1,)), ((), ()))
    ones = jnp.ones((blk, HEAD_DIM_SWA), BF16)
    rows = [slice(g * blk, (g + 1) * blk) for g in range(DIL_GROUP)]
    dist_cur, dist_prev, ok_prev, prev_of = [], [], [], []
    for g in range(DIL_GROUP):
        pq = pq_ref[rows[g], :]
        dist_cur.append(jnp.abs(pq - pkc_ref[0][:, rows[g]]))
        if g == 0:
            dist_prev.append(jnp.abs(pq - pkp_ref[0]))
            ok_prev.append((j >= i) & (n > 0))
            prev_of.append((kp_ref, vp_ref, slice(0, blk)))
        else:
            dist_prev.append(jnp.abs(pq - pkc_ref[0][:, rows[g - 1]]))
            ok_prev.append(j >= i)
            prev_of.append((kc_ref, vc_ref, rows[g - 1]))
    units = [(g, h) for g in range(DIL_GROUP) for h in range(N_HEADS_SWA)]
    hs = lambda h: slice(h * HEAD_DIM_SWA, (h + 1) * HEAD_DIM_SWA)
    slope = lambda h: 2.0 ** (-8.0 * (h + 1) / N_HEADS_SWA)
    sc = [jnp.where(ok_cur, lax.dot_general(q_ref[rows[g], hs(h)], kc_ref[rows[g], hs(h)], nt,
                                            preferred_element_type=F32) * scale - slope(h) * dist_cur[g], NEG_INF)
          for g, h in units]
    sp = [jnp.where(ok_prev[g], lax.dot_general(q_ref[rows[g], hs(h)], prev_of[g][0][prev_of[g][2], hs(h)], nt,
                                                preferred_element_type=F32) * scale - slope(h) * dist_prev[g], NEG_INF)
          for g, h in units]
    m = [jnp.max(jnp.maximum(a, b), axis=-1, keepdims=True) for a, b in zip(sc, sp)]
    pc = [jnp.exp(a - mm).astype(BF16) for a, mm in zip(sc, m)]
    pp = [jnp.exp(b - mm).astype(BF16) for b, mm in zip(sp, m)]
    acc = [jnp.dot(pc[u], jnp.concatenate([vc_ref[rows[g], hs(h)], ones], axis=1), preferred_element_type=F32)
           + jnp.dot(pp[u], jnp.concatenate([prev_of[g][1][prev_of[g][2], hs(h)], ones], axis=1),
                     preferred_element_type=F32)
           for u, (g, h) in enumerate(units)]
    for u, (g, h) in enumerate(units):
        den = acc[u][:, HEAD_DIM_SWA:]
        o_ref[rows[g], hs(h)] = acc[u][:, :HEAD_DIM_SWA] / den
        lse_ref[rows[g], hs(h)] = m[u] + jnp.log(den)


def _dilated(qkv_v, posf, dil):
    sd = qkv_v.shape[0]
    blk = SWA_BLOCK
    grp = DIL_GROUP * blk
    assert sd % grp == 0
    pos_v = posf.reshape(sd, dil)
    pq_v = jnp.repeat(pos_v, LANES, axis=1)
    pos_rows = pos_v.T.reshape(dil, 1, sd)
    prev = lambda n: jnp.maximum(DIL_GROUP * n - 1, 0)
    wide, one = (grp, D_SWA), (blk, D_SWA)
    return pl.pallas_call(
        _dilated_body,
        grid=(dil, sd // grp),
        in_specs=[pl.BlockSpec(wide, lambda r, n: (n, 3 * r)),
                  pl.BlockSpec(wide, lambda r, n: (n, 3 * r + 1)),
                  pl.BlockSpec(one, lambda r, n: (prev(n), 3 * r + 1)),
                  pl.BlockSpec(wide, lambda r, n: (n, 3 * r + 2)),
                  pl.BlockSpec(one, lambda r, n: (prev(n), 3 * r + 2)),
                  pl.BlockSpec((grp, LANES), lambda r, n: (n, r)),
                  pl.BlockSpec((1, 1, grp), lambda r, n: (r, 0, n)),
                  pl.BlockSpec((1, 1, blk), lambda r, n: (r, 0, prev(n)))],
        out_specs=[pl.BlockSpec(wide, lambda r, n: (n, r)), pl.BlockSpec(wide, lambda r, n: (n, r))],
        out_shape=[jax.ShapeDtypeStruct((sd, dil * D_SWA), F32), jax.ShapeDtypeStruct((sd, dil * D_SWA), F32)],
        compiler_params=_params("parallel", "parallel"),
        name=f"dil{dil}",
    )(qkv_v, qkv_v, qkv_v, qkv_v, qkv_v, pq_v, pos_rows, pos_rows)


def _outproj_body(*refs, tm, dils):
    npat = len(dils)
    o_views, l_views = refs[:npat], refs[npat:2 * npat]
    (ob_ref, x_ref, gsw_ref, gml_ref, wo_ref, ga_ref, nfg_ref, scf_ref, shf_ref, wrt_ref,
     x1_ref, h2p_ref, lgt_ref) = refs[2 * npat:2 * npat + 13]
    scratch = list(refs[2 * npat + 13:])

    chunks = range(D_SWA // LANES)

    def token_order(view_ref, dil):
        if dil == 1:
            return lambda rs: view_ref[rs, :]
        nat_ref = scratch.pop(0)
        for r in range(dil):
            for c in chunks:
                b = r * D_SWA + c * LANES
                nat_ref[c, pl.ds(r, tm // dil, stride=dil), :] = view_ref[:, b:b + LANES]
        return lambda rs: jnp.concatenate([nat_ref[c, rs, :] for c in chunks], axis=1)

    o1, o2, o3 = [token_order(v, dil) for v, dil in zip(o_views, dils)]
    l1f, l2f, l3f = [token_order(v, dil) for v, dil in zip(l_views, dils)]

    nsub = OUTPROJ_SUBTILES
    r = tm // nsub
    subs = range(nsub)
    rows = [slice(i * r, (i + 1) * r) for i in subs]

    def merged(rs):
        l1, l2, l3 = l1f(rs), l2f(rs), l3f(rs)
        m = jnp.maximum(jnp.maximum(l1, l2), l3)
        e1, e2, e3 = jnp.exp(l1 - m), jnp.exp(l2 - m), jnp.exp(l3 - m)
        return (e1 * o1(rs) + e2 * o2(rs) + e3 * o3(rs)) / (e1 + e2 + e3)

    mix = [jnp.concatenate([_rms(merged(rs), gsw_ref[...]), _rms(ob_ref[rs, :], gml_ref[...])],
                           axis=-1).astype(BF16) for rs in rows]
    proj = [jnp.dot(mix[i], wo_ref[...], preferred_element_type=F32) for i in subs]
    x1 = [x_ref[rows[i], :] + ga_ref[...] * proj[i] for i in subs]
    h2 = [_rms(x1[i], nfg_ref[...]) * (1.0 + scf_ref[...]) + shf_ref[...] for i in subs]
    for i in subs:
        x1_ref[rows[i], :] = x1[i]
        lgt_ref[:, rows[i]] = lax.dot_general(wrt_ref[...], h2[i], (((1,), (1,)), ((), ())),
                                              precision=lax.Precision.HIGHEST, preferred_element_type=F32)
    for i in subs:
        for s in range(PACK_ROWS):
            b = 2 * LANES * s
            h2p_ref[pl.ds(i * r * PACK_ROWS + s, r, stride=PACK_ROWS), :] = _pack_bf16_pairs(
                h2[i][:, b:b + LANES], h2[i][:, b + LANES:b + 2 * LANES])


def _outproj(o_pats, lse_pats, dils, o_b, x, g_sw, g_ml, w_o, gate_a, nfg, scale_f, shift_f, w_router_t):
    s, d = x.shape
    tm = min(TM_OUTPROJ, s)
    row = lambda i: (i, 0)
    views = [pl.BlockSpec((tm // dil, dil * D_SWA), row) for dil in dils]
    vec = lambda n: pl.BlockSpec((1, n), lambda i: (0, 0))
    n_reordered = 2 * sum(1 for dil in dils if dil > 1)
    return pl.pallas_call(
        functools.partial(_outproj_body, tm=tm, dils=dils),
        grid=(s // tm,),
        in_specs=views + views + [pl.BlockSpec((tm, D_MLA), row), pl.BlockSpec((tm, d), row), vec(D_SWA),
                                  vec(D_MLA), _resident(w_o.shape), vec(d), vec(d), vec(d), vec(d),
                                  _resident(w_router_t.shape)],
        out_specs=[pl.BlockSpec((tm, d), row), pl.BlockSpec((tm * PACK_ROWS, LANES), row),
                   pl.BlockSpec((N_EXPERTS, tm), lambda i: (0, i))],
        out_shape=[jax.ShapeDtypeStruct((s, d), F32), jax.ShapeDtypeStruct((s * PACK_ROWS, LANES), I32),
                   jax.ShapeDtypeStruct((N_EXPERTS, s), F32)],
        scratch_shapes=[pltpu.VMEM((D_SWA // LANES, tm, LANES), F32)] * n_reordered,
        compiler_params=_params("parallel"),
        name="outproj",
    )(*o_pats, *lse_pats, o_b, x, g_sw, g_ml, w_o, gate_a, nfg, scale_f, shift_f, w_router_t)


def _first_index(hit_value, x, iota, size, axis):
    return jnp.min(jnp.where(x == hit_value, iota, size), axis=axis, keepdims=True)


def _route_body(lgt_ref, bias_ref, tri_ref, eidx_ref, wts_ref, rank_ref, cnt_ref, carry_ref, *, tn):
    @pl.when(pl.program_id(0) == 0)
    def _():
        carry_ref[...] = jnp.zeros(carry_ref.shape, F32)

    gsz = N_EXPERTS // N_GROUPS
    scores = jax.nn.sigmoid(lgt_ref[...])
    choice = scores + bias_ref[...]
    neg = jnp.float32(-jnp.inf)

    g3 = choice.reshape(N_GROUPS, gsz, tn)
    i3 = lax.broadcasted_iota(I32, g3.shape, 1)
    m1 = jnp.max(g3, axis=1, keepdims=True)
    f1 = _first_index(m1, g3, i3, gsz, 1)
    m2 = jnp.max(jnp.where(i3 == f1, neg, g3), axis=1, keepdims=True)
    gs = (m1 + m2).reshape(N_GROUPS, tn)

    ig = lax.broadcasted_iota(I32, gs.shape, 0)
    gsel = jnp.zeros(gs.shape, F32)
    for _ in range(TOPK_GROUPS):
        hit = ig == _first_index(jnp.max(gs, axis=0, keepdims=True), gs, ig, N_GROUPS, 0)
        gsel = jnp.where(hit, 1.0, gsel)
        gs = jnp.where(hit, neg, gs)
    emask = jnp.broadcast_to(gsel.reshape(N_GROUPS, 1, tn), (N_GROUPS, gsz, tn)).reshape(N_EXPERTS, tn)
    cand = jnp.where(emask > 0.0, choice, NEG_INF)

    ie = lax.broadcasted_iota(I32, cand.shape, 0)
    picks, wsel = [], []
    onehot = jnp.zeros(cand.shape, F32)
    for _ in range(TOP_K):
        f = _first_index(jnp.max(cand, axis=0, keepdims=True), cand, ie, N_EXPERTS, 0)
        hit = ie == f
        picks.append(f)
        wsel.append(jnp.sum(jnp.where(hit, scores, 0.0), axis=0, keepdims=True))
        onehot = jnp.where(hit, 1.0, onehot)
        cand = jnp.where(hit, neg, cand)

    rank = carry_ref[...] + jnp.dot(onehot.astype(BF16), tri_ref[...], preferred_element_type=F32)
    carry_ref[...] = carry_ref[...] + jnp.sum(onehot, axis=1, keepdims=True)
    cnt_ref[...] = carry_ref[...]

    w = jnp.concatenate(wsel, axis=0)
    wts_ref[...] = w / jnp.sum(w, axis=0, keepdims=True) * ROUTED_SCALE
    eidx_ref[...] = jnp.concatenate(picks, axis=0)
    rank_ref[...] = jnp.concatenate(
        [jnp.sum(jnp.where(ie == f, rank, 0.0), axis=0, keepdims=True) for f in picks], axis=0).astype(I32)


def _route(logits_t, router_bias):
    e, n = logits_t.shape
    tn = min(TN_ROUTE, n)
    tri = (lax.broadcasted_iota(I32, (tn, tn), 0) < lax.broadcasted_iota(I32, (tn, tn), 1)).astype(BF16)
    col = lambda i: (0, i)
    return pl.pallas_call(
        functools.partial(_route_body, tn=tn),
        grid=(n // tn,),
        in_specs=[pl.BlockSpec((e, tn), col), pl.BlockSpec((e, 1), lambda i: (0, 0)),
                  pl.BlockSpec((tn, tn), lambda i: (0, 0))],
        out_specs=[pl.BlockSpec((TOP_K, tn), col), pl.BlockSpec((TOP_K, tn), col), pl.BlockSpec((TOP_K, tn), col),
                   pl.BlockSpec((e, 1), lambda i: (0, 0))],
        out_shape=[jax.ShapeDtypeStruct((TOP_K, n), I32), jax.ShapeDtypeStruct((TOP_K, n), F32),
                   jax.ShapeDtypeStruct((TOP_K, n), I32), jax.ShapeDtypeStruct((e, 1), F32)],
        scratch_shapes=[pltpu.VMEM((e, 1), F32)],
        compiler_params=_params("arbitrary"),
        name="route",
    )(logits_t, router_bias.reshape(e, 1), tri)


def _dispatch_body(tail_start_ref, tail_len_ref, dest_ref, h_ref, xs_ref, zeros_ref, sem, zsem, *, tb, tr):
    sizes = [1 << b for b in reversed(range((tr - 1).bit_length()))]

    def tail_copies(e, act):
        start, length = tail_start_ref[e], tail_len_ref[e]
        done = jnp.int32(0)
        for size in sizes:
            piece = (length & size) != 0
            dst = xs_ref.at[pl.ds(pl.multiple_of((start + done) * PACK_ROWS, PACK_ROWS), size * PACK_ROWS), :]
            copy = pltpu.make_async_copy(zeros_ref.at[pl.ds(0, size * PACK_ROWS), :], dst, zsem)
            pl.when(piece)(functools.partial(act, copy))
            done = done + (length & size)

    @pl.when(pl.program_id(0) == 0)
    def _():
        zeros_ref[...] = jnp.zeros(zeros_ref.shape, I32)

        def start_tail(e, carry):
            tail_copies(e, lambda copy: copy.start())
            return carry

        def wait_tail(e, carry):
            tail_copies(e, lambda copy: copy.wait())
            return carry

        lax.fori_loop(0, N_EXPERTS, start_tail, 0)
        lax.fori_loop(0, N_EXPERTS, wait_tail, 0)

    def row_copy(t, k):
        src = h_ref.at[pl.ds(pl.multiple_of(t * PACK_ROWS, PACK_ROWS), PACK_ROWS), :]
        dst = xs_ref.at[pl.ds(pl.multiple_of(dest_ref[t * TOP_K + k] * PACK_ROWS, PACK_ROWS), PACK_ROWS), :]
        return pltpu.make_async_copy(src, dst, sem)

    def issue(t, carry):
        for k in range(TOP_K):
            row_copy(t, k).start(priority=k % 2)
        return carry

    def drain(t, carry):
        for k in range(TOP_K):
            row_copy(t, k).wait()
        return carry

    lax.fori_loop(0, tb, issue, 0)
    lax.fori_loop(0, tb, drain, 0)


def _dispatch(dest, h2p, tail_start, tail_len, n_slots, tr):
    n = dest.shape[0] // TOP_K
    tb = min(TB_DISPATCH, n)
    max_piece = 1 << ((tr - 1).bit_length() - 1)
    grid_spec = pltpu.PrefetchScalarGridSpec(
        num_scalar_prefetch=2,
        grid=(n // tb,),
        in_specs=[pl.BlockSpec((tb * TOP_K,), lambda i, ts, tl: (i,), memory_space=pltpu.SMEM),
                  pl.BlockSpec((tb * PACK_ROWS, LANES), lambda i, ts, tl: (i, 0))],
        out_specs=pl.BlockSpec(memory_space=pl.ANY),
        scratch_shapes=[pltpu.VMEM((max_piece * PACK_ROWS, LANES), I32),
                        pltpu.SemaphoreType.DMA(()), pltpu.SemaphoreType.DMA(())],
    )
    return pl.pallas_call(
        functools.partial(_dispatch_body, tb=tb, tr=tr),
        grid_spec=grid_spec,
        out_shape=jax.ShapeDtypeStruct((n_slots * PACK_ROWS, LANES), I32),
        compiler_params=_params("arbitrary"),
        name="dispatch",
    )(tail_start, tail_len, dest, h2p)


def _swiglu_packed(xp_ref, wgu_ref, wd_ref, rows, nsub=1):
    r = rows // nsub
    subs = range(nsub)
    x = [jnp.concatenate([_unpack_bf16_pairs(xp_ref[pl.ds(i * r * PACK_ROWS + s, r, stride=PACK_ROWS), :])
                          for s in range(PACK_ROWS)], axis=1) for i in subs]
    h = [jnp.dot(x[i], wgu_ref[...], preferred_element_type=F32) for i in subs]
    a = [(h[i][:, :D_EXPERT] * jax.nn.sigmoid(h[i][:, :D_EXPERT]) * h[i][:, D_EXPERT:]).astype(BF16) for i in subs]
    return [jnp.dot(a[i], wd_ref[...], preferred_element_type=F32) for i in subs]


def _experts_body(be_ref, nv_ref, new_ref, par_ref, nxt_ref, xs_ref, wg_ref, wu_ref, wd_ref, y_ref,
                  wg_st, wu_st, wd_st, wgu_s, wd_s, sems, *, tr):
    b = pl.program_id(0)

    def weight_copies(e, slot):
        return [pltpu.make_async_copy(src.at[e], dst.at[slot], sems.at[slot])
                for src, dst in ((wg_ref, wg_st), (wu_ref, wu_st), (wd_ref, wd_st))]

    @pl.when(b == 0)
    def _():
        for copy in weight_copies(be_ref[0], par_ref[0]):
            copy.start()

    @pl.when(new_ref[b] > 0)
    def _():
        slot = par_ref[b]
        for copy in weight_copies(be_ref[b], slot):
            copy.wait()
        wgu_s[:, :D_EXPERT] = wg_st[slot].astype(BF16)
        wgu_s[:, D_EXPERT:] = wu_st[slot].astype(BF16)
        wd_s[...] = wd_st[slot].astype(BF16)

        @pl.when(nxt_ref[b] >= 0)
        def _():
            for copy in weight_copies(nxt_ref[b], 1 - slot):
                copy.start()

    nsub = EXPERT_SUBTILES
    r = tr // nsub
    nv = nv_ref[b]

    def run(live):
        ys = _swiglu_packed(xs_ref, wgu_s, wd_s, live * r, live)
        for i, y in enumerate(ys):
            for s in range(PACK_ROWS):
                c = 2 * LANES * s
                y_ref[pl.ds(i * r * PACK_ROWS + s, r, stride=PACK_ROWS), :] = _pack_bf16_pairs(
                    y[:, c:c + LANES], y[:, c + LANES:c + 2 * LANES])
        if live < nsub:
            y_ref[live * r * PACK_ROWS:, :] = jnp.zeros(((nsub - live) * r * PACK_ROWS, LANES), I32)

    for live in range(1, nsub + 1):
        pl.when((nv > (live - 1) * r) & (nv <= live * r))(functools.partial(run, live))

    @pl.when(nv == 0)
    def _():
        y_ref[...] = jnp.zeros(y_ref.shape, I32)


def _experts(blk_e, blk_nv, blk_new, blk_par, blk_nxt, xs, w_gate, w_up, w_down):
    tr = TR_EXPERT
    nb = blk_e.shape[0]
    d, f = w_gate.shape[1], w_gate.shape[2]
    any_space = pl.BlockSpec(memory_space=pl.ANY)
    grid_spec = pltpu.PrefetchScalarGridSpec(
        num_scalar_prefetch=5,
        grid=(nb,),
        in_specs=[pl.BlockSpec((tr * PACK_ROWS, LANES), lambda b, be, nv, *_: (jnp.where(nv[b] > 0, b, 0), 0)),
                  any_space, any_space, any_space],
        out_specs=pl.BlockSpec((tr * PACK_ROWS, LANES), lambda b, *_: (b, 0)),
        scratch_shapes=[pltpu.VMEM((2, d, f), F32), pltpu.VMEM((2, d, f), F32), pltpu.VMEM((2, f, d), F32),
                        pltpu.VMEM((d, 2 * f), BF16), pltpu.VMEM((f, d), BF16), pltpu.SemaphoreType.DMA((2,))],
    )
    return pl.pallas_call(
        functools.partial(_experts_body, tr=tr),
        grid_spec=grid_spec,
        out_shape=jax.ShapeDtypeStruct((nb * tr * PACK_ROWS, LANES), I32),
        compiler_params=_params("arbitrary"),
        name="experts",
    )(blk_e, blk_nv, blk_new, blk_par, blk_nxt, xs, w_gate, w_up, w_down)


def _combine_body(dest_ref, dest_next_ref, wts_ref, h2p_ref, x1_ref, gf_ref, fg_ref, wsgu_ref, wsd_ref, y_ref,
                  out_ref, buf_a, buf_b, sem_a, sem_b, *, tb, nsteps):
    i = pl.program_id(0)
    buf_rows = TOP_K * tb * PACK_ROWS

    def row_copy(slots_ref, t, k, buf, sem):
        src = y_ref.at[pl.ds(pl.multiple_of(slots_ref[t * TOP_K + k] * PACK_ROWS, PACK_ROWS), PACK_ROWS), :]
        dst = buf.at[pl.ds(pl.multiple_of((k * tb + t) * PACK_ROWS, PACK_ROWS), PACK_ROWS), :]
        return pltpu.make_async_copy(src, dst, sem)

    def wait_block(buf, sem):
        pltpu.make_async_copy(y_ref.at[pl.ds(0, buf_rows), :], buf, sem).wait()

    @pl.when(i == 0)
    def _():
        def issue(t, carry):
            for k in range(TOP_K):
                row_copy(dest_ref, t, k, buf_a, sem_a).start(priority=k % 2)
            return carry

        lax.fori_loop(0, tb, issue, 0)

    def step(buf, sem, next_buf, next_sem):
        wait_block(buf, sem)
        for t in range(tb):
            for k in range(TOP_K):
                row_copy(dest_next_ref, t, k, next_buf, next_sem).start(priority=k % 2)
        shared = _swiglu_packed(h2p_ref, wsgu_ref, wsd_ref, tb)[0]
        wts = wts_ref[...]
        wb = [jnp.broadcast_to(wts[:, k:k + 1], (tb, LANES)) for k in range(TOP_K)]
        cols = []
        for s in range(PACK_ROWS):
            c = 2 * LANES * s
            lo, hi = shared[:, c:c + LANES], shared[:, c + LANES:c + 2 * LANES]
            for k in range(TOP_K):
                w = buf[pl.ds(k * tb * PACK_ROWS + s, tb, stride=PACK_ROWS), :]
                y_lo, y_hi = _unpack_f32_pairs(w)
                lo = lo + wb[k] * y_lo
                hi = hi + wb[k] * y_hi
            cols += [lo, hi]
        moe = jnp.concatenate(cols, axis=1)
        out_ref[...] = _rms(x1_ref[...] + gf_ref[...] * moe, fg_ref[...])

        @pl.when(i == nsteps - 1)
        def _():
            wait_block(next_buf, next_sem)

    pl.when(i % 2 == 0)(functools.partial(step, buf_a, sem_a, buf_b, sem_b))
    pl.when(i % 2 == 1)(functools.partial(step, buf_b, sem_b, buf_a, sem_a))


def _combine(dest, wts_t, h2p, x1, gate_f, final_g, w_sgu, w_sd, y):
    n, d = x1.shape
    tb = min(TB_COMBINE, n)
    row = lambda i: (i, 0)
    vec = pl.BlockSpec((1, d), lambda i: (0, 0))
    nsteps = n // tb
    buf = pltpu.VMEM((TOP_K * tb * PACK_ROWS, LANES), I32)
    return pl.pallas_call(
        functools.partial(_combine_body, tb=tb, nsteps=nsteps),
        grid=(nsteps,),
        in_specs=[pl.BlockSpec((tb * TOP_K,), lambda i: (i,), memory_space=pltpu.SMEM),
                  pl.BlockSpec((tb * TOP_K,), lambda i: (jnp.minimum(i + 1, nsteps - 1),), memory_space=pltpu.SMEM),
                  pl.BlockSpec((tb, TOP_K), row),
                  pl.BlockSpec((tb * PACK_ROWS, LANES), row),
                  pl.BlockSpec((tb, d), row), vec, vec,
                  _resident(w_sgu.shape), _resident(w_sd.shape),
                  pl.BlockSpec(memory_space=pl.ANY)],
        out_specs=pl.BlockSpec((tb, d), row),
        out_shape=jax.ShapeDtypeStruct((n, d), F32),
        scratch_shapes=[buf, buf, pltpu.SemaphoreType.DMA(()), pltpu.SemaphoreType.DMA(())],
        compiler_params=_params("arbitrary"),
        name="combine",
    )(dest, dest, wts_t, h2p, x1, gate_f, final_g, w_sgu, w_sd, y)


def _rope_tables(pos):
    half = QK_ROPE_DIM // 2
    assert LANES == 4 * half
    inv_freq = ROPE_THETA ** (-jnp.arange(half, dtype=F32) / half)
    ang = pos.astype(F32)[:, None] * inv_freq
    cos, sin = jnp.cos(ang), jnp.sin(ang)
    z = jnp.zeros_like(cos)
    c = jnp.concatenate([cos, cos, z, z], axis=1)
    s1 = jnp.concatenate([z, sin, z, z], axis=1)
    s2 = jnp.concatenate([-sin, z, z, z], axis=1)
    return c, s1, s2


def _layer(x, c, pos, norm_attn_g, w_ada, b_ada, w_in, g_q, w_uq, g_kv, w_ukv, g_out_swa, g_out_mla, w_o,
           norm_ffn_g, w_router, router_bias, w_exp_gate, w_exp_up, w_exp_down, w_sh_gate, w_sh_up, w_sh_down,
           final_g):
    s, d = x.shape
    row = lambda a: a.reshape(1, -1)

    mod = _ada(c, w_ada, b_ada)
    shift_a, scale_a, gate_a, shift_f, scale_f, gate_f = [mod[:, i * d:(i + 1) * d] for i in range(N_ADA)]

    n_qkv = 3 * D_SWA
    w_qkv = w_in[:, :n_qkv].astype(BF16)
    w_rest = jnp.pad(w_in[:, n_qkv:], ((0, 0), (0, LANES - QK_ROPE_DIM))).astype(BF16)
    dils = tuple(dil for _, dil in SWA_PATTERNS)
    assert all(window // dil == SWA_BLOCK and s % (dil * SWA_BLOCK) == 0 for window, dil in SWA_PATTERNS)
    rest, qkv_views = _inproj(x, row(norm_attn_g), scale_a, shift_a, w_qkv, w_rest, dils)

    dq = QK_NOPE_DIM + QK_ROPE_DIM
    w_uq_p = jnp.pad(w_uq.reshape(Q_LORA_RANK, N_HEADS_MLA, dq), ((0, 0), (0, 0), (0, MLA_QK_PAD - dq)))
    w_uq_p = w_uq_p.reshape(Q_LORA_RANK, N_HEADS_MLA * MLA_QK_PAD).astype(BF16)
    rc, rs1, rs2 = _rope_tables(pos)
    q_m, k_m, v_m = _mlaproj(rest, row(g_q), row(g_kv), w_uq_p, w_ukv.astype(BF16), rc, rs1, rs2)
    o_b = _mla(q_m, k_m, v_m)

    posf = pos.astype(F32)
    o_pats, lse_pats = zip(*[_dilated(qkv_v, posf, dil) for qkv_v, dil in zip(qkv_views, dils)])

    x1, h2p, logits_t = _outproj(o_pats, lse_pats, dils, o_b, x, row(g_out_swa), row(g_out_mla),
                                 w_o.astype(BF16), gate_a, row(norm_ffn_g), scale_f, shift_f, w_router.T)

    eidx, wts, rank, cnt = _route(logits_t, router_bias)
    tr = TR_EXPERT
    counts = cnt[:, 0].astype(I32)
    padded = (counts + tr - 1) // tr * tr
    e_ids = jnp.arange(N_EXPERTS, dtype=I32)
    pad_end = jnp.sum(jnp.where(e_ids[None, :] <= e_ids[:, None], padded[None, :], 0), axis=1)
    pad_start = pad_end - padded
    lookup = lambda table, idx: jnp.sum(jnp.where(idx[..., None] == e_ids, table, 0), axis=-1)
    dest = (lookup(pad_start, eidx) + rank).T.reshape(-1)
    n_slots = s * TOP_K + N_EXPERTS * tr
    blk_start = jnp.arange(n_slots // tr, dtype=I32) * tr
    blk_e = jnp.minimum(jnp.sum((pad_end[None, :] <= blk_start[:, None]).astype(I32), axis=1), N_EXPERTS - 1)
    blk_nv = jnp.clip(lookup(counts, blk_e) - (blk_start - lookup(pad_start, blk_e)), 0, tr)
    blk_new = ((blk_nv > 0) & (blk_start == lookup(pad_start, blk_e))).astype(I32)
    used = counts > 0
    later_used = used[None, :] & (e_ids[None, :] > e_ids[:, None])
    next_used = jnp.min(jnp.where(later_used, e_ids[None, :], N_EXPERTS), axis=1)
    next_used = jnp.where(next_used < N_EXPERTS, next_used, -1)
    used_before = jnp.sum((used[None, :] & (e_ids[None, :] < e_ids[:, None])).astype(I32), axis=1)
    blk_par = lookup(used_before % 2, blk_e)
    blk_nxt = lookup(next_used, blk_e)

    xs = _dispatch(dest, h2p, pad_start + counts, padded - counts, n_slots, tr)
    y = _experts(blk_e, blk_nv, blk_new, blk_par, blk_nxt, xs, w_exp_gate, w_exp_up, w_exp_down)
    w_sgu = jnp.concatenate([w_sh_gate, w_sh_up], axis=1).astype(BF16)
    return _combine(dest, wts.T, h2p, x1, gate_f, row(final_g), w_sgu, w_sh_down.astype(BF16), y)


def kernel(x, c, positions, norm_attn_g, w_ada, b_ada, w_in, g_q, w_uq, g_kv, w_ukv, g_out_swa, g_out_mla, w_o,
           norm_ffn_g, w_router, router_bias, w_exp_gate, w_exp_up, w_exp_down, w_sh_gate, w_sh_up, w_sh_down,
           final_norm_g):
    assert x.shape[0] == 1 and w_ada.shape[0] == 1
    out = _layer(x[0], c[0], positions[0], norm_attn_g[0], w_ada[0], b_ada[0], w_in[0], g_q[0], w_uq[0], g_kv[0],
                 w_ukv[0], g_out_swa[0], g_out_mla[0], w_o[0], norm_ffn_g[0], w_router[0], router_bias[0],
                 w_exp_gate[0], w_exp_up[0], w_exp_down[0], w_sh_gate[0], w_sh_up[0], w_sh_down[0], final_norm_g)
    return out[None]
```
